```python
import math
import jax, jax.numpy as jnp
from jax import lax
import numpy as np

D_MODEL = 1024
BATCH = 4
SEQ = 4096
DEPTH = 1

MEM_LEN = 256
EPS = 1e-6
M_HEADS = 4
M_INNER = 2 * D_MODEL
M_HD = M_INNER // M_HEADS
M_CONV = 4
M_CHUNK = 128
A_HEADS = 8
A_QK = D_MODEL // 16
A_V = 2 * A_QK
A_WIDTH = A_HEADS * A_V
Q_BLOCK = 128
X_HEADS = 4
X_HD = D_MODEL // X_HEADS
X_WIDTH = X_HEADS * X_HD
REL_BUCKETS = 32
REL_MAX_DIST = 128
D_FF = 2816
N_BRANCH = 3
IN_SIZES = [M_INNER, M_INNER, M_HEADS, M_HEADS,
            A_HEADS * 2 * A_QK, A_HEADS * 2 * A_QK, A_WIDTH,
            X_WIDTH, N_BRANCH * D_MODEL]
IN_WIDTH = sum(IN_SIZES)
IN_SPLITS = [int(s) for s in np.cumsum(IN_SIZES)[:-1]]

kernel_name = 'hybrid_mlstm_diffattn_xattn_block'


def rmsnorm(x, g):
    xf = x.astype(jnp.float32)
    y = xf * lax.rsqrt(jnp.mean(xf * xf, axis=-1, keepdims=True) + EPS)
    return (y * g.astype(jnp.float32)).astype(x.dtype)


def head_rmsnorm(h, g):
    B, H, S, d = h.shape
    hf = h.astype(jnp.float32)
    hn = hf * lax.rsqrt(jnp.mean(hf * hf, axis=-1, keepdims=True) + EPS)
    hn = jnp.transpose(hn, (0, 2, 1, 3)).reshape(B, S, H * d)
    return hn * g.astype(jnp.float32)


def swiglu(h, w_gu, w_down):
    gate, up = jnp.split(h @ w_gu, 2, axis=-1)
    return (jax.nn.silu(gate) * up) @ w_down


def causal_conv(x, w, b):
    C = x.shape[-1]
    y = lax.conv_general_dilated(x, w[:, None, :].astype(x.dtype), window_strides=(1,),
                                 padding=[(M_CONV - 1, 0)],
                                 dimension_numbers=('NWC', 'WIO', 'NWC'),
                                 feature_group_count=C)
    return y + b


def headwise(x, w):
    B, S, _ = x.shape
    H, d, _ = w.shape
    return jnp.einsum('bshd,hde->bhse', x.reshape(B, S, H, d), w)


def mlstm_chunkwise(q, k, v, ig, lf):
    B, H, S, d = q.shape
    L = M_CHUNK
    NC = S // L

    def chunks(a):
        return jnp.moveaxis(a.reshape((B, H, NC, L) + a.shape[3:]), 2, 0)

    causal = jnp.tril(jnp.ones((L, L), dtype=bool))

    def step(carry, inp):
        C, n, m = carry
        qb, kb, vb, ib, fb = inp
        b = jnp.cumsum(fb, axis=-1)
        dlog = b[..., :, None] - b[..., None, :] + ib[..., None, :]
        dlog = jnp.where(causal, dlog, -jnp.inf)
        inter = b + m[..., None]
        m_t = jnp.maximum(inter, jnp.max(dlog, axis=-1))
        w_intra = jnp.exp(dlog - m_t[..., None])
        w_inter = jnp.exp(inter - m_t)
        s = jnp.einsum('bhtd,bhsd->bhts', qb, kb) * w_intra
        num = jnp.einsum('bhts,bhse->bhte', s, vb) + w_inter[..., None] * jnp.einsum('bhtd,bhde->bhte', qb, C)
        den = jnp.sum(s, axis=-1) + w_inter * jnp.einsum('bhtd,bhd->bht', qb, n)
        h = num / jnp.maximum(jnp.abs(den), jnp.exp(-m_t))[..., None]
        bL = b[..., -1]
        tail = bL[..., None] - b + ib
        m_new = jnp.maximum(bL + m, jnp.max(tail, axis=-1))
        w_s = jnp.exp(tail - m_new[..., None])
        decay = jnp.exp(bL + m - m_new)
        C_new = decay[..., None, None] * C + jnp.einsum('bhs,bhsd,bhse->bhde', w_s, kb, vb)
        n_new = decay[..., None] * n + jnp.einsum('bhs,bhsd->bhd', w_s, kb)
        return (C_new, n_new, m_new), h

    init = (jnp.zeros((B, H, d, d), jnp.float32), jnp.zeros((B, H, d), jnp.float32),
            jnp.zeros((B, H), jnp.float32))
    _, hc = lax.scan(step, init, (chunks(q), chunks(k), chunks(v), chunks(ig), chunks(lf)))
    return jnp.moveaxis(hc, 0, 2).reshape(B, H, S, v.shape[-1])


def t5_bucket(rel):
    n = jnp.maximum(-rel, 0)
    max_exact = REL_BUCKETS // 2
    nf = jnp.maximum(n, 1).astype(jnp.float32)
    large = max_exact + (jnp.log(nf / max_exact) / math.log(REL_MAX_DIST / max_exact)
                         * (REL_BUCKETS - max_exact)).astype(jnp.int32)
    large = jnp.minimum(large, REL_BUCKETS - 1)
    return jnp.where(n < max_exact, n, large)


def diff_attention(q1, q2, k1, k2, v, positions, rel_bias, lam):
    B, H, S, dq = q1.shape
    NB = S // Q_BLOCK
    scale = dq ** -0.5

    def to_blocks(a):
        return jnp.moveaxis(a.reshape(B, H, NB, Q_BLOCK, a.shape[-1]), 2, 0)

    def block(inp):
        q1b, q2b, pq = inp
        rel = positions[None, :] - pq[:, None]
        bias = jnp.transpose(rel_bias[t5_bucket(rel)], (2, 0, 1))[None].astype(jnp.float32)
        visible = rel <= 0

        def probs(qb, kk):
            s = jnp.einsum('bhqd,bhkd->bhqk', qb, kk).astype(jnp.float32) * scale + bias
            return jax.nn.softmax(jnp.where(visible, s, -jnp.inf), axis=-1)

        a = probs(q1b, k1) - lam * probs(q2b, k2)
        return jnp.einsum('bhqk,bhkd->bhqd', a.astype(v.dtype), v)

    out = lax.map(block, (to_blocks(q1), to_blocks(q2), positions.reshape(NB, Q_BLOCK)))
    return jnp.moveaxis(out, 0, 2).reshape(B, H, S, v.shape[-1])


def cross_attention(qx, mem_n, w_mem_kv):
    B, S, _ = qx.shape
    kx, vx = jnp.split(mem_n @ w_mem_kv, 2, axis=-1)
    q = qx.reshape(B, S, X_HEADS, X_HD)
    k = kx.reshape(B, MEM_LEN, X_HEADS, X_HD)
    v = vx.reshape(B, MEM_LEN, X_HEADS, X_HD)
    s = jnp.einsum('bshd,bmhd->bhsm', q, k).astype(jnp.float32) * X_HD ** -0.5
    p = jax.nn.softmax(s, axis=-1).astype(v.dtype)
    return jnp.einsum('bhsm,bmhd->bshd', p, v).reshape(B, S, X_WIDTH)


def setup_inputs(seed: int = 0) -> dict:
    key = jax.random.key(seed)
    ks = iter(jax.random.split(key, 40))

    def nrm(shape, scale):
        return jax.random.normal(next(ks), shape, jnp.float32) * scale

    def gain(shape):
        return 1.0 + nrm(shape, 0.02)

    L, D = DEPTH, D_MODEL
    f_bias = jnp.broadcast_to(jnp.linspace(3.0, 6.0, M_HEADS, dtype=jnp.float32), (L, M_HEADS))
    b_if = jnp.concatenate([nrm((L, M_HEADS), 0.1), f_bias + nrm((L, M_HEADS), 0.1)], axis=-1)
    offset = jax.random.randint(next(ks), (), 0, 1024, dtype=jnp.int32)
    return {
        'x': nrm((BATCH, SEQ, D), 1.0),
        'mem': nrm((BATCH, MEM_LEN, D), 1.0),
        'positions': offset + jnp.arange(SEQ, dtype=jnp.int32),
        'rel_bias': nrm((REL_BUCKETS, A_HEADS), 0.5),
        'ffn1_norm': gain((L, D)),
        'ffn1_w_gu': nrm((L, D, 2 * D_FF), D ** -0.5),
        'ffn1_w_down': nrm((L, D_FF, D), D_FF ** -0.5),
        'mix_norm': gain((L, D)),
        'mem_norm': gain((L, D)),
        'w_in': nrm((L, D, IN_WIDTH), D ** -0.5),
        'b_if': b_if,
        'b_gate': nrm((L, N_BRANCH * D), 0.01),
        'm_conv_w': nrm((L, M_CONV, M_INNER), 0.5),
        'm_conv_b': nrm((L, M_INNER), 0.01),
        'm_wq': nrm((L, M_HEADS, M_HD, M_HD), M_HD ** -0.5),
        'm_wk': nrm((L, M_HEADS, M_HD, M_HD), M_HD ** -0.5),
        'm_wv': nrm((L, M_HEADS, M_HD, M_HD), M_HD ** -0.5),
        'm_head_norm': gain((L, M_INNER)),
        'a_lam_q1': nrm((L, A_QK), 0.1),
        'a_lam_k1': nrm((L, A_QK), 0.1),
        'a_lam_q2': nrm((L, A_QK), 0.1),
        'a_lam_k2': nrm((L, A_QK), 0.1),
        'a_head_norm': gain((L, A_WIDTH)),
        'w_mem_kv': nrm((L, D, 2 * X_WIDTH), D ** -0.5),
        'w_branch_m': nrm((L, M_INNER, D), M_INNER ** -0.5),
        'w_branch_a': nrm((L, A_WIDTH, D), A_WIDTH ** -0.5),
        'w_branch_x': nrm((L, X_WIDTH, D), X_WIDTH ** -0.5),
        'w_out': nrm((L, D, D), D ** -0.5),
        'ffn2_norm': gain((L, D)),
        'ffn2_w_gu': nrm((L, D, 2 * D_FF), D ** -0.5),
        'ffn2_w_down': nrm((L, D_FF, D), D_FF ** -0.5),
        'final_norm': gain((D,)),
    }


def reference(x, mem, positions, rel_bias, ffn1_norm, ffn1_w_gu, ffn1_w_down, mix_norm, mem_norm,
              w_in, b_if, b_gate, m_conv_w, m_conv_b, m_wq, m_wk, m_wv, m_head_norm,
              a_lam_q1, a_lam_k1, a_lam_q2, a_lam_k2, a_head_norm, w_mem_kv,
              w_branch_m, w_branch_a, w_branch_x, w_out, ffn2_norm, ffn2_w_gu, ffn2_w_down,
              final_norm):
    B, S, D = x.shape
    for l in range(DEPTH):
        x = x + 0.5 * swiglu(rmsnorm(x, ffn1_norm[l]), ffn1_w_gu[l], ffn1_w_down[l])

        h = rmsnorm(x, mix_norm[l])
        (xm, o_pre, i_pre, f_pre, qa, ka, va, qx, g_pre) = jnp.split(h @ w_in[l], IN_SPLITS, axis=-1)

        xc = jax.nn.silu(causal_conv(xm, m_conv_w[l], m_conv_b[l]))
        q_m = headwise(xc, m_wq[l]).astype(jnp.float32)
        k_m = headwise(xc, m_wk[l]).astype(jnp.float32) * M_HD ** -0.5
        v_m = headwise(xm, m_wv[l]).astype(jnp.float32)
        gates_if = (jnp.concatenate([i_pre, f_pre], axis=-1) + b_if[l]).astype(jnp.float32)
        ig = jnp.transpose(gates_if[..., :M_HEADS], (0, 2, 1))
        lf = jnp.transpose(jax.nn.log_sigmoid(gates_if[..., M_HEADS:]), (0, 2, 1))
        h_m = mlstm_chunkwise(q_m, k_m, v_m, ig, lf)
        y_m = (jax.nn.sigmoid(o_pre.astype(jnp.float32)) * head_rmsnorm(h_m, m_head_norm[l])).astype(x.dtype)

        qa = jnp.transpose(qa.reshape(B, S, A_HEADS, 2, A_QK), (3, 0, 2, 1, 4))
        ka = jnp.transpose(ka.reshape(B, S, A_HEADS, 2, A_QK), (3, 0, 2, 1, 4))
        va = jnp.transpose(va.reshape(B, S, A_HEADS, A_V), (0, 2, 1, 3))
        lam_init = 0.8 - 0.6 * math.exp(-0.3 * l)
        lam = (jnp.exp(jnp.sum(a_lam_q1[l] * a_lam_k1[l]).astype(jnp.float32))
               - jnp.exp(jnp.sum(a_lam_q2[l] * a_lam_k2[l]).astype(jnp.float32)) + lam_init)
        h_a = diff_attention(qa[0], qa[1], ka[0], ka[1], va, positions, rel_bias, lam)
        y_a = (head_rmsnorm(h_a, a_head_norm[l]) * (1.0 - lam_init)).astype(x.dtype)

        y_x = cross_attention(qx, rmsnorm(mem, mem_norm[l]), w_mem_kv[l])

        g = jax.nn.sigmoid(g_pre + b_gate[l]).reshape(B, S, N_BRANCH, D)
        merged = (g[:, :, 0] * (y_m @ w_branch_m[l]) + g[:, :, 1] * (y_a @ w_branch_a[l])
                  + g[:, :, 2] * (y_x @ w_branch_x[l]))
        x = x + merged @ w_out[l]

        x = x + 0.5 * swiglu(rmsnorm(x, ffn2_norm[l]), ffn2_w_gu[l], ffn2_w_down[l])
    return rmsnorm(x, final_norm)
```

```python
import functools
import math

import jax
import jax.numpy as jnp
from jax import lax
from jax.experimental import pallas as pl
from jax.experimental.pallas import tpu as pltpu

EPS = 1e-6
BF16 = jnp.bfloat16
F32 = jnp.float32
NEG_BIG = -1e30

M_HEADS = 4
M_CONV = 4
A_HEADS = 8
X_HEADS = 4
N_BRANCH = 3
REL_BUCKETS = 32
REL_MAX_DIST = 128
LAYER = 0

VMEM_LIMIT_BYTES = 56 * 1024 * 1024

FFN_ROWS = 512
INPROJ_ROWS = 1024
INPROJ_COLS = 1024
QKV_ROWS = 512
MLSTM_CHUNK = 256
ATTN_TILE = 256
MERGE_ROWS = 512
CONV_HALO = 16


def _params(*sem):
    return pltpu.CompilerParams(dimension_semantics=sem, vmem_limit_bytes=VMEM_LIMIT_BYTES)


def _resident(shape):
    zeros = (0,) * len(shape)
    return pl.BlockSpec(shape, lambda *_: zeros, pipeline_mode=pl.Buffered(1))


def _rms(x):
    return x * lax.rsqrt(jnp.mean(x * x, axis=-1, keepdims=True) + EPS)


def _sigmoid(x):
    return 1.0 / (1.0 + jnp.exp(-x))


def _ffn_kernel(x_ref, g_ref, wgu_ref, wd_ref, fg_ref, o_ref, *, d_ff, final_norm):
    x = x_ref[...]
    hb = (_rms(x) * g_ref[...]).astype(BF16)
    gate = jnp.dot(hb, wgu_ref[:, :d_ff], preferred_element_type=F32)
    up = jnp.dot(hb, wgu_ref[:, d_ff:], preferred_element_type=F32)
    act = (gate * _sigmoid(gate) * up).astype(BF16)
    y = x + 0.5 * jnp.dot(act, wd_ref[...], preferred_element_type=F32)
    if final_norm:
        y = _rms(y) * fg_ref[...]
    o_ref[...] = y


def _ffn(x, gain, w_gu, w_down, final_gain, *, final_norm):
    n, d = x.shape
    d_ff = w_down.shape[0]
    tm = min(FFN_ROWS, n)
    return pl.pallas_call(
        functools.partial(_ffn_kernel, d_ff=d_ff, final_norm=final_norm),
        grid=(n // tm,),
        in_specs=[
            pl.BlockSpec((tm, d), lambda i: (i, 0)),
            _resident((1, d)),
            _resident((d, 2 * d_ff)),
            _resident((d_ff, d)),
            _resident((1, d)),
        ],
        out_specs=pl.BlockSpec((tm, d), lambda i: (i, 0)),
        out_shape=jax.ShapeDtypeStruct((n, d), F32),
        compiler_params=_params("parallel"),
        name="ffn_final" if final_norm else "ffn",
    )(x, gain.reshape(1, d), w_gu, w_down, final_gain.reshape(1, d))


def _inproj_kernel(x_ref, g_ref, w_ref, wgate_ref, p_ref, gates_ref, h_scr):
    @pl.when(pl.program_id(1) == 0)
    def _():
        hb = (_rms(x_ref[...]) * g_ref[...]).astype(BF16)
        h_scr[...] = hb
        gates_ref[...] = jnp.dot(hb, wgate_ref[...], preferred_element_type=F32)

    p_ref[...] = jnp.dot(h_scr[...], w_ref[...], preferred_element_type=F32).astype(BF16)


def _inproj(x, gain, w_main, w_gate):
    n, d = x.shape
    width = w_main.shape[1]
    tm = min(INPROJ_ROWS, n)
    tn = INPROJ_COLS
    return pl.pallas_call(
        _inproj_kernel,
        grid=(n // tm, width // tn),
        in_specs=[
            pl.BlockSpec((tm, d), lambda i, j: (i, 0)),
            _resident((1, d)),
            pl.BlockSpec((d, tn), lambda i, j: (0, j)),
            _resident(w_gate.shape),
        ],
        out_specs=[
            pl.BlockSpec((tm, tn), lambda i, j: (i, j)),
            pl.BlockSpec((tm, w_gate.shape[1]), lambda i, j: (i, 0)),
        ],
        out_shape=[
            jax.ShapeDtypeStruct((n, width), BF16),
            jax.ShapeDtypeStruct((n, w_gate.shape[1]), F32),
        ],
        scratch_shapes=[pltpu.VMEM((tm, d), BF16)],
        compiler_params=_params("parallel", "arbitrary"),
        name="inproj",
    )(x, gain.reshape(1, d), w_main, w_gate)


def _mlstm_qkv_kernel(x_ref, halo_ref, cw_ref, cb_ref, wq_ref, wk_ref, wv_ref,
                      q_ref, k_ref, v_ref, *, k_scale):
    ts = x_ref.shape[0]
    xb = x_ref[...]
    cur = xb.astype(F32)
    prev = jnp.where(pl.program_id(2) > 0, halo_ref[...].astype(F32), 0.0)
    full = jnp.concatenate([prev, cur], axis=0)
    cw = cw_ref[...]
    conv = cb_ref[...] + cw[M_CONV - 1:M_CONV, :] * cur
    for back in range(1, M_CONV):
        tap = M_CONV - 1 - back
        conv = conv + cw[tap:tap + 1, :] * full[CONV_HALO - back:CONV_HALO - back + ts, :]
    xc = (conv * _sigmoid(conv)).astype(BF16)
    q_ref[...] = jnp.dot(xc, wq_ref[0], preferred_element_type=F32).astype(BF16)
    k_ref[...] = (jnp.dot(xc, wk_ref[0], preferred_element_type=F32) * k_scale).astype(BF16)
    v_ref[...] = jnp.dot(xb, wv_ref[0], preferred_element_type=F32).astype(BF16)


def _mlstm_qkv(proj, conv_w, conv_b, wq, wk, wv, *, batch, seq):
    heads, hd, _ = wq.shape
    n = batch * seq
    ts = min(QKV_ROWS, seq)
    nt = seq // ts
    halo_per_tile = ts // CONV_HALO

    def x_map(h, b, t):
        return (b * nt + t, h)

    def halo_map(h, b, t):
        return (jnp.maximum((b * nt + t) * halo_per_tile - 1, 0), h)

    def w_map(h, b, t):
        return (h, 0, 0)

    out = jax.ShapeDtypeStruct((n, heads * hd), BF16)
    return pl.pallas_call(
        functools.partial(_mlstm_qkv_kernel, k_scale=hd ** -0.5),
        grid=(heads, batch, nt),
        in_specs=[
            pl.BlockSpec((ts, hd), x_map),
            pl.BlockSpec((CONV_HALO, hd), halo_map),
            pl.BlockSpec((M_CONV, hd), lambda h, b, t: (0, h)),
            pl.BlockSpec((1, hd), lambda h, b, t: (0, h)),
            pl.BlockSpec((1, hd, hd), w_map),
            pl.BlockSpec((1, hd, hd), w_map),
            pl.BlockSpec((1, hd, hd), w_map),
        ],
        out_specs=[pl.BlockSpec((ts, hd), x_map)] * 3,
        out_shape=[out, out, out],
        compiler_params=_params("parallel", "parallel", "parallel"),
        name="mlstm_qkv",
    )(proj, proj, conv_w, conv_b.reshape(1, -1), wq, wk, wv)


def _mlstm_scan_kernel(q_ref, k_ref, v_ref, g_ref, bias_ref, o_ref, gain_ref, y_ref,
                       c_scr, n_scr, m_scr):
    @pl.when(pl.program_id(2) == 0)
    def _():
        c_scr[...] = jnp.zeros_like(c_scr)
        n_scr[...] = jnp.zeros_like(n_scr)
        m_scr[...] = jnp.zeros_like(m_scr)

    chunk = q_ref.shape[0]
    q = q_ref[...]
    k = k_ref[...]
    v = v_ref[...]
    gates = g_ref[0, 0] + bias_ref[0]
    ig = gates[0:1, :]
    fpre = gates[1:2, :]
    lf = jnp.minimum(fpre, 0.0) - jnp.log1p(jnp.exp(-jnp.abs(fpre)))

    t_idx = lax.broadcasted_iota(jnp.int32, (chunk, chunk), 0)
    s_idx = lax.broadcasted_iota(jnp.int32, (chunk, chunk), 1)
    causal = s_idx <= t_idx
    diag = s_idx == t_idx

    def to_col(row):
        return jnp.sum(jnp.where(diag, row, 0.0), axis=1, keepdims=True)

    def to_row(col):
        return jnp.sum(jnp.where(diag, col, 0.0), axis=0, keepdims=True)

    b_col = jnp.sum(jnp.where(causal, lf, 0.0), axis=1, keepdims=True)
    b_row = to_row(b_col)
    m_prev = m_scr[...]
    dlog = jnp.where(causal, b_col - b_row + ig, NEG_BIG)
    inter = b_col + m_prev
    m_t = jnp.maximum(inter, jnp.max(dlog, axis=1, keepdims=True))
    w_intra = jnp.exp(dlog - m_t)
    w_inter = jnp.exp(inter - m_t)

    s = lax.dot_general(q, k, (((1,), (1,)), ((), ())), preferred_element_type=F32) * w_intra
    c_prev = c_scr[...]
    n_prev = n_scr[...]
    num = (jnp.dot(s.astype(BF16), v, preferred_element_type=F32)
           + w_inter * jnp.dot(q, c_prev.astype(BF16), preferred_element_type=F32))
    qn = jnp.sum(q.astype(F32) * n_prev, axis=1, keepdims=True)
    den = jnp.sum(s, axis=1, keepdims=True) + w_inter * qn
    h = num / jnp.maximum(jnp.abs(den), jnp.exp(-m_t))

    b_last = b_col[chunk - 1:chunk, :]
    tail = b_last - b_row + ig
    m_new = jnp.maximum(b_last + m_prev, jnp.max(tail, axis=1, keepdims=True))
    decay = jnp.exp(b_last + m_prev - m_new)
    w_s = jnp.exp(to_col(tail) - m_new)
    kw = k.astype(F32) * w_s
    c_scr[...] = decay * c_prev + jnp.dot(kw.T.astype(BF16), v, preferred_element_type=F32)
    n_scr[...] = decay * n_prev + jnp.sum(kw, axis=0, keepdims=True)
    m_scr[...] = m_new

    gate = _sigmoid(o_ref[...].astype(F32))
    y_ref[...] = (gate * (_rms(h) * gain_ref[...])).astype(BF16)


def _mlstm_scan(q, k, v, gates_t, b_if, proj, o_col_block, head_gain, *, batch, seq):
    n, width = q.shape
    heads = gates_t.shape[1]
    hd = width // heads
    chunk = min(MLSTM_CHUNK, seq)
    nc = seq // chunk

    def qkv_map(b, h, c):
        return (b * nc + c, h)

    return pl.pallas_call(
        _mlstm_scan_kernel,
        grid=(batch, heads, nc),
        in_specs=[
            pl.BlockSpec((chunk, hd), qkv_map),
            pl.BlockSpec((chunk, hd), qkv_map),
            pl.BlockSpec((chunk, hd), qkv_map),
            pl.BlockSpec((1, 1, 2, chunk), lambda b, h, c: (b, h, 0, c)),
            pl.BlockSpec((1, 2, 1), lambda b, h, c: (h, 0, 0)),
            pl.BlockSpec((chunk, hd), lambda b, h, c: (b * nc + c, o_col_block + h)),
            pl.BlockSpec((1, hd), lambda b, h, c: (0, h)),
        ],
        out_specs=pl.BlockSpec((chunk, hd), qkv_map),
        out_shape=jax.ShapeDtypeStruct((n, width), BF16),
        scratch_shapes=[pltpu.VMEM((hd, hd), F32), pltpu.VMEM((1, hd), F32), pltpu.VMEM((1, 1), F32)],
        compiler_params=_params("parallel", "parallel", "arbitrary"),
        name="mlstm_scan",
    )(q, k, v, gates_t, b_if.reshape(2, heads, 1).transpose(1, 0, 2), proj, head_gain.reshape(1, -1))


def _diff_attn_kernel(far_ref, q_ref, k_ref, v_ref, band_ref, lam_ref, gain_ref, y_ref,
                      m_scr, l_scr, acc_scr, *, out_scale, lam_init):
    head = pl.program_id(1)
    qi = pl.program_id(2)
    tile, width = q_ref.shape
    half = width // 2

    q = q_ref[...]
    lane = lax.broadcasted_iota(jnp.int32, q.shape, 1)
    zero = jnp.zeros_like(q)
    qs = jnp.concatenate([jnp.where(lane < half, q, zero), jnp.where(lane >= half, q, zero)], axis=0)
    qs = qs * (half ** -0.5)

    m_scr[...] = jnp.full_like(m_scr, NEG_BIG)
    l_scr[...] = jnp.zeros_like(l_scr)
    acc_scr[...] = jnp.zeros_like(acc_scr)

    def scores(kj):
        start = pl.multiple_of(kj * tile, tile)
        kb = k_ref[pl.ds(start, tile), :]
        vb = v_ref[pl.ds(start, tile), :]
        s = lax.dot_general(qs, kb, (((1,), (1,)), ((), ())), preferred_element_type=F32)
        return s, vb

    def update(s, vb):
        m_prev = m_scr[...]
        m_new = jnp.maximum(m_prev, jnp.max(s, axis=1, keepdims=True))
        alpha = jnp.exp(m_prev - m_new)
        p = jnp.exp(s - m_new)
        l_scr[...] = alpha * l_scr[...] + jnp.sum(p, axis=1, keepdims=True)
        acc_scr[...] = alpha * acc_scr[...] + jnp.dot(p.astype(BF16), vb, preferred_element_type=F32)
        m_scr[...] = m_new

    far_bias = far_ref[head]

    def far_body(kj, carry):
        s, vb = scores(kj)
        update(s + far_bias, vb)
        return carry

    lax.fori_loop(0, jnp.maximum(qi - 1, 0), far_body, 0)

    @pl.when(qi > 0)
    def _():
        s, vb = scores(qi - 1)
        s = s + jnp.concatenate([band_ref[0, 0], band_ref[0, 0]], axis=0)
        update(s, vb)

    s, vb = scores(qi)
    s = s + jnp.concatenate([band_ref[0, 1], band_ref[0, 1]], axis=0)
    update(s, vb)

    lam_v = lam_ref[...]
    lam = (jnp.exp(jnp.sum(lam_v[0:1, :] * lam_v[1:2, :], axis=1, keepdims=True))
           - jnp.exp(jnp.sum(lam_v[2:3, :] * lam_v[3:4, :], axis=1, keepdims=True)) + lam_init)
    o = acc_scr[...] / l_scr[...]
    out = o[:tile, :] - lam * o[tile:, :]
    y_ref[...] = (_rms(out) * gain_ref[...] * out_scale).astype(BF16)


def _t5_bucket(n):
    max_exact = REL_BUCKETS // 2
    nf = jnp.maximum(n, 1).astype(F32)
    large = max_exact + (jnp.log(nf / max_exact) / math.log(REL_MAX_DIST / max_exact)
                         * (REL_BUCKETS - max_exact)).astype(jnp.int32)
    large = jnp.minimum(large, REL_BUCKETS - 1)
    return jnp.where(n < max_exact, n, large)


def _diff_attn(proj, q_col, k_col, v_col, rel_bias, lam_vecs, head_gain, *, batch, seq):
    n = proj.shape[0]
    heads = rel_bias.shape[1]
    hw = head_gain.shape[0] // heads
    tile = min(ATTN_TILE, seq)
    nq = seq // tile
    assert tile >= REL_MAX_DIST, "distances beyond one tile must all fall in the last bucket"

    t = jnp.arange(tile, dtype=jnp.int32)[:, None]
    s = jnp.arange(tile, dtype=jnp.int32)[None, :]
    dist = jnp.stack([tile + t - s, t - s])
    band = jnp.transpose(rel_bias[_t5_bucket(jnp.maximum(dist, 0))], (3, 0, 1, 2)).astype(F32)
    band = jnp.where(dist[None] >= 0, band, NEG_BIG)
    far = rel_bias[REL_BUCKETS - 1].astype(F32)

    lam_init = 0.8 - 0.6 * math.exp(-0.3 * LAYER)
    return pl.pallas_call(
        functools.partial(_diff_attn_kernel, out_scale=1.0 - lam_init, lam_init=lam_init),
        grid=(batch, heads, nq),
        in_specs=[
            pl.BlockSpec(memory_space=pltpu.SMEM),
            pl.BlockSpec((tile, hw), lambda b, h, i: (b * nq + i, q_col + h)),
            pl.BlockSpec((seq, hw), lambda b, h, i: (b, k_col + h)),
            pl.BlockSpec((seq, hw), lambda b, h, i: (b, v_col + h)),
            pl.BlockSpec((1, 2, tile, tile), lambda b, h, i: (h, 0, 0, 0)),
            _resident(lam_vecs.shape),
            pl.BlockSpec((1, hw), lambda b, h, i: (0, h)),
        ],
        out_specs=pl.BlockSpec((tile, hw), lambda b, h, i: (b * nq + i, h)),
        out_shape=jax.ShapeDtypeStruct((n, heads * hw), BF16),
        scratch_shapes=[
            pltpu.VMEM((2 * tile, 1), F32),
            pltpu.VMEM((2 * tile, 1), F32),
            pltpu.VMEM((2 * tile, hw), F32),
        ],
        compiler_params=_params("parallel", "parallel", "parallel"),
        name="diff_attn",
    )(far, proj, proj, proj, band, lam_vecs, head_gain.reshape(1, -1))


def _mem_kv_kernel(mem_ref, g_ref, w_ref, k_ref, v_ref):
    width = k_ref.shape[-1]
    hb = (_rms(mem_ref[0]) * g_ref[...]).astype(BF16)
    k_ref[0] = jnp.dot(hb, w_ref[:, :width], preferred_element_type=F32).astype(BF16)
    v_ref[0] = jnp.dot(hb, w_ref[:, width:], preferred_element_type=F32).astype(BF16)


def _mem_kv(mem, gain, w_kv):
    batch, mlen, d = mem.shape
    width = w_kv.shape[1] // 2
    out = jax.ShapeDtypeStruct((batch, mlen, width), BF16)
    return pl.pallas_call(
        _mem_kv_kernel,
        grid=(batch,),
        in_specs=[
            pl.BlockSpec((1, mlen, d), lambda b: (b, 0, 0)),
            _resident((1, d)),
            _resident(w_kv.shape),
        ],
        out_specs=[pl.BlockSpec((1, mlen, width), lambda b: (b, 0, 0))] * 2,
        out_shape=[out, out],
        compiler_params=_params("parallel"),
        name="mem_kv",
    )(mem, gain.reshape(1, d), w_kv)


def _merge_kernel(x_ref, qx_ref, g0_ref, g1_ref, g2_ref, ym_ref, ya_ref, kx_ref, vx_ref,
                  bg_ref, wm_ref, wa_ref, wx_ref, wo_ref, o_ref):
    d = x_ref.shape[1]
    xw = qx_ref.shape[1]
    xhd = xw // X_HEADS
    heads = []
    for h in range(X_HEADS):
        cols = slice(h * xhd, (h + 1) * xhd)
        s = lax.dot_general(qx_ref[:, cols], kx_ref[0, :, cols], (((1,), (1,)), ((), ())),
                            preferred_element_type=F32) * (xhd ** -0.5)
        p = jnp.exp(s - jnp.max(s, axis=1, keepdims=True))
        pv = jnp.dot(p.astype(BF16), vx_ref[0, :, cols], preferred_element_type=F32)
        heads.append((pv / jnp.sum(p, axis=1, keepdims=True)).astype(BF16))
    yx = jnp.concatenate(heads, axis=1)

    def gate(g_ref, k):
        return _sigmoid(g_ref[...].astype(F32) + bg_ref[:, k * d:(k + 1) * d])

    merged = (gate(g0_ref, 0) * jnp.dot(ym_ref[...], wm_ref[...], preferred_element_type=F32)
              + gate(g1_ref, 1) * jnp.dot(ya_ref[...], wa_ref[...], preferred_element_type=F32)
              + gate(g2_ref, 2) * jnp.dot(yx, wx_ref[...], preferred_element_type=F32))
    o_ref[...] = x_ref[...] + jnp.dot(merged.astype(BF16), wo_ref[...], preferred_element_type=F32)


def _merge(x, proj, qx_col, g_col, y_m, y_a, kx, vx, b_gate, w_m, w_a, w_x, w_o, *, batch, seq):
    n, d = x.shape
    tm = min(MERGE_ROWS, seq)
    nt = seq // tm

    def row(b, t):
        return b * nt + t

    def proj_spec(col):
        return pl.BlockSpec((tm, d), lambda b, t: (row(b, t), col))

    def mem_spec(a):
        return pl.BlockSpec((1,) + a.shape[1:], lambda b, t: (b, 0, 0))

    return pl.pallas_call(
        _merge_kernel,
        grid=(batch, nt),
        in_specs=[
            pl.BlockSpec((tm, d), lambda b, t: (row(b, t), 0)),
            proj_spec(qx_col), proj_spec(g_col), proj_spec(g_col + 1), proj_spec(g_col + 2),
            pl.BlockSpec((tm, y_m.shape[1]), lambda b, t: (row(b, t), 0)),
            pl.BlockSpec((tm, y_a.shape[1]), lambda b, t: (row(b, t), 0)),
            mem_spec(kx), mem_spec(vx),
            _resident((1, b_gate.shape[0])),
            _resident(w_m.shape), _resident(w_a.shape), _resident(w_x.shape), _resident(w_o.shape),
        ],
        out_specs=pl.BlockSpec((tm, d), lambda b, t: (row(b, t), 0)),
        out_shape=jax.ShapeDtypeStruct((n, d), F32),
        compiler_params=_params("parallel", "parallel"),
        name="merge",
    )(x, proj, proj, proj, proj, y_m, y_a, kx, vx, b_gate.reshape(1, -1), w_m, w_a, w_x, w_o)


def kernel(x, mem, positions, rel_bias, ffn1_norm, ffn1_w_gu, ffn1_w_down, mix_norm, mem_norm, w_in, b_if, b_gate, m_conv_w, m_conv_b, m_wq, m_wk, m_wv, m_head_norm, a_lam_q1, a_lam_k1, a_lam_q2, a_lam_k2, a_head_norm, w_mem_kv, w_branch_m, w_branch_a, w_branch_x, w_out, ffn2_norm, ffn2_w_gu, ffn2_w_down, final_norm):
    del positions
    batch, seq, d = x.shape
    n = batch * seq
    l = LAYER
    m_inner = m_conv_w.shape[-1]
    a_width = a_head_norm.shape[-1]
    x_width = w_mem_kv.shape[-1] // 2
    bf = lambda a: a.astype(BF16)

    gate_lo = 2 * m_inner
    gate_hi = gate_lo + 2 * M_HEADS
    w_main = bf(jnp.concatenate([w_in[l][:, :gate_lo], w_in[l][:, gate_hi:]], axis=1))
    w_gate = bf(jnp.pad(w_in[l][:, gate_lo:gate_hi], ((0, 0), (0, 128 - 2 * M_HEADS))))

    x0 = x.reshape(n, d)
    x1 = _ffn(x0, ffn1_norm[l], bf(ffn1_w_gu[l]), bf(ffn1_w_down[l]), final_norm, final_norm=False)

    proj, gates = _inproj(x1, mix_norm[l], w_main, w_gate)

    q_m, k_m, v_m = _mlstm_qkv(proj, m_conv_w[l], m_conv_b[l], bf(m_wq[l]), bf(m_wk[l]), bf(m_wv[l]),
                               batch=batch, seq=seq)
    gates_t = gates[:, :2 * M_HEADS].reshape(batch, seq, 2, M_HEADS).transpose(0, 3, 2, 1)
    m_hd = m_inner // M_HEADS
    y_m = _mlstm_scan(q_m, k_m, v_m, gates_t, b_if[l], proj, m_inner // m_hd, m_head_norm[l],
                      batch=batch, seq=seq)

    a_hw = a_width // A_HEADS
    qa_col = 2 * m_inner // a_hw
    lam_vecs = jnp.stack([a_lam_q1[l], a_lam_k1[l], a_lam_q2[l], a_lam_k2[l]])
    y_a = _diff_attn(proj, qa_col, qa_col + A_HEADS, qa_col + 2 * A_HEADS, rel_bias, lam_vecs,
                     a_head_norm[l], batch=batch, seq=seq)

    kx, vx = _mem_kv(mem, mem_norm[l], bf(w_mem_kv[l]))
    qx_col = (2 * m_inner + 3 * a_width) // d
    x2 = _merge(x1, proj, qx_col, qx_col + x_width // d, y_m, y_a, kx, vx, b_gate[l],
                bf(w_branch_m[l]), bf(w_branch_a[l]), bf(w_branch_x[l]), bf(w_out[l]),
                batch=batch, seq=seq)

    out = _ffn(x2, ffn2_norm[l], bf(ffn2_w_gu[l]), bf(ffn2_w_down[l]), final_norm, final_norm=True)
    return out.reshape(batch, seq, d)
```

```python
import functools
import math

import jax
import jax.numpy as jnp
from jax import lax
from jax.experimental import pallas as pl
from jax.experimental.pallas import tpu as pltpu

EPS = 1e-6
BF16 = jnp.bfloat16
F32 = jnp.float32
NEG_BIG = -1e30

M_HEADS = 4
M_CONV = 4
A_HEADS = 8
X_HEADS = 4
N_BRANCH = 3
REL_BUCKETS = 32
REL_MAX_DIST = 128
LAYER = 0

VMEM_LIMIT_BYTES = 56 * 1024 * 1024

FFN_ROWS = 512
INPROJ_ROWS = 1024
INPROJ_COLS = 1024
QKV_ROWS = 512
MLSTM_CHUNK = 256
ATTN_TILE = 256
MERGE_ROWS = 512
CONV_HALO = 16


def _params(*sem):
    return pltpu.CompilerParams(dimension_semantics=sem, vmem_limit_bytes=VMEM_LIMIT_BYTES)


def _resident(shape):
    zeros = (0,) * len(shape)
    return pl.BlockSpec(shape, lambda *_: zeros, pipeline_mode=pl.Buffered(1))


def _rms(x):
    return x * lax.rsqrt(jnp.mean(x * x, axis=-1, keepdims=True) + EPS)


def _sigmoid(x):
    return 1.0 / (1.0 + jnp.exp(-x))


def _ffn_kernel(x_ref, g_ref, wgu_ref, wd_ref, fg_ref, o_ref, *, d_ff, final_norm):
    x = x_ref[...]
    hb = (_rms(x) * g_ref[...]).astype(BF16)
    gate = jnp.dot(hb, wgu_ref[:, :d_ff], preferred_element_type=F32)
    up = jnp.dot(hb, wgu_ref[:, d_ff:], preferred_element_type=F32)
    act = (gate * _sigmoid(gate) * up).astype(BF16)
    y = x + 0.5 * jnp.dot(act, wd_ref[...], preferred_element_type=F32)
    if final_norm:
        y = _rms(y) * fg_ref[...]
    o_ref[...] = y


def _ffn(x, gain, w_gu, w_down, final_gain, *, final_norm):
    n, d = x.shape
    d_ff = w_down.shape[0]
    tm = min(FFN_ROWS, n)
    return pl.pallas_call(
        functools.partial(_ffn_kernel, d_ff=d_ff, final_norm=final_norm),
        grid=(n // tm,),
        in_specs=[
            pl.BlockSpec((tm, d), lambda i: (i, 0)),
            _resident((1, d)),
            _resident((d, 2 * d_ff)),
            _resident((d_ff, d)),
            _resident((1, d)),
        ],
        out_specs=pl.BlockSpec((tm, d), lambda i: (i, 0)),
        out_shape=jax.ShapeDtypeStruct((n, d), F32),
        compiler_params=_params("parallel"),
        name="ffn_final" if final_norm else "ffn",
    )(x, gain.reshape(1, d), w_gu, w_down, final_gain.reshape(1, d))


def _inproj_kernel(x_ref, g_ref, w_ref, wgate_ref, p_ref, gates_ref, h_scr):
    @pl.when(pl.program_id(1) == 0)
    def _():
        hb = (_rms(x_ref[...]) * g_ref[...]).astype(BF16)
        h_scr[...] = hb
        gates_ref[...] = jnp.dot(hb, wgate_ref[...], preferred_element_type=F32)

    p_ref[...] = jnp.dot(h_scr[...], w_ref[...], preferred_element_type=F32).astype(BF16)


def _inproj(x, gain, w_main, w_gate):
    n, d = x.shape
    width = w_main.shape[1]
    tm = min(INPROJ_ROWS, n)
    tn = INPROJ_COLS
    return pl.pallas_call(
        _inproj_kernel,
        grid=(n // tm, width // tn),
        in_specs=[
            pl.BlockSpec((tm, d), lambda i, j: (i, 0)),
            _resident((1, d)),
            pl.BlockSpec((d, tn), lambda i, j: (0, j)),
            _resident(w_gate.shape),
        ],
        out_specs=[
            pl.BlockSpec((tm, tn), lambda i, j: (i, j)),
            pl.BlockSpec((tm, w_gate.shape[1]), lambda i, j: (i, 0)),
        ],
        out_shape=[
            jax.ShapeDtypeStruct((n, width), BF16),
            jax.ShapeDtypeStruct((n, w_gate.shape[1]), F32),
        ],
        scratch_shapes=[pltpu.VMEM((tm, d), BF16)],
        compiler_params=_params("parallel", "arbitrary"),
        name="inproj",
    )(x, gain.reshape(1, d), w_main, w_gate)


def _mlstm_qkv_kernel(x_ref, halo_ref, cw_ref, cb_ref, wq_ref, wk_ref, wv_ref,
                      q_ref, k_ref, v_ref, *, k_scale):
    ts = x_ref.shape[0]
    xb = x_ref[...]
    cur = xb.astype(F32)
    prev = jnp.where(pl.program_id(2) > 0, halo_ref[...].astype(F32), 0.0)
    full = jnp.concatenate([prev, cur], axis=0)
    cw = cw_ref[...]
    conv = cb_ref[...] + cw[M_CONV - 1:M_CONV, :] * cur
    for back in range(1, M_CONV):
        tap = M_CONV - 1 - back
        conv = conv + cw[tap:tap + 1, :] * full[CONV_HALO - back:CONV_HALO - back + ts, :]
    xc = (conv * _sigmoid(conv)).astype(BF16)
    q_ref[...] = jnp.dot(xc, wq_ref[0], preferred_element_type=F32).astype(BF16)
    k_ref[...] = (jnp.dot(xc, wk_ref[0], preferred_element_type=F32) * k_scale).astype(BF16)
    v_ref[...] = jnp.dot(xb, wv_ref[0], preferred_element_type=F32).astype(BF16)


def _mlstm_qkv(proj, conv_w, conv_b, wq, wk, wv, *, batch, seq):
    heads, hd, _ = wq.shape
    n = batch * seq
    ts = min(QKV_ROWS, seq)
    nt = seq // ts
    halo_per_tile = ts // CONV_HALO

    def x_map(h, b, t):
        return (b * nt + t, h)

    def halo_map(h, b, t):
        return (jnp.maximum((b * nt + t) * halo_per_tile - 1, 0), h)

    def w_map(h, b, t):
        return (h, 0, 0)

    out = jax.ShapeDtypeStruct((n, heads * hd), BF16)
    return pl.pallas_call(
        functools.partial(_mlstm_qkv_kernel, k_scale=hd ** -0.5),
        grid=(heads, batch, nt),
        in_specs=[
            pl.BlockSpec((ts, hd), x_map),
            pl.BlockSpec((CONV_HALO, hd), halo_map),
            pl.BlockSpec((M_CONV, hd), lambda h, b, t: (0, h)),
            pl.BlockSpec((1, hd), lambda h, b, t: (0, h)),
            pl.BlockSpec((1, hd, hd), w_map),
            pl.BlockSpec((1, hd, hd), w_map),
            pl.BlockSpec((1, hd, hd), w_map),
        ],
        out_specs=[pl.BlockSpec((ts, hd), x_map)] * 3,
        out_shape=[out, out, out],
        compiler_params=_params("parallel", "parallel", "parallel"),
        name="mlstm_qkv",
    )(proj, proj, conv_w, conv_b.reshape(1, -1), wq, wk, wv)


def _mlstm_scan_kernel(q_ref, k_ref, v_ref, g_ref, bias_ref, o_ref, gain_ref, y_ref,
                       c_scr, n_scr, m_scr):
    @pl.when(pl.program_id(2) == 0)
    def _():
        c_scr[...] = jnp.zeros_like(c_scr)
        n_scr[...] = jnp.zeros_like(n_scr)
        m_scr[...] = jnp.zeros_like(m_scr)

    chunk = q_ref.shape[0]
    q = q_ref[...]
    k = k_ref[...]
    v = v_ref[...]
    gates = g_ref[0, 0] + bias_ref[0]
    ig = gates[0:1, :]
    fpre = gates[1:2, :]
    lf = jnp.minimum(fpre, 0.0) - jnp.log1p(jnp.exp(-jnp.abs(fpre)))

    t_idx = lax.broadcasted_iota(jnp.int32, (chunk, chunk), 0)
    s_idx = lax.broadcasted_iota(jnp.int32, (chunk, chunk), 1)
    causal = s_idx <= t_idx
    diag = s_idx == t_idx

    def to_col(row):
        return jnp.sum(jnp.where(diag, row, 0.0), axis=1, keepdims=True)

    def to_row(col):
        return jnp.sum(jnp.where(diag, col, 0.0), axis=0, keepdims=True)

    b_col = jnp.sum(jnp.where(causal, lf, 0.0), axis=1, keepdims=True)
    b_row = to_row(b_col)
    m_prev = m_scr[...]
    dlog = jnp.where(causal, b_col - b_row + ig, NEG_BIG)
    inter = b_col + m_prev
    m_t = jnp.maximum(inter, jnp.max(dlog, axis=1, keepdims=True))
    w_intra = jnp.exp(dlog - m_t)
    w_inter = jnp.exp(inter - m_t)

    s = lax.dot_general(q, k, (((1,), (1,)), ((), ())), preferred_element_type=F32) * w_intra
    c_prev = c_scr[...]
    n_prev = n_scr[...]
    num = (jnp.dot(s.astype(BF16), v, preferred_element_type=F32)
           + w_inter * jnp.dot(q, c_prev.astype(BF16), preferred_element_type=F32))
    qn = jnp.sum(q.astype(F32) * n_prev, axis=1, keepdims=True)
    den = jnp.sum(s, axis=1, keepdims=True) + w_inter * qn
    h = num / jnp.maximum(jnp.abs(den), jnp.exp(-m_t))

    b_last = b_col[chunk - 1:chunk, :]
    tail = b_last - b_row + ig
    m_new = jnp.maximum(b_last + m_prev, jnp.max(tail, axis=1, keepdims=True))
    decay = jnp.exp(b_last + m_prev - m_new)
    w_s = jnp.exp(to_col(tail) - m_new)
    kw = k.astype(F32) * w_s
    c_scr[...] = decay * c_prev + jnp.dot(kw.T.astype(BF16), v, preferred_element_type=F32)
    n_scr[...] = decay * n_prev + jnp.sum(kw, axis=0, keepdims=True)
    m_scr[...] = m_new

    gate = _sigmoid(o_ref[...].astype(F32))
    y_ref[...] = (gate * (_rms(h) * gain_ref[...])).astype(BF16)


def _mlstm_scan(q, k, v, gates_t, b_if, proj, o_col_block, head_gain, *, batch, seq):
    n, width = q.shape
    heads = gates_t.shape[1]
    hd = width // heads
    chunk = min(MLSTM_CHUNK, seq)
    nc = seq // chunk

    def qkv_map(b, h, c):
        return (b * nc + c, h)

    return pl.pallas_call(
        _mlstm_scan_kernel,
        grid=(batch, heads, nc),
        in_specs=[
            pl.BlockSpec((chunk, hd), qkv_map),
            pl.BlockSpec((chunk, hd), qkv_map),
            pl.BlockSpec((chunk, hd), qkv_map),
            pl.BlockSpec((1, 1, 2, chunk), lambda b, h, c: (b, h, 0, c)),
            pl.BlockSpec((1, 2, 1), lambda b, h, c: (h, 0, 0)),
            pl.BlockSpec((chunk, hd), lambda b, h, c: (b * nc + c, o_col_block + h)),
            pl.BlockSpec((1, hd), lambda b, h, c: (0, h)),
        ],
        out_specs=pl.BlockSpec((chunk, hd), qkv_map),
        out_shape=jax.ShapeDtypeStruct((n, width), BF16),
        scratch_shapes=[pltpu.VMEM((hd, hd), F32), pltpu.VMEM((1, hd), F32), pltpu.VMEM((1, 1), F32)],
        compiler_params=_params("parallel", "parallel", "arbitrary"),
        name="mlstm_scan",
    )(q, k, v, gates_t, b_if.reshape(2, heads, 1).transpose(1, 0, 2), proj, head_gain.reshape(1, -1))


def _diff_attn_kernel(far_ref, q_ref, k_ref, v_ref, band_ref, lam_ref, gain_ref, y_ref,
                      vt_scr, acc_scr, *, out_scale, lam_init):
    head = pl.program_id(1)
    qi = pl.program_id(2)
    tile, width = q_ref.shape
    half = width // 2
    n_tiles = vt_scr.shape[0]

    @pl.when(qi == 0)
    def _():
        for j in range(n_tiles):
            vt_scr[j] = v_ref[j * tile:(j + 1) * tile, :].astype(F32).T.astype(BF16)

    q = q_ref[...]
    lane = lax.broadcasted_iota(jnp.int32, q.shape, 1)
    zero = jnp.zeros_like(q)
    qs = jnp.concatenate([jnp.where(lane < half, q, zero), jnp.where(lane >= half, q, zero)], axis=0)
    qs = qs * (half ** -0.5)

    acc_scr[...] = jnp.zeros_like(acc_scr)

    def block(kj, m_prev, l_prev, shift, band):
        start = pl.multiple_of(kj * tile, tile)
        st = lax.dot_general(k_ref[pl.ds(start, tile), :], qs, (((1,), (1,)), ((), ())),
                             preferred_element_type=F32)
        if band is not None:
            st = st + jnp.concatenate([band, band], axis=1)
        m_new = jnp.maximum(m_prev, jnp.max(st, axis=0, keepdims=True) + shift)
        alpha = jnp.exp(m_prev - m_new)
        p = jnp.exp(st - (m_new - shift))
        l_new = alpha * l_prev + jnp.sum(p, axis=0, keepdims=True)
        acc_scr[...] = alpha * acc_scr[...] + jnp.dot(vt_scr[kj], p.astype(BF16),
                                                      preferred_element_type=F32)
        return m_new, l_new

    m0 = jnp.full((1, 2 * tile), NEG_BIG, F32)
    l0 = jnp.zeros((1, 2 * tile), F32)

    far_bias = far_ref[head]
    m1, l1 = lax.fori_loop(0, jnp.maximum(qi - 1, 0),
                           lambda kj, c: block(kj, c[0], c[1], far_bias, None), (m0, l0))

    m2, l2 = lax.cond(qi > 0,
                      lambda: block(qi - 1, m1, l1, 0.0, band_ref[0, 0]),
                      lambda: (m1, l1))
    _, l3 = block(qi, m2, l2, 0.0, band_ref[0, 1])

    lam_v = lam_ref[...]
    lam = (jnp.exp(jnp.sum(lam_v[0:1, :] * lam_v[1:2, :], axis=1, keepdims=True))
           - jnp.exp(jnp.sum(lam_v[2:3, :] * lam_v[3:4, :], axis=1, keepdims=True)) + lam_init)
    o = acc_scr[...] / l3
    out = (o[:, :tile] - lam * o[:, tile:]).T
    y_ref[...] = (_rms(out) * gain_ref[...] * out_scale).astype(BF16)


def _t5_bucket(n):
    max_exact = REL_BUCKETS // 2
    nf = jnp.maximum(n, 1).astype(F32)
    large = max_exact + (jnp.log(nf / max_exact) / math.log(REL_MAX_DIST / max_exact)
                         * (REL_BUCKETS - max_exact)).astype(jnp.int32)
    large = jnp.minimum(large, REL_BUCKETS - 1)
    return jnp.where(n < max_exact, n, large)


def _diff_attn(proj, q_col, k_col, v_col, rel_bias, lam_vecs, head_gain, *, batch, seq):
    n = proj.shape[0]
    heads = rel_bias.shape[1]
    hw = head_gain.shape[0] // heads
    tile = min(ATTN_TILE, seq)
    nq = seq // tile
    assert tile >= REL_MAX_DIST, "distances beyond one tile must all fall in the last bucket"

    kk = jnp.arange(tile, dtype=jnp.int32)[:, None]
    qq = jnp.arange(tile, dtype=jnp.int32)[None, :]
    dist = jnp.stack([tile + qq - kk, qq - kk])
    bucket = _t5_bucket(jnp.maximum(dist, 0))
    band = jnp.zeros((heads,) + dist.shape, F32)
    for b in range(REL_BUCKETS):
        band = jnp.where(bucket[None] == b, rel_bias[b].astype(F32)[:, None, None, None], band)
    band = jnp.where(dist[None] >= 0, band, NEG_BIG)
    far = rel_bias[REL_BUCKETS - 1].astype(F32)

    lam_init = 0.8 - 0.6 * math.exp(-0.3 * LAYER)
    return pl.pallas_call(
        functools.partial(_diff_attn_kernel, out_scale=1.0 - lam_init, lam_init=lam_init),
        grid=(batch, heads, nq),
        in_specs=[
            pl.BlockSpec(memory_space=pltpu.SMEM),
            pl.BlockSpec((tile, hw), lambda b, h, i: (b * nq + i, q_col + h)),
            pl.BlockSpec((seq, hw), lambda b, h, i: (b, k_col + h)),
            pl.BlockSpec((seq, hw), lambda b, h, i: (b, v_col + h)),
            pl.BlockSpec((1, 2, tile, tile), lambda b, h, i: (h, 0, 0, 0)),
            _resident(lam_vecs.shape),
            pl.BlockSpec((1, hw), lambda b, h, i: (0, h)),
        ],
        out_specs=pl.BlockSpec((tile, hw), lambda b, h, i: (b * nq + i, h)),
        out_shape=jax.ShapeDtypeStruct((n, heads * hw), BF16),
        scratch_shapes=[
            pltpu.VMEM((nq, hw, tile), BF16),
            pltpu.VMEM((hw, 2 * tile), F32),
        ],
        compiler_params=_params("parallel", "parallel", "arbitrary"),
        name="diff_attn",
    )(far, proj, proj, proj, band, lam_vecs, head_gain.reshape(1, -1))


def _mem_kv_kernel(mem_ref, g_ref, w_ref, k_ref, v_ref):
    width = k_ref.shape[-1]
    hb = (_rms(mem_ref[0]) * g_ref[...]).astype(BF16)
    k_ref[0] = jnp.dot(hb, w_ref[:, :width], preferred_element_type=F32).astype(BF16)
    v_ref[0] = jnp.dot(hb, w_ref[:, width:], preferred_element_type=F32).astype(BF16)


def _mem_kv(mem, gain, w_kv):
    batch, mlen, d = mem.shape
    width = w_kv.shape[1] // 2
    out = jax.ShapeDtypeStruct((batch, mlen, width), BF16)
    return pl.pallas_call(
        _mem_kv_kernel,
        grid=(batch,),
        in_specs=[
            pl.BlockSpec((1, mlen, d), lambda b: (b, 0, 0)),
            _resident((1, d)),
            _resident(w_kv.shape),
        ],
        out_specs=[pl.BlockSpec((1, mlen, width), lambda b: (b, 0, 0))] * 2,
        out_shape=[out, out],
        compiler_params=_params("parallel"),
        name="mem_kv",
    )(mem, gain.reshape(1, d), w_kv)


def _merge_kernel(x_ref, qx_ref, g0_ref, g1_ref, g2_ref, ym_ref, ya_ref, kx_ref, vx_ref,
                  bg_ref, wm_ref, wa_ref, wx_ref, wo_ref, o_ref):
    d = x_ref.shape[1]
    xw = qx_ref.shape[1]
    xhd = xw // X_HEADS
    heads = []
    for h in range(X_HEADS):
        cols = slice(h * xhd, (h + 1) * xhd)
        s = lax.dot_general(qx_ref[:, cols], kx_ref[0, :, cols], (((1,), (1,)), ((), ())),
                            preferred_element_type=F32) * (xhd ** -0.5)
        p = jnp.exp(s - jnp.max(s, axis=1, keepdims=True))
        pv = jnp.dot(p.astype(BF16), vx_ref[0, :, cols], preferred_element_type=F32)
        heads.append((pv / jnp.sum(p, axis=1, keepdims=True)).astype(BF16))
    yx = jnp.concatenate(heads, axis=1)

    def gate(g_ref, k):
        return _sigmoid(g_ref[...].astype(F32) + bg_ref[:, k * d:(k + 1) * d])

    merged = (gate(g0_ref, 0) * jnp.dot(ym_ref[...], wm_ref[...], preferred_element_type=F32)
              + gate(g1_ref, 1) * jnp.dot(ya_ref[...], wa_ref[...], preferred_element_type=F32)
              + gate(g2_ref, 2) * jnp.dot(yx, wx_ref[...], preferred_element_type=F32))
    o_ref[...] = x_ref[...] + jnp.dot(merged.astype(BF16), wo_ref[...], preferred_element_type=F32)


def _merge(x, proj, qx_col, g_col, y_m, y_a, kx, vx, b_gate, w_m, w_a, w_x, w_o, *, batch, seq):
    n, d = x.shape
    tm = min(MERGE_ROWS, seq)
    nt = seq // tm

    def row(b, t):
        return b * nt + t

    def proj_spec(col):
        return pl.BlockSpec((tm, d), lambda b, t: (row(b, t), col))

    def mem_spec(a):
        return pl.BlockSpec((1,) + a.shape[1:], lambda b, t: (b, 0, 0))

    return pl.pallas_call(
        _merge_kernel,
        grid=(batch, nt),
        in_specs=[
            pl.BlockSpec((tm, d), lambda b, t: (row(b, t), 0)),
            proj_spec(qx_col), proj_spec(g_col), proj_spec(g_col + 1), proj_spec(g_col + 2),
            pl.BlockSpec((tm, y_m.shape[1]), lambda b, t: (row(b, t), 0)),
            pl.BlockSpec((tm, y_a.shape[1]), lambda b, t: (row(b, t), 0)),
            mem_spec(kx), mem_spec(vx),
            _resident((1, b_gate.shape[0])),
            _resident(w_m.shape), _resident(w_a.shape), _resident(w_x.shape), _resident(w_o.shape),
        ],
        out_specs=pl.BlockSpec((tm, d), lambda b, t: (row(b, t), 0)),
        out_shape=jax.ShapeDtypeStruct((n, d), F32),
        compiler_params=_params("parallel", "parallel"),
        name="merge",
    )(x, proj, proj, proj, proj, y_m, y_a, kx, vx, b_gate.reshape(1, -1), w_m, w_a, w_x, w_o)


def kernel(x, mem, positions, rel_bias, ffn1_norm, ffn1_w_gu, ffn1_w_down, mix_norm, mem_norm, w_in, b_if, b_gate, m_conv_w, m_conv_b, m_wq, m_wk, m_wv, m_head_norm, a_lam_q1, a_lam_k1, a_lam_q2, a_lam_k2, a_head_norm, w_mem_kv, w_branch_m, w_branch_a, w_branch_x, w_out, ffn2_norm, ffn2_w_gu, ffn2_w_down, final_norm):
    del positions
    batch, seq, d = x.shape
    n = batch * seq
    l = LAYER
    m_inner = m_conv_w.shape[-1]
    a_width = a_head_norm.shape[-1]
    x_width = w_mem_kv.shape[-1] // 2
    bf = lambda a: a.astype(BF16)

    gate_lo = 2 * m_inner
    gate_hi = gate_lo + 2 * M_HEADS
    w_main = bf(jnp.concatenate([w_in[l][:, :gate_lo], w_in[l][:, gate_hi:]], axis=1))
    w_gate = bf(jnp.pad(w_in[l][:, gate_lo:gate_hi], ((0, 0), (0, 128 - 2 * M_HEADS))))

    x0 = x.reshape(n, d)
    x1 = _ffn(x0, ffn1_norm[l], bf(ffn1_w_gu[l]), bf(ffn1_w_down[l]), final_norm, final_norm=False)

    proj, gates = _inproj(x1, mix_norm[l], w_main, w_gate)

    q_m, k_m, v_m = _mlstm_qkv(proj, m_conv_w[l], m_conv_b[l], bf(m_wq[l]), bf(m_wk[l]), bf(m_wv[l]),
                               batch=batch, seq=seq)
    gates_t = gates[:, :2 * M_HEADS].reshape(batch, seq, 2, M_HEADS).transpose(0, 3, 2, 1)
    m_hd = m_inner // M_HEADS
    y_m = _mlstm_scan(q_m, k_m, v_m, gates_t, b_if[l], proj, m_inner // m_hd, m_head_norm[l],
                      batch=batch, seq=seq)

    a_hw = a_width // A_HEADS
    qa_col = 2 * m_inner // a_hw
    lam_vecs = jnp.stack([a_lam_q1[l], a_lam_k1[l], a_lam_q2[l], a_lam_k2[l]])
    y_a = _diff_attn(proj, qa_col, qa_col + A_HEADS, qa_col + 2 * A_HEADS, rel_bias, lam_vecs,
                     a_head_norm[l], batch=batch, seq=seq)

    kx, vx = _mem_kv(mem, mem_norm[l], bf(w_mem_kv[l]))
    qx_col = (2 * m_inner + 3 * a_width) // d
    x2 = _merge(x1, proj, qx_col, qx_col + x_width // d, y_m, y_a, kx, vx, b_gate[l],
                bf(w_branch_m[l]), bf(w_branch_a[l]), bf(w_branch_x[l]), bf(w_out[l]),
                batch=batch, seq=seq)

    out = _ffn(x2, ffn2_norm[l], bf(ffn2_w_gu[l]), bf(ffn2_w_down[l]), final_norm, final_norm=True)
    return out.reshape(batch, seq, d)
```

```python
import functools
import math

import jax
import jax.numpy as jnp
from jax import lax
from jax.experimental import pallas as pl
from jax.experimental.pallas import tpu as pltpu

EPS = 1e-6
BF16 = jnp.bfloat16
F32 = jnp.float32
NEG_BIG = -1e30

M_HEADS = 4
M_CONV = 4
A_HEADS = 8
X_HEADS = 4
N_BRANCH = 3
REL_BUCKETS = 32
REL_MAX_DIST = 128
LAYER = 0

VMEM_LIMIT_BYTES = 56 * 1024 * 1024

FFN_ROWS = 512
INPROJ_ROWS = 1024
INPROJ_COLS = 1024
QKV_ROWS = 512
MLSTM_CHUNK = 256
ATTN_TILE = 256
ATTN_HEAD_GROUP = 4
ATTN_STRIP = 128
ATTN_PV_COLS = 256
MERGE_ROWS = 512
CONV_HALO = 16


def _params(*sem, flags=None):
    return pltpu.CompilerParams(dimension_semantics=sem, vmem_limit_bytes=VMEM_LIMIT_BYTES, flags=flags)


def _resident(shape):
    zeros = (0,) * len(shape)
    return pl.BlockSpec(shape, lambda *_: zeros, pipeline_mode=pl.Buffered(1))


def _rms(x):
    return x * lax.rsqrt(jnp.mean(x * x, axis=-1, keepdims=True) + EPS)


def _sigmoid(x):
    return 1.0 / (1.0 + jnp.exp(-x))


def _ffn_kernel(x_ref, g_ref, wgu_ref, wd_ref, fg_ref, o_ref, *, d_ff, final_norm):
    x = x_ref[...]
    hb = (_rms(x) * g_ref[...]).astype(BF16)
    gate = jnp.dot(hb, wgu_ref[:, :d_ff], preferred_element_type=F32)
    up = jnp.dot(hb, wgu_ref[:, d_ff:], preferred_element_type=F32)
    act = (gate * _sigmoid(gate) * up).astype(BF16)
    y = x + 0.5 * jnp.dot(act, wd_ref[...], preferred_element_type=F32)
    if final_norm:
        y = _rms(y) * fg_ref[...]
    o_ref[...] = y


def _ffn(x, gain, w_gu, w_down, final_gain, *, final_norm):
    n, d = x.shape
    d_ff = w_down.shape[0]
    tm = min(FFN_ROWS, n)
    return pl.pallas_call(
        functools.partial(_ffn_kernel, d_ff=d_ff, final_norm=final_norm),
        grid=(n // tm,),
        in_specs=[
            pl.BlockSpec((tm, d), lambda i: (i, 0)),
            _resident((1, d)),
            _resident((d, 2 * d_ff)),
            _resident((d_ff, d)),
            _resident((1, d)),
        ],
        out_specs=pl.BlockSpec((tm, d), lambda i: (i, 0)),
        out_shape=jax.ShapeDtypeStruct((n, d), F32),
        compiler_params=_params("parallel"),
        name="ffn_final" if final_norm else "ffn",
    )(x, gain.reshape(1, d), w_gu, w_down, final_gain.reshape(1, d))


def _inproj_kernel(x_ref, g_ref, w_ref, wgate_ref, p_ref, gates_ref, h_scr):
    @pl.when(pl.program_id(1) == 0)
    def _():
        hb = (_rms(x_ref[...]) * g_ref[...]).astype(BF16)
        h_scr[...] = hb
        gates_ref[...] = jnp.dot(hb, wgate_ref[...], preferred_element_type=F32)

    p_ref[...] = jnp.dot(h_scr[...], w_ref[...], preferred_element_type=F32).astype(BF16)


def _inproj(x, gain, w_main, w_gate):
    n, d = x.shape
    width = w_main.shape[1]
    tm = min(INPROJ_ROWS, n)
    tn = INPROJ_COLS
    return pl.pallas_call(
        _inproj_kernel,
        grid=(n // tm, width // tn),
        in_specs=[
            pl.BlockSpec((tm, d), lambda i, j: (i, 0)),
            _resident((1, d)),
            pl.BlockSpec((d, tn), lambda i, j: (0, j)),
            _resident(w_gate.shape),
        ],
        out_specs=[
            pl.BlockSpec((tm, tn), lambda i, j: (i, j)),
            pl.BlockSpec((tm, w_gate.shape[1]), lambda i, j: (i, 0)),
        ],
        out_shape=[
            jax.ShapeDtypeStruct((n, width), BF16),
            jax.ShapeDtypeStruct((n, w_gate.shape[1]), F32),
        ],
        scratch_shapes=[pltpu.VMEM((tm, d), BF16)],
        compiler_params=_params("parallel", "arbitrary"),
        name="inproj",
    )(x, gain.reshape(1, d), w_main, w_gate)


def _mlstm_qkv_kernel(x_ref, halo_ref, cw_ref, cb_ref, wq_ref, wk_ref, wv_ref,
                      q_ref, k_ref, v_ref, *, k_scale):
    ts = x_ref.shape[0]
    xb = x_ref[...]
    cur = xb.astype(F32)
    prev = jnp.where(pl.program_id(2) > 0, halo_ref[...].astype(F32), 0.0)
    full = jnp.concatenate([prev, cur], axis=0)
    cw = cw_ref[...]
    conv = cb_ref[...] + cw[M_CONV - 1:M_CONV, :] * cur
    for back in range(1, M_CONV):
        tap = M_CONV - 1 - back
        conv = conv + cw[tap:tap + 1, :] * full[CONV_HALO - back:CONV_HALO - back + ts, :]
    xc = (conv * _sigmoid(conv)).astype(BF16)
    q_ref[...] = jnp.dot(xc, wq_ref[0], preferred_element_type=F32).astype(BF16)
    k_ref[...] = (jnp.dot(xc, wk_ref[0], preferred_element_type=F32) * k_scale).astype(BF16)
    v_ref[...] = jnp.dot(xb, wv_ref[0], preferred_element_type=F32).astype(BF16)


def _mlstm_qkv(proj, conv_w, conv_b, wq, wk, wv, *, batch, seq):
    heads, hd, _ = wq.shape
    n = batch * seq
    ts = min(QKV_ROWS, seq)
    nt = seq // ts
    halo_per_tile = ts // CONV_HALO

    def x_map(h, b, t):
        return (b * nt + t, h)

    def halo_map(h, b, t):
        return (jnp.maximum((b * nt + t) * halo_per_tile - 1, 0), h)

    def w_map(h, b, t):
        return (h, 0, 0)

    out = jax.ShapeDtypeStruct((n, heads * hd), BF16)
    return pl.pallas_call(
        functools.partial(_mlstm_qkv_kernel, k_scale=hd ** -0.5),
        grid=(heads, batch, nt),
        in_specs=[
            pl.BlockSpec((ts, hd), x_map),
            pl.BlockSpec((CONV_HALO, hd), halo_map),
            pl.BlockSpec((M_CONV, hd), lambda h, b, t: (0, h)),
            pl.BlockSpec((1, hd), lambda h, b, t: (0, h)),
            pl.BlockSpec((1, hd, hd), w_map),
            pl.BlockSpec((1, hd, hd), w_map),
            pl.BlockSpec((1, hd, hd), w_map),
        ],
        out_specs=[pl.BlockSpec((ts, hd), x_map)] * 3,
        out_shape=[out, out, out],
        compiler_params=_params("parallel", "parallel", "parallel"),
        name="mlstm_qkv",
    )(proj, proj, conv_w, conv_b.reshape(1, -1), wq, wk, wv)


def _mlstm_scan_kernel(q_ref, k_ref, v_ref, g_ref, bias_ref, o_ref, gain_ref, y_ref,
                       c_scr, n_scr, m_scr):
    @pl.when(pl.program_id(2) == 0)
    def _():
        c_scr[...] = jnp.zeros_like(c_scr)
        n_scr[...] = jnp.zeros_like(n_scr)
        m_scr[...] = jnp.zeros_like(m_scr)

    chunk = q_ref.shape[0]
    q = q_ref[...]
    k = k_ref[...]
    v = v_ref[...]
    gates = g_ref[0, 0] + bias_ref[0]
    ig = gates[0:1, :]
    fpre = gates[1:2, :]
    lf = jnp.minimum(fpre, 0.0) - jnp.log1p(jnp.exp(-jnp.abs(fpre)))

    t_idx = lax.broadcasted_iota(jnp.int32, (chunk, chunk), 0)
    s_idx = lax.broadcasted_iota(jnp.int32, (chunk, chunk), 1)
    causal = s_idx <= t_idx
    diag = s_idx == t_idx

    def to_col(row):
        return jnp.sum(jnp.where(diag, row, 0.0), axis=1, keepdims=True)

    def to_row(col):
        return jnp.sum(jnp.where(diag, col, 0.0), axis=0, keepdims=True)

    b_col = jnp.sum(jnp.where(causal, lf, 0.0), axis=1, keepdims=True)
    b_row = to_row(b_col)
    m_prev = m_scr[...]
    dlog = jnp.where(causal, b_col - b_row + ig, NEG_BIG)
    inter = b_col + m_prev
    m_t = jnp.maximum(inter, jnp.max(dlog, axis=1, keepdims=True))
    w_intra = jnp.exp(dlog - m_t)
    w_inter = jnp.exp(inter - m_t)

    s = lax.dot_general(q, k, (((1,), (1,)), ((), ())), preferred_element_type=F32) * w_intra
    c_prev = c_scr[...]
    n_prev = n_scr[...]
    num = (jnp.dot(s.astype(BF16), v, preferred_element_type=F32)
           + w_inter * jnp.dot(q, c_prev.astype(BF16), preferred_element_type=F32))
    qn = jnp.sum(q.astype(F32) * n_prev, axis=1, keepdims=True)
    den = jnp.sum(s, axis=1, keepdims=True) + w_inter * qn
    h = num / jnp.maximum(jnp.abs(den), jnp.exp(-m_t))

    b_last = b_col[chunk - 1:chunk, :]
    tail = b_last - b_row + ig
    m_new = jnp.maximum(b_last + m_prev, jnp.max(tail, axis=1, keepdims=True))
    decay = jnp.exp(b_last + m_prev - m_new)
    w_s = jnp.exp(to_col(tail) - m_new)
    kw = k.astype(F32) * w_s
    c_scr[...] = decay * c_prev + jnp.dot(kw.T.astype(BF16), v, preferred_element_type=F32)
    n_scr[...] = decay * n_prev + jnp.sum(kw, axis=0, keepdims=True)
    m_scr[...] = m_new

    gate = _sigmoid(o_ref[...].astype(F32))
    y_ref[...] = (gate * (_rms(h) * gain_ref[...])).astype(BF16)


def _mlstm_scan(q, k, v, gates_t, b_if, proj, o_col_block, head_gain, *, batch, seq):
    n, width = q.shape
    heads = gates_t.shape[1]
    hd = width // heads
    chunk = min(MLSTM_CHUNK, seq)
    nc = seq // chunk

    def qkv_map(b, h, c):
        return (b * nc + c, h)

    return pl.pallas_call(
        _mlstm_scan_kernel,
        grid=(batch, heads, nc),
        in_specs=[
            pl.BlockSpec((chunk, hd), qkv_map),
            pl.BlockSpec((chunk, hd), qkv_map),
            pl.BlockSpec((chunk, hd), qkv_map),
            pl.BlockSpec((1, 1, 2, chunk), lambda b, h, c: (b, h, 0, c)),
            pl.BlockSpec((1, 2, 1), lambda b, h, c: (h, 0, 0)),
            pl.BlockSpec((chunk, hd), lambda b, h, c: (b * nc + c, o_col_block + h)),
            pl.BlockSpec((1, hd), lambda b, h, c: (0, h)),
        ],
        out_specs=pl.BlockSpec((chunk, hd), qkv_map),
        out_shape=jax.ShapeDtypeStruct((n, width), BF16),
        scratch_shapes=[pltpu.VMEM((hd, hd), F32), pltpu.VMEM((1, hd), F32), pltpu.VMEM((1, 1), F32)],
        compiler_params=_params("parallel", "parallel", "arbitrary"),
        name="mlstm_scan",
    )(q, k, v, gates_t, b_if.reshape(2, heads, 1).transpose(1, 0, 2), proj, head_gain.reshape(1, -1))


def _diff_attn_kernel(far_ref, q_ref, k_ref, v_ref, band_ref, lam_ref, gain_ref, y_ref,
                      vt_scr, acc_scr, *, out_scale, lam_init):
    group, n_tiles, width, tile = vt_scr.shape
    head0 = pl.program_id(1) * group
    qi = pl.program_id(2)
    half = width // 2

    @pl.when(qi == 0)
    def _():
        for g in range(group):
            for j in range(n_tiles):
                v_tile = v_ref[j * tile:(j + 1) * tile, g * width:(g + 1) * width]
                vt_scr[g, j] = v_tile.astype(F32).T.astype(BF16)

    q_all = q_ref[...]
    lane = lax.broadcasted_iota(jnp.int32, (tile, width), 1)
    zero = jnp.zeros((tile, width), BF16)
    qs = []
    for g in range(group):
        q = q_all[:, g * width:(g + 1) * width]
        stacked = jnp.concatenate([jnp.where(lane < half, q, zero), jnp.where(lane >= half, q, zero)], axis=0)
        qs.append(stacked * (half ** -0.5))

    acc_scr[...] = jnp.zeros_like(acc_scr)

    def block(kj, stats, shifts, band_idx):
        start = pl.multiple_of(kj * tile, tile)
        k_all = k_ref[pl.ds(start, tile), :]
        out = []
        scores = [lax.dot_general(k_all[:, g * width:(g + 1) * width], qs[g], (((1,), (1,)), ((), ())),
                                  preferred_element_type=F32) for g in range(group)]
        for g in range(group):
            m_prev, l_prev = stats[g]
            st = scores[g]
            if band_idx is not None:
                band = band_ref[g, band_idx]
                st = st + jnp.concatenate([band, band], axis=1)
            vt = vt_scr[g, kj]
            m_cols, l_cols = [], []
            for c0 in range(0, 2 * tile, ATTN_PV_COLS):
                p_cols, a_cols = [], []
                for c in range(c0, c0 + ATTN_PV_COLS, ATTN_STRIP):
                    cols = slice(c, c + ATTN_STRIP)
                    s_c = st[:, cols]
                    m_c = jnp.maximum(m_prev[:, cols], jnp.max(s_c, axis=0, keepdims=True) + shifts[g])
                    a_c = jnp.exp(m_prev[:, cols] - m_c)
                    p_c = jnp.exp(s_c - (m_c - shifts[g]))
                    l_cols.append(a_c * l_prev[:, cols] + jnp.sum(p_c, axis=0, keepdims=True))
                    m_cols.append(m_c)
                    a_cols.append(a_c)
                    p_cols.append(p_c.astype(BF16))
                pv = jnp.dot(vt, jnp.concatenate(p_cols, axis=1), preferred_element_type=F32)
                wide = slice(c0, c0 + ATTN_PV_COLS)
                acc_scr[g, :, wide] = jnp.concatenate(a_cols, axis=1) * acc_scr[g, :, wide] + pv
            out.append((jnp.concatenate(m_cols, axis=1), jnp.concatenate(l_cols, axis=1)))
        return tuple(out)

    init = tuple((jnp.full((1, 2 * tile), NEG_BIG, F32), jnp.zeros((1, 2 * tile), F32))
                 for _ in range(group))
    no_shift = (0.0,) * group

    far_bias = tuple(far_ref[head0 + g] for g in range(group))
    stats = lax.fori_loop(0, jnp.maximum(qi - 1, 0), lambda kj, c: block(kj, c, far_bias, None), init)
    stats = lax.cond(qi > 0, lambda: block(qi - 1, stats, no_shift, 0), lambda: stats)
    stats = block(qi, stats, no_shift, 1)

    lam_v = lam_ref[...]
    lam = (jnp.exp(jnp.sum(lam_v[0:1, :] * lam_v[1:2, :], axis=1, keepdims=True))
           - jnp.exp(jnp.sum(lam_v[2:3, :] * lam_v[3:4, :], axis=1, keepdims=True)) + lam_init)
    for g in range(group):
        o = acc_scr[g] / stats[g][1]
        out = (o[:, :tile] - lam * o[:, tile:]).T
        gain = gain_ref[:, g * width:(g + 1) * width]
        y_ref[:, g * width:(g + 1) * width] = (_rms(out) * gain * out_scale).astype(BF16)


def _t5_bucket(n):
    max_exact = REL_BUCKETS // 2
    nf = jnp.maximum(n, 1).astype(F32)
    large = max_exact + (jnp.log(nf / max_exact) / math.log(REL_MAX_DIST / max_exact)
                         * (REL_BUCKETS - max_exact)).astype(jnp.int32)
    large = jnp.minimum(large, REL_BUCKETS - 1)
    return jnp.where(n < max_exact, n, large)


def _diff_attn(proj, q_col, k_col, v_col, rel_bias, lam_vecs, head_gain, *, batch, seq):
    n = proj.shape[0]
    heads = rel_bias.shape[1]
    hw = head_gain.shape[0] // heads
    tile = min(ATTN_TILE, seq)
    nq = seq // tile
    assert tile >= REL_MAX_DIST, "distances beyond one tile must all fall in the last bucket"
    group = ATTN_HEAD_GROUP
    gw = group * hw
    assert heads % group == 0 and q_col % group == 0 and k_col % group == 0 and v_col % group == 0

    kk = jnp.arange(tile, dtype=jnp.int32)[:, None]
    qq = jnp.arange(tile, dtype=jnp.int32)[None, :]
    dist = jnp.stack([tile + qq - kk, qq - kk])
    bucket = _t5_bucket(jnp.maximum(dist, 0))
    band = jnp.zeros((heads,) + dist.shape, F32)
    for b in range(REL_BUCKETS):
        band = jnp.where(bucket[None] == b, rel_bias[b].astype(F32)[:, None, None, None], band)
    band = jnp.where(dist[None] >= 0, band, NEG_BIG)
    far = rel_bias[REL_BUCKETS - 1].astype(F32)

    lam_init = 0.8 - 0.6 * math.exp(-0.3 * LAYER)
    return pl.pallas_call(
        functools.partial(_diff_attn_kernel, out_scale=1.0 - lam_init, lam_init=lam_init),
        grid=(batch, heads // group, nq),
        in_specs=[
            pl.BlockSpec(memory_space=pltpu.SMEM),
            pl.BlockSpec((tile, gw), lambda b, h, i: (b * nq + i, q_col // group + h)),
            pl.BlockSpec((seq, gw), lambda b, h, i: (b, k_col // group + h)),
            pl.BlockSpec((seq, gw), lambda b, h, i: (b, v_col // group + h)),
            pl.BlockSpec((group, 2, tile, tile), lambda b, h, i: (h, 0, 0, 0)),
            _resident(lam_vecs.shape),
            pl.BlockSpec((1, gw), lambda b, h, i: (0, h)),
        ],
        out_specs=pl.BlockSpec((tile, gw), lambda b, h, i: (b * nq + i, h)),
        out_shape=jax.ShapeDtypeStruct((n, heads * hw), BF16),
        scratch_shapes=[
            pltpu.VMEM((group, nq, hw, tile), BF16),
            pltpu.VMEM((group, hw, 2 * tile), F32),
        ],
        compiler_params=_params("parallel", "parallel", "arbitrary"),
        name="diff_attn",
    )(far, proj, proj, proj, band, lam_vecs, head_gain.reshape(1, -1))


def _mem_kv_kernel(mem_ref, g_ref, w_ref, k_ref, v_ref):
    width = k_ref.shape[-1]
    hb = (_rms(mem_ref[0]) * g_ref[...]).astype(BF16)
    k_ref[0] = jnp.dot(hb, w_ref[:, :width], preferred_element_type=F32).astype(BF16)
    v_ref[0] = jnp.dot(hb, w_ref[:, width:], preferred_element_type=F32).astype(BF16)


def _mem_kv(mem, gain, w_kv):
    batch, mlen, d = mem.shape
    width = w_kv.shape[1] // 2
    out = jax.ShapeDtypeStruct((batch, mlen, width), BF16)
    return pl.pallas_call(
        _mem_kv_kernel,
        grid=(batch,),
        in_specs=[
            pl.BlockSpec((1, mlen, d), lambda b: (b, 0, 0)),
            _resident((1, d)),
            _resident(w_kv.shape),
        ],
        out_specs=[pl.BlockSpec((1, mlen, width), lambda b: (b, 0, 0))] * 2,
        out_shape=[out, out],
        compiler_params=_params("parallel"),
        name="mem_kv",
    )(mem, gain.reshape(1, d), w_kv)


def _merge_kernel(x_ref, qx_ref, g0_ref, g1_ref, g2_ref, ym_ref, ya_ref, kx_ref, vx_ref,
                  bg_ref, wm_ref, wa_ref, wx_ref, wo_ref, o_ref):
    d = x_ref.shape[1]
    xw = qx_ref.shape[1]
    xhd = xw // X_HEADS
    heads = []
    for h in range(X_HEADS):
        cols = slice(h * xhd, (h + 1) * xhd)
        s = lax.dot_general(qx_ref[:, cols], kx_ref[0, :, cols], (((1,), (1,)), ((), ())),
                            preferred_element_type=F32) * (xhd ** -0.5)
        p = jnp.exp(s - jnp.max(s, axis=1, keepdims=True))
        pv = jnp.dot(p.astype(BF16), vx_ref[0, :, cols], preferred_element_type=F32)
        heads.append((pv / jnp.sum(p, axis=1, keepdims=True)).astype(BF16))
    yx = jnp.concatenate(heads, axis=1)

    def gate(g_ref, k):
        return _sigmoid(g_ref[...].astype(F32) + bg_ref[:, k * d:(k + 1) * d])

    merged = (gate(g0_ref, 0) * jnp.dot(ym_ref[...], wm_ref[...], preferred_element_type=F32)
              + gate(g1_ref, 1) * jnp.dot(ya_ref[...], wa_ref[...], preferred_element_type=F32)
              + gate(g2_ref, 2) * jnp.dot(yx, wx_ref[...], preferred_element_type=F32))
    o_ref[...] = x_ref[...] + jnp.dot(merged.astype(BF16), wo_ref[...], preferred_element_type=F32)


def _merge(x, proj, qx_col, g_col, y_m, y_a, kx, vx, b_gate, w_m, w_a, w_x, w_o, *, batch, seq):
    n, d = x.shape
    tm = min(MERGE_ROWS, seq)
    nt = seq // tm

    def row(b, t):
        return b * nt + t

    def proj_spec(col):
        return pl.BlockSpec((tm, d), lambda b, t: (row(b, t), col))

    def mem_spec(a):
        return pl.BlockSpec((1,) + a.shape[1:], lambda b, t: (b, 0, 0))

    return pl.pallas_call(
        _merge_kernel,
        grid=(batch, nt),
        in_specs=[
            pl.BlockSpec((tm, d), lambda b, t: (row(b, t), 0)),
            proj_spec(qx_col), proj_spec(g_col), proj_spec(g_col + 1), proj_spec(g_col + 2),
            pl.BlockSpec((tm, y_m.shape[1]), lambda b, t: (row(b, t), 0)),
            pl.BlockSpec((tm, y_a.shape[1]), lambda b, t: (row(b, t), 0)),
            mem_spec(kx), mem_spec(vx),
            _resident((1, b_gate.shape[0])),
            _resident(w_m.shape), _resident(w_a.shape), _resident(w_x.shape), _resident(w_o.shape),
        ],
        out_specs=pl.BlockSpec((tm, d), lambda b, t: (row(b, t), 0)),
        out_shape=jax.ShapeDtypeStruct((n, d), F32),
        compiler_params=_params("parallel", "parallel"),
        name="merge",
    )(x, proj, proj, proj, proj, y_m, y_a, kx, vx, b_gate.reshape(1, -1), w_m, w_a, w_x, w_o)


def kernel(x, mem, positions, rel_bias, ffn1_norm, ffn1_w_gu, ffn1_w_down, mix_norm, mem_norm, w_in, b_if, b_gate, m_conv_w, m_conv_b, m_wq, m_wk, m_wv, m_head_norm, a_lam_q1, a_lam_k1, a_lam_q2, a_lam_k2, a_head_norm, w_mem_kv, w_branch_m, w_branch_a, w_branch_x, w_out, ffn2_norm, ffn2_w_gu, ffn2_w_down, final_norm):
    del positions
    batch, seq, d = x.shape
    n = batch * seq
    l = LAYER
    m_inner = m_conv_w.shape[-1]
    a_width = a_head_norm.shape[-1]
    x_width = w_mem_kv.shape[-1] // 2
    bf = lambda a: a.astype(BF16)

    gate_lo = 2 * m_inner
    gate_hi = gate_lo + 2 * M_HEADS
    w_main = bf(jnp.concatenate([w_in[l][:, :gate_lo], w_in[l][:, gate_hi:]], axis=1))
    w_gate = bf(jnp.pad(w_in[l][:, gate_lo:gate_hi], ((0, 0), (0, 128 - 2 * M_HEADS))))

    x0 = x.reshape(n, d)
    x1 = _ffn(x0, ffn1_norm[l], bf(ffn1_w_gu[l]), bf(ffn1_w_down[l]), final_norm, final_norm=False)

    proj, gates = _inproj(x1, mix_norm[l], w_main, w_gate)

    q_m, k_m, v_m = _mlstm_qkv(proj, m_conv_w[l], m_conv_b[l], bf(m_wq[l]), bf(m_wk[l]), bf(m_wv[l]),
                               batch=batch, seq=seq)
    gates_t = gates[:, :2 * M_HEADS].reshape(batch, seq, 2, M_HEADS).transpose(0, 3, 2, 1)
    m_hd = m_inner // M_HEADS
    y_m = _mlstm_scan(q_m, k_m, v_m, gates_t, b_if[l], proj, m_inner // m_hd, m_head_norm[l],
                      batch=batch, seq=seq)

    a_hw = a_width // A_HEADS
    qa_col = 2 * m_inner // a_hw
    lam_vecs = jnp.stack([a_lam_q1[l], a_lam_k1[l], a_lam_q2[l], a_lam_k2[l]])
    y_a = _diff_attn(proj, qa_col, qa_col + A_HEADS, qa_col + 2 * A_HEADS, rel_bias, lam_vecs,
                     a_head_norm[l], batch=batch, seq=seq)

    kx, vx = _mem_kv(mem, mem_norm[l], bf(w_mem_kv[l]))
    qx_col = (2 * m_inner + 3 * a_width) // d
    x2 = _merge(x1, proj, qx_col, qx_col + x_width // d, y_m, y_a, kx, vx, b_gate[l],
                bf(w_branch_m[l]), bf(w_branch_a[l]), bf(w_branch_x[l]), bf(w_out[l]),
                batch=batch, seq=seq)

    out = _ffn(x2, ffn2_norm[l], bf(ffn2_w_gu[l]), bf(ffn2_w_down[l]), final_norm, final_norm=True)
    return out.reshape(batch, seq, d)
```

```python
import functools
import math

import jax
import jax.numpy as jnp
from jax import lax
from jax.experimental import pallas as pl
from jax.experimental.pallas import tpu as pltpu

EPS = 1e-6
BF16 = jnp.bfloat16
F32 = jnp.float32
NEG_BIG = -1e30

M_HEADS = 4
M_CONV = 4
A_HEADS = 8
X_HEADS = 4
N_BRANCH = 3
REL_BUCKETS = 32
REL_MAX_DIST = 128
LAYER = 0

VMEM_LIMIT_BYTES = 56 * 1024 * 1024

FFN_ROWS = 512
INPROJ_ROWS = 1024
INPROJ_COLS = 1024
QKV_ROWS = 512
MLSTM_CHUNK = 256
ATTN_TILE = 256
ATTN_HEAD_GROUP = 4
ATTN_STRIP = 128
ATTN_PV_COLS = 256
ATTN_ONES_ROWS = 16
LOG2_E = math.log2(math.e)
MERGE_ROWS = 512
CONV_HALO = 16


def _params(*sem, flags=None):
    return pltpu.CompilerParams(dimension_semantics=sem, vmem_limit_bytes=VMEM_LIMIT_BYTES, flags=flags)


def _resident(shape):
    zeros = (0,) * len(shape)
    return pl.BlockSpec(shape, lambda *_: zeros, pipeline_mode=pl.Buffered(1))


def _rms(x):
    return x * lax.rsqrt(jnp.mean(x * x, axis=-1, keepdims=True) + EPS)


def _sigmoid(x):
    return 1.0 / (1.0 + jnp.exp(-x))


def _ffn_kernel(x_ref, g_ref, wgu_ref, wd_ref, fg_ref, o_ref, *, d_ff, final_norm):
    x = x_ref[...]
    hb = (_rms(x) * g_ref[...]).astype(BF16)
    gate = jnp.dot(hb, wgu_ref[:, :d_ff], preferred_element_type=F32)
    up = jnp.dot(hb, wgu_ref[:, d_ff:], preferred_element_type=F32)
    act = (gate * _sigmoid(gate) * up).astype(BF16)
    y = x + 0.5 * jnp.dot(act, wd_ref[...], preferred_element_type=F32)
    if final_norm:
        y = _rms(y) * fg_ref[...]
    o_ref[...] = y


def _ffn(x, gain, w_gu, w_down, final_gain, *, final_norm):
    n, d = x.shape
    d_ff = w_down.shape[0]
    tm = min(FFN_ROWS, n)
    return pl.pallas_call(
        functools.partial(_ffn_kernel, d_ff=d_ff, final_norm=final_norm),
        grid=(n // tm,),
        in_specs=[
            pl.BlockSpec((tm, d), lambda i: (i, 0)),
            _resident((1, d)),
            _resident((d, 2 * d_ff)),
            _resident((d_ff, d)),
            _resident((1, d)),
        ],
        out_specs=pl.BlockSpec((tm, d), lambda i: (i, 0)),
        out_shape=jax.ShapeDtypeStruct((n, d), F32),
        compiler_params=_params("parallel"),
        name="ffn_final" if final_norm else "ffn",
    )(x, gain.reshape(1, d), w_gu, w_down, final_gain.reshape(1, d))


def _inproj_kernel(x_ref, g_ref, w_ref, wgate_ref, p_ref, gates_ref, h_scr):
    @pl.when(pl.program_id(1) == 0)
    def _():
        hb = (_rms(x_ref[...]) * g_ref[...]).astype(BF16)
        h_scr[...] = hb
        gates_ref[...] = jnp.dot(hb, wgate_ref[...], preferred_element_type=F32)

    p_ref[...] = jnp.dot(h_scr[...], w_ref[...], preferred_element_type=F32).astype(BF16)


def _inproj(x, gain, w_main, w_gate):
    n, d = x.shape
    width = w_main.shape[1]
    tm = min(INPROJ_ROWS, n)
    tn = INPROJ_COLS
    return pl.pallas_call(
        _inproj_kernel,
        grid=(n // tm, width // tn),
        in_specs=[
            pl.BlockSpec((tm, d), lambda i, j: (i, 0)),
            _resident((1, d)),
            pl.BlockSpec((d, tn), lambda i, j: (0, j)),
            _resident(w_gate.shape),
        ],
        out_specs=[
            pl.BlockSpec((tm, tn), lambda i, j: (i, j)),
            pl.BlockSpec((tm, w_gate.shape[1]), lambda i, j: (i, 0)),
        ],
        out_shape=[
            jax.ShapeDtypeStruct((n, width), BF16),
            jax.ShapeDtypeStruct((n, w_gate.shape[1]), F32),
        ],
        scratch_shapes=[pltpu.VMEM((tm, d), BF16)],
        compiler_params=_params("parallel", "arbitrary"),
        name="inproj",
    )(x, gain.reshape(1, d), w_main, w_gate)


def _mlstm_qkv_kernel(x_ref, halo_ref, cw_ref, cb_ref, wq_ref, wk_ref, wv_ref,
                      q_ref, k_ref, v_ref, *, k_scale):
    ts = x_ref.shape[0]
    xb = x_ref[...]
    cur = xb.astype(F32)
    prev = jnp.where(pl.program_id(2) > 0, halo_ref[...].astype(F32), 0.0)
    full = jnp.concatenate([prev, cur], axis=0)
    cw = cw_ref[...]
    conv = cb_ref[...] + cw[M_CONV - 1:M_CONV, :] * cur
    for back in range(1, M_CONV):
        tap = M_CONV - 1 - back
        conv = conv + cw[tap:tap + 1, :] * full[CONV_HALO - back:CONV_HALO - back + ts, :]
    xc = (conv * _sigmoid(conv)).astype(BF16)
    q_ref[...] = jnp.dot(xc, wq_ref[0], preferred_element_type=F32).astype(BF16)
    k_ref[...] = (jnp.dot(xc, wk_ref[0], preferred_element_type=F32) * k_scale).astype(BF16)
    v_ref[...] = jnp.dot(xb, wv_ref[0], preferred_element_type=F32).astype(BF16)


def _mlstm_qkv(proj, conv_w, conv_b, wq, wk, wv, *, batch, seq):
    heads, hd, _ = wq.shape
    n = batch * seq
    ts = min(QKV_ROWS, seq)
    nt = seq // ts
    halo_per_tile = ts // CONV_HALO

    def x_map(h, b, t):
        return (b * nt + t, h)

    def halo_map(h, b, t):
        return (jnp.maximum((b * nt + t) * halo_per_tile - 1, 0), h)

    def w_map(h, b, t):
        return (h, 0, 0)

    out = jax.ShapeDtypeStruct((n, heads * hd), BF16)
    return pl.pallas_call(
        functools.partial(_mlstm_qkv_kernel, k_scale=hd ** -0.5),
        grid=(heads, batch, nt),
        in_specs=[
            pl.BlockSpec((ts, hd), x_map),
            pl.BlockSpec((CONV_HALO, hd), halo_map),
            pl.BlockSpec((M_CONV, hd), lambda h, b, t: (0, h)),
            pl.BlockSpec((1, hd), lambda h, b, t: (0, h)),
            pl.BlockSpec((1, hd, hd), w_map),
            pl.BlockSpec((1, hd, hd), w_map),
            pl.BlockSpec((1, hd, hd), w_map),
        ],
        out_specs=[pl.BlockSpec((ts, hd), x_map)] * 3,
        out_shape=[out, out, out],
        compiler_params=_params("parallel", "parallel", "parallel"),
        name="mlstm_qkv",
    )(proj, proj, conv_w, conv_b.reshape(1, -1), wq, wk, wv)


def _mlstm_scan_kernel(q_ref, k_ref, v_ref, g_ref, bias_ref, o_ref, gain_ref, y_ref,
                       c_scr, n_scr, m_scr):
    @pl.when(pl.program_id(2) == 0)
    def _():
        c_scr[...] = jnp.zeros_like(c_scr)
        n_scr[...] = jnp.zeros_like(n_scr)
        m_scr[...] = jnp.zeros_like(m_scr)

    chunk = q_ref.shape[0]
    q = q_ref[...]
    k = k_ref[...]
    v = v_ref[...]
    gates = g_ref[0, 0] + bias_ref[0]
    ig = gates[0:1, :]
    fpre = gates[1:2, :]
    lf = jnp.minimum(fpre, 0.0) - jnp.log1p(jnp.exp(-jnp.abs(fpre)))

    t_idx = lax.broadcasted_iota(jnp.int32, (chunk, chunk), 0)
    s_idx = lax.broadcasted_iota(jnp.int32, (chunk, chunk), 1)
    causal = s_idx <= t_idx
    diag = s_idx == t_idx

    def to_col(row):
        return jnp.sum(jnp.where(diag, row, 0.0), axis=1, keepdims=True)

    def to_row(col):
        return jnp.sum(jnp.where(diag, col, 0.0), axis=0, keepdims=True)

    b_col = jnp.sum(jnp.where(causal, lf, 0.0), axis=1, keepdims=True)
    b_row = to_row(b_col)
    m_prev = m_scr[...]
    dlog = jnp.where(causal, b_col - b_row + ig, NEG_BIG)
    inter = b_col + m_prev
    m_t = jnp.maximum(inter, jnp.max(dlog, axis=1, keepdims=True))
    w_intra = jnp.exp(dlog - m_t)
    w_inter = jnp.exp(inter - m_t)

    s = lax.dot_general(q, k, (((1,), (1,)), ((), ())), preferred_element_type=F32) * w_intra
    c_prev = c_scr[...]
    n_prev = n_scr[...]
    num = (jnp.dot(s.astype(BF16), v, preferred_element_type=F32)
           + w_inter * jnp.dot(q, c_prev.astype(BF16), preferred_element_type=F32))
    qn = jnp.sum(q.astype(F32) * n_prev, axis=1, keepdims=True)
    den = jnp.sum(s, axis=1, keepdims=True) + w_inter * qn
    h = num / jnp.maximum(jnp.abs(den), jnp.exp(-m_t))

    b_last = b_col[chunk - 1:chunk, :]
    tail = b_last - b_row + ig
    m_new = jnp.maximum(b_last + m_prev, jnp.max(tail, axis=1, keepdims=True))
    decay = jnp.exp(b_last + m_prev - m_new)
    w_s = jnp.exp(to_col(tail) - m_new)
    kw = k.astype(F32) * w_s
    c_scr[...] = decay * c_prev + jnp.dot(kw.T.astype(BF16), v, preferred_element_type=F32)
    n_scr[...] = decay * n_prev + jnp.sum(kw, axis=0, keepdims=True)
    m_scr[...] = m_new

    gate = _sigmoid(o_ref[...].astype(F32))
    y_ref[...] = (gate * (_rms(h) * gain_ref[...])).astype(BF16)


def _mlstm_scan(q, k, v, gates_t, b_if, proj, o_col_block, head_gain, *, batch, seq):
    n, width = q.shape
    heads = gates_t.shape[1]
    hd = width // heads
    chunk = min(MLSTM_CHUNK, seq)
    nc = seq // chunk

    def qkv_map(b, h, c):
        return (b * nc + c, h)

    return pl.pallas_call(
        _mlstm_scan_kernel,
        grid=(batch, heads, nc),
        in_specs=[
            pl.BlockSpec((chunk, hd), qkv_map),
            pl.BlockSpec((chunk, hd), qkv_map),
            pl.BlockSpec((chunk, hd), qkv_map),
            pl.BlockSpec((1, 1, 2, chunk), lambda b, h, c: (b, h, 0, c)),
            pl.BlockSpec((1, 2, 1), lambda b, h, c: (h, 0, 0)),
            pl.BlockSpec((chunk, hd), lambda b, h, c: (b * nc + c, o_col_block + h)),
            pl.BlockSpec((1, hd), lambda b, h, c: (0, h)),
        ],
        out_specs=pl.BlockSpec((chunk, hd), qkv_map),
        out_shape=jax.ShapeDtypeStruct((n, width), BF16),
        scratch_shapes=[pltpu.VMEM((hd, hd), F32), pltpu.VMEM((1, hd), F32), pltpu.VMEM((1, 1), F32)],
        compiler_params=_params("parallel", "parallel", "arbitrary"),
        name="mlstm_scan",
    )(q, k, v, gates_t, b_if.reshape(2, heads, 1).transpose(1, 0, 2), proj, head_gain.reshape(1, -1))


def _diff_attn_kernel(far_ref, q_ref, k_ref, v_ref, band_ref, lam_ref, gain_ref, y_ref,
                      vt_scr, acc_scr, *, out_scale, lam_init):
    group, n_tiles, rows, tile = vt_scr.shape
    width = q_ref.shape[1] // group
    head0 = pl.program_id(1) * group
    qi = pl.program_id(2)
    half = width // 2

    @pl.when(qi == 0)
    def _():
        ones = jnp.ones((rows - width, tile), BF16)
        for g in range(group):
            for j in range(n_tiles):
                v_tile = v_ref[j * tile:(j + 1) * tile, g * width:(g + 1) * width]
                vt_scr[g, j] = jnp.concatenate([v_tile.astype(F32).T.astype(BF16), ones], axis=0)

    q_all = q_ref[...]
    lane = lax.broadcasted_iota(jnp.int32, (tile, width), 1)
    qs = []
    for g in range(group):
        q = q_all[:, g * width:(g + 1) * width].astype(F32) * (half ** -0.5 * LOG2_E)
        stacked = jnp.concatenate([jnp.where(lane < half, q, 0.0), jnp.where(lane >= half, q, 0.0)], axis=0)
        qs.append(stacked.astype(BF16))

    acc_scr[...] = jnp.zeros_like(acc_scr)

    def scores(kj, g):
        start = pl.multiple_of(kj * tile, tile)
        return lax.dot_general(k_ref[pl.ds(start, tile), g * width:(g + 1) * width], qs[g],
                               (((1,), (1,)), ((), ())), preferred_element_type=F32)

    def block(kjs, maxes, shifts, band_idx):
        all_scores = [[scores(kj, g) for g in range(group)] for kj in kjs]
        maxes = list(maxes)
        for t, kj in enumerate(kjs):
            for g in range(group):
                m_prev = maxes[g]
                st = all_scores[t][g]
                if band_idx is not None:
                    band = band_ref[g, band_idx]
                    st = st + jnp.concatenate([band, band], axis=1)
                vt = vt_scr[g, kj]
                m_cols, pvs, alphas = [], [], []
                for c0 in range(0, 2 * tile, ATTN_PV_COLS):
                    p_cols, a_cols = [], []
                    for c in range(c0, c0 + ATTN_PV_COLS, ATTN_STRIP):
                        cols = slice(c, c + ATTN_STRIP)
                        s_c = st[:, cols]
                        m_c = jnp.maximum(m_prev[:, cols], jnp.max(s_c, axis=0, keepdims=True) + shifts[g])
                        a_cols.append(jnp.exp2(m_prev[:, cols] - m_c))
                        p_cols.append(jnp.exp2(s_c - (m_c - shifts[g])).astype(BF16))
                        m_cols.append(m_c)
                    pvs.append(jnp.dot(vt, jnp.concatenate(p_cols, axis=1), preferred_element_type=F32))
                    alphas.append(jnp.concatenate(a_cols, axis=1))
                for i, c0 in enumerate(range(0, 2 * tile, ATTN_PV_COLS)):
                    wide = slice(c0, c0 + ATTN_PV_COLS)
                    acc_scr[g, :, wide] = alphas[i] * acc_scr[g, :, wide] + pvs[i]
                maxes[g] = jnp.concatenate(m_cols, axis=1)
        return tuple(maxes)

    init = tuple(jnp.full((1, 2 * tile), NEG_BIG, F32) for _ in range(group))
    no_shift = (0.0,) * group

    far_bias = tuple(far_ref[head0 + g] * LOG2_E for g in range(group))
    n_far = jnp.maximum(qi - 1, 0)
    n_pairs = n_far // 2
    maxes = lax.fori_loop(0, n_pairs, lambda i, c: block((2 * i, 2 * i + 1), c, far_bias, None), init)
    maxes = lax.cond(n_far > 2 * n_pairs, lambda: block((n_far - 1,), maxes, far_bias, None), lambda: maxes)
    maxes = lax.cond(qi > 0, lambda: block((qi - 1,), maxes, no_shift, 0), lambda: maxes)
    block((qi,), maxes, no_shift, 1)

    lam_v = lam_ref[...]
    lam = (jnp.exp(jnp.sum(lam_v[0:1, :] * lam_v[1:2, :], axis=1, keepdims=True))
           - jnp.exp(jnp.sum(lam_v[2:3, :] * lam_v[3:4, :], axis=1, keepdims=True)) + lam_init)
    for g in range(group):
        o = acc_scr[g, :width, :] / acc_scr[g, width:width + 1, :]
        out = (o[:, :tile] - lam * o[:, tile:]).T
        gain = gain_ref[:, g * width:(g + 1) * width]
        y_ref[:, g * width:(g + 1) * width] = (_rms(out) * gain * out_scale).astype(BF16)


def _t5_bucket(n):
    max_exact = REL_BUCKETS // 2
    nf = jnp.maximum(n, 1).astype(F32)
    large = max_exact + (jnp.log(nf / max_exact) / math.log(REL_MAX_DIST / max_exact)
                         * (REL_BUCKETS - max_exact)).astype(jnp.int32)
    large = jnp.minimum(large, REL_BUCKETS - 1)
    return jnp.where(n < max_exact, n, large)


def _diff_attn(proj, q_col, k_col, v_col, rel_bias, lam_vecs, head_gain, *, batch, seq):
    n = proj.shape[0]
    heads = rel_bias.shape[1]
    hw = head_gain.shape[0] // heads
    tile = min(ATTN_TILE, seq)
    nq = seq // tile
    assert tile >= REL_MAX_DIST, "distances beyond one tile must all fall in the last bucket"
    group = ATTN_HEAD_GROUP
    gw = group * hw
    assert heads % group == 0 and q_col % group == 0 and k_col % group == 0 and v_col % group == 0

    kk = jnp.arange(tile, dtype=jnp.int32)[:, None]
    qq = jnp.arange(tile, dtype=jnp.int32)[None, :]
    dist = jnp.stack([tile + qq - kk, qq - kk])
    bucket = _t5_bucket(jnp.maximum(dist, 0))
    band = jnp.zeros((heads,) + dist.shape, F32)
    for b in range(REL_BUCKETS):
        band = jnp.where(bucket[None] == b, rel_bias[b].astype(F32)[:, None, None, None], band)
    band = jnp.where(dist[None] >= 0, band * LOG2_E, NEG_BIG)
    far = rel_bias[REL_BUCKETS - 1].astype(F32)

    lam_init = 0.8 - 0.6 * math.exp(-0.3 * LAYER)
    return pl.pallas_call(
        functools.partial(_diff_attn_kernel, out_scale=1.0 - lam_init, lam_init=lam_init),
        grid=(batch, heads // group, nq),
        in_specs=[
            pl.BlockSpec(memory_space=pltpu.SMEM),
            pl.BlockSpec((tile, gw), lambda b, h, i: (b * nq + i, q_col // group + h)),
            pl.BlockSpec((seq, gw), lambda b, h, i: (b, k_col // group + h)),
            pl.BlockSpec((seq, gw), lambda b, h, i: (b, v_col // group + h)),
            pl.BlockSpec((group, 2, tile, tile), lambda b, h, i: (h, 0, 0, 0)),
            _resident(lam_vecs.shape),
            pl.BlockSpec((1, gw), lambda b, h, i: (0, h)),
        ],
        out_specs=pl.BlockSpec((tile, gw), lambda b, h, i: (b * nq + i, h)),
        out_shape=jax.ShapeDtypeStruct((n, heads * hw), BF16),
        scratch_shapes=[
            pltpu.VMEM((group, nq, hw + ATTN_ONES_ROWS, tile), BF16),
            pltpu.VMEM((group, hw + ATTN_ONES_ROWS, 2 * tile), F32),
        ],
        compiler_params=_params("parallel", "parallel", "arbitrary"),
        name="diff_attn",
    )(far, proj, proj, proj, band, lam_vecs, head_gain.reshape(1, -1))


def _mem_kv_kernel(mem_ref, g_ref, w_ref, k_ref, v_ref):
    width = k_ref.shape[-1]
    hb = (_rms(mem_ref[0]) * g_ref[...]).astype(BF16)
    k_ref[0] = jnp.dot(hb, w_ref[:, :width], preferred_element_type=F32).astype(BF16)
    v_ref[0] = jnp.dot(hb, w_ref[:, width:], preferred_element_type=F32).astype(BF16)


def _mem_kv(mem, gain, w_kv):
    batch, mlen, d = mem.shape
    width = w_kv.shape[1] // 2
    out = jax.ShapeDtypeStruct((batch, mlen, width), BF16)
    return pl.pallas_call(
        _mem_kv_kernel,
        grid=(batch,),
        in_specs=[
            pl.BlockSpec((1, mlen, d), lambda b: (b, 0, 0)),
            _resident((1, d)),
            _resident(w_kv.shape),
        ],
        out_specs=[pl.BlockSpec((1, mlen, width), lambda b: (b, 0, 0))] * 2,
        out_shape=[out, out],
        compiler_params=_params("parallel"),
        name="mem_kv",
    )(mem, gain.reshape(1, d), w_kv)


def _merge_kernel(x_ref, qx_ref, g0_ref, g1_ref, g2_ref, ym_ref, ya_ref, kx_ref, vx_ref,
                  bg_ref, wm_ref, wa_ref, wx_ref, wo_ref, o_ref):
    d = x_ref.shape[1]
    xw = qx_ref.shape[1]
    xhd = xw // X_HEADS
    heads = []
    for h in range(X_HEADS):
        cols = slice(h * xhd, (h + 1) * xhd)
        s = lax.dot_general(qx_ref[:, cols], kx_ref[0, :, cols], (((1,), (1,)), ((), ())),
                            preferred_element_type=F32) * (xhd ** -0.5)
        p = jnp.exp(s - jnp.max(s, axis=1, keepdims=True))
        pv = jnp.dot(p.astype(BF16), vx_ref[0, :, cols], preferred_element_type=F32)
        heads.append((pv / jnp.sum(p, axis=1, keepdims=True)).astype(BF16))
    yx = jnp.concatenate(heads, axis=1)

    def gate(g_ref, k):
        return _sigmoid(g_ref[...].astype(F32) + bg_ref[:, k * d:(k + 1) * d])

    merged = (gate(g0_ref, 0) * jnp.dot(ym_ref[...], wm_ref[...], preferred_element_type=F32)
              + gate(g1_ref, 1) * jnp.dot(ya_ref[...], wa_ref[...], preferred_element_type=F32)
              + gate(g2_ref, 2) * jnp.dot(yx, wx_ref[...], preferred_element_type=F32))
    o_ref[...] = x_ref[...] + jnp.dot(merged.astype(BF16), wo_ref[...], preferred_element_type=F32)


def _merge(x, proj, qx_col, g_col, y_m, y_a, kx, vx, b_gate, w_m, w_a, w_x, w_o, *, batch, seq):
    n, d = x.shape
    tm = min(MERGE_ROWS, seq)
    nt = seq // tm

    def row(b, t):
        return b * nt + t

    def proj_spec(col):
        return pl.BlockSpec((tm, d), lambda b, t: (row(b, t), col))

    def mem_spec(a):
        return pl.BlockSpec((1,) + a.shape[1:], lambda b, t: (b, 0, 0))

    return pl.pallas_call(
        _merge_kernel,
        grid=(batch, nt),
        in_specs=[
            pl.BlockSpec((tm, d), lambda b, t: (row(b, t), 0)),
            proj_spec(qx_col), proj_spec(g_col), proj_spec(g_col + 1), proj_spec(g_col + 2),
            pl.BlockSpec((tm, y_m.shape[1]), lambda b, t: (row(b, t), 0)),
            pl.BlockSpec((tm, y_a.shape[1]), lambda b, t: (row(b, t), 0)),
            mem_spec(kx), mem_spec(vx),
            _resident((1, b_gate.shape[0])),
            _resident(w_m.shape), _resident(w_a.shape), _resident(w_x.shape), _resident(w_o.shape),
        ],
        out_specs=pl.BlockSpec((tm, d), lambda b, t: (row(b, t), 0)),
        out_shape=jax.ShapeDtypeStruct((n, d), F32),
        compiler_params=_params("parallel", "parallel"),
        name="merge",
    )(x, proj, proj, proj, proj, y_m, y_a, kx, vx, b_gate.reshape(1, -1), w_m, w_a, w_x, w_o)


def kernel(x, mem, positions, rel_bias, ffn1_norm, ffn1_w_gu, ffn1_w_down, mix_norm, mem_norm, w_in, b_if, b_gate, m_conv_w, m_conv_b, m_wq, m_wk, m_wv, m_head_norm, a_lam_q1, a_lam_k1, a_lam_q2, a_lam_k2, a_head_norm, w_mem_kv, w_branch_m, w_branch_a, w_branch_x, w_out, ffn2_norm, ffn2_w_gu, ffn2_w_down, final_norm):
    del positions
    batch, seq, d = x.shape
    n = batch * seq
    l = LAYER
    m_inner = m_conv_w.shape[-1]
    a_width = a_head_norm.shape[-1]
    x_width = w_mem_kv.shape[-1] // 2
    bf = lambda a: a.astype(BF16)

    gate_lo = 2 * m_inner
    gate_hi = gate_lo + 2 * M_HEADS
    w_main = bf(jnp.concatenate([w_in[l][:, :gate_lo], w_in[l][:, gate_hi:]], axis=1))
    w_gate = bf(jnp.pad(w_in[l][:, gate_lo:gate_hi], ((0, 0), (0, 128 - 2 * M_HEADS))))

    x0 = x.reshape(n, d)
    x1 = _ffn(x0, ffn1_norm[l], bf(ffn1_w_gu[l]), bf(ffn1_w_down[l]), final_norm, final_norm=False)

    proj, gates = _inproj(x1, mix_norm[l], w_main, w_gate)

    q_m, k_m, v_m = _mlstm_qkv(proj, m_conv_w[l], m_conv_b[l], bf(m_wq[l]), bf(m_wk[l]), bf(m_wv[l]),
                               batch=batch, seq=seq)
    gates_t = gates[:, :2 * M_HEADS].reshape(batch, seq, 2, M_HEADS).transpose(0, 3, 2, 1)
    m_hd = m_inner // M_HEADS
    y_m = _mlstm_scan(q_m, k_m, v_m, gates_t, b_if[l], proj, m_inner // m_hd, m_head_norm[l],
                      batch=batch, seq=seq)

    a_hw = a_width // A_HEADS
    qa_col = 2 * m_inner // a_hw
    lam_vecs = jnp.stack([a_lam_q1[l], a_lam_k1[l], a_lam_q2[l], a_lam_k2[l]])
    y_a = _diff_attn(proj, qa_col, qa_col + A_HEADS, qa_col + 2 * A_HEADS, rel_bias, lam_vecs,
                     a_head_norm[l], batch=batch, seq=seq)

    kx, vx = _mem_kv(mem, mem_norm[l], bf(w_mem_kv[l]))
    qx_col = (2 * m_inner + 3 * a_width) // d
    x2 = _merge(x1, proj, qx_col, qx_col + x_width // d, y_m, y_a, kx, vx, b_gate[l],
                bf(w_branch_m[l]), bf(w_branch_a[l]), bf(w_branch_x[l]), bf(w_out[l]),
                batch=batch, seq=seq)

    out = _ffn(x2, ffn2_norm[l], bf(ffn2_w_gu[l]), bf(ffn2_w_down[l]), final_norm, final_norm=True)
    return out.reshape(batch, seq, d)
```

```python
import functools
import math

import jax
import jax.numpy as jnp
from jax import lax
from jax.experimental import pallas as pl
from jax.experimental.pallas import tpu as pltpu

EPS = 1e-6
BF16 = jnp.bfloat16
F32 = jnp.float32
NEG_BIG = -1e30

M_HEADS = 4
M_CONV = 4
A_HEADS = 8
X_HEADS = 4
N_BRANCH = 3
REL_BUCKETS = 32
REL_MAX_DIST = 128
LAYER = 0

VMEM_LIMIT_BYTES = 56 * 1024 * 1024

FFN_ROWS = 512
INPROJ_ROWS = 1024
INPROJ_COLS = 1024
QKV_ROWS = 512
MLSTM_CHUNK = 256
ATTN_TILE = 256
ATTN_HEAD_GROUP = 4
ATTN_STRIP = 128
ATTN_PV_COLS = 256
ATTN_ONES_ROWS = 16
ATTN_LOOKAHEAD = 8
LOG2_E = math.log2(math.e)
MERGE_ROWS = 512
CONV_HALO = 16


def _params(*sem, flags=None):
    return pltpu.CompilerParams(dimension_semantics=sem, vmem_limit_bytes=VMEM_LIMIT_BYTES, flags=flags)


def _resident(shape):
    zeros = (0,) * len(shape)
    return pl.BlockSpec(shape, lambda *_: zeros, pipeline_mode=pl.Buffered(1))


def _rms(x):
    return x * lax.rsqrt(jnp.mean(x * x, axis=-1, keepdims=True) + EPS)


def _sigmoid(x):
    return 1.0 / (1.0 + jnp.exp(-x))


def _ffn_kernel(x_ref, g_ref, wgu_ref, wd_ref, fg_ref, o_ref, *, d_ff, final_norm):
    x = x_ref[...]
    hb = (_rms(x) * g_ref[...]).astype(BF16)
    gate = jnp.dot(hb, wgu_ref[:, :d_ff], preferred_element_type=F32)
    up = jnp.dot(hb, wgu_ref[:, d_ff:], preferred_element_type=F32)
    act = (gate * _sigmoid(gate) * up).astype(BF16)
    y = x + 0.5 * jnp.dot(act, wd_ref[...], preferred_element_type=F32)
    if final_norm:
        y = _rms(y) * fg_ref[...]
    o_ref[...] = y


def _ffn(x, gain, w_gu, w_down, final_gain, *, final_norm):
    n, d = x.shape
    d_ff = w_down.shape[0]
    tm = min(FFN_ROWS, n)
    return pl.pallas_call(
        functools.partial(_ffn_kernel, d_ff=d_ff, final_norm=final_norm),
        grid=(n // tm,),
        in_specs=[
            pl.BlockSpec((tm, d), lambda i: (i, 0)),
            _resident((1, d)),
            _resident((d, 2 * d_ff)),
            _resident((d_ff, d)),
            _resident((1, d)),
        ],
        out_specs=pl.BlockSpec((tm, d), lambda i: (i, 0)),
        out_shape=jax.ShapeDtypeStruct((n, d), F32),
        compiler_params=_params("parallel"),
        name="ffn_final" if final_norm else "ffn",
    )(x, gain.reshape(1, d), w_gu, w_down, final_gain.reshape(1, d))


def _inproj_kernel(x_ref, g_ref, w_ref, wgate_ref, p_ref, gates_ref, h_scr):
    @pl.when(pl.program_id(1) == 0)
    def _():
        hb = (_rms(x_ref[...]) * g_ref[...]).astype(BF16)
        h_scr[...] = hb
        gates_ref[...] = jnp.dot(hb, wgate_ref[...], preferred_element_type=F32)

    w = w_ref[pl.program_id(1)]
    p_ref[...] = jnp.dot(h_scr[...], w, preferred_element_type=F32).astype(BF16)


def _inproj(x, gain, w_main, w_gate):
    n, d = x.shape
    width = w_main.shape[1]
    tm = min(INPROJ_ROWS, n)
    tn = INPROJ_COLS
    w_blocks = w_main.reshape(d, width // tn, tn).transpose(1, 0, 2)
    return pl.pallas_call(
        _inproj_kernel,
        grid=(n // tm, width // tn),
        in_specs=[
            pl.BlockSpec((tm, d), lambda i, j: (i, 0)),
            _resident((1, d)),
            _resident(w_blocks.shape),
            _resident(w_gate.shape),
        ],
        out_specs=[
            pl.BlockSpec((tm, tn), lambda i, j: (i, j)),
            pl.BlockSpec((tm, w_gate.shape[1]), lambda i, j: (i, 0)),
        ],
        out_shape=[
            jax.ShapeDtypeStruct((n, width), BF16),
            jax.ShapeDtypeStruct((n, w_gate.shape[1]), F32),
        ],
        scratch_shapes=[pltpu.VMEM((tm, d), BF16)],
        compiler_params=_params("parallel", "arbitrary"),
        name="inproj",
    )(x, gain.reshape(1, d), w_blocks, w_gate)


def _mlstm_qkv_kernel(x_ref, halo_ref, cw_ref, cb_ref, wq_ref, wk_ref, wv_ref,
                      q_ref, k_ref, v_ref, *, k_scale):
    ts = x_ref.shape[0]
    xb = x_ref[...]
    v_ref[...] = jnp.dot(xb, wv_ref[0], preferred_element_type=F32).astype(BF16)
    cur = xb.astype(F32)
    prev = jnp.where(pl.program_id(2) > 0, halo_ref[...].astype(F32), 0.0)
    full = jnp.concatenate([prev, cur], axis=0)
    cw = cw_ref[...]
    conv = cb_ref[...] + cw[M_CONV - 1:M_CONV, :] * cur
    for back in range(1, M_CONV):
        tap = M_CONV - 1 - back
        conv = conv + cw[tap:tap + 1, :] * full[CONV_HALO - back:CONV_HALO - back + ts, :]
    xc = (conv * _sigmoid(conv)).astype(BF16)
    q_ref[...] = jnp.dot(xc, wq_ref[0], preferred_element_type=F32).astype(BF16)
    k_ref[...] = (jnp.dot(xc, wk_ref[0], preferred_element_type=F32) * k_scale).astype(BF16)


def _mlstm_qkv(proj, conv_w, conv_b, wq, wk, wv, *, batch, seq):
    heads, hd, _ = wq.shape
    n = batch * seq
    ts = min(QKV_ROWS, seq)
    nt = seq // ts
    halo_per_tile = ts // CONV_HALO

    def x_map(h, b, t):
        return (b * nt + t, h)

    def halo_map(h, b, t):
        return (jnp.maximum((b * nt + t) * halo_per_tile - 1, 0), h)

    def w_map(h, b, t):
        return (h, 0, 0)

    out = jax.ShapeDtypeStruct((n, heads * hd), BF16)
    return pl.pallas_call(
        functools.partial(_mlstm_qkv_kernel, k_scale=hd ** -0.5),
        grid=(heads, batch, nt),
        in_specs=[
            pl.BlockSpec((ts, hd), x_map),
            pl.BlockSpec((CONV_HALO, hd), halo_map),
            pl.BlockSpec((M_CONV, hd), lambda h, b, t: (0, h)),
            pl.BlockSpec((1, hd), lambda h, b, t: (0, h)),
            pl.BlockSpec((1, hd, hd), w_map),
            pl.BlockSpec((1, hd, hd), w_map),
            pl.BlockSpec((1, hd, hd), w_map),
        ],
        out_specs=[pl.BlockSpec((ts, hd), x_map)] * 3,
        out_shape=[out, out, out],
        compiler_params=_params("parallel", "parallel", "parallel"),
        name="mlstm_qkv",
    )(proj, proj, conv_w, conv_b.reshape(1, -1), wq, wk, wv)


def _mlstm_scan_kernel(q_ref, k_ref, v_ref, g_ref, bias_ref, o_ref, gain_ref, y_ref,
                       c_scr, n_scr, m_scr):
    @pl.when(pl.program_id(2) == 0)
    def _():
        c_scr[...] = jnp.zeros_like(c_scr)
        n_scr[...] = jnp.zeros_like(n_scr)
        m_scr[...] = jnp.zeros_like(m_scr)

    chunk = q_ref.shape[0]
    q = q_ref[...]
    k = k_ref[...]
    v = v_ref[...]
    gates = g_ref[0, 0] + bias_ref[0]
    ig = gates[0:1, :]
    fpre = gates[1:2, :]
    lf = jnp.minimum(fpre, 0.0) - jnp.log1p(jnp.exp(-jnp.abs(fpre)))

    t_idx = lax.broadcasted_iota(jnp.int32, (chunk, chunk), 0)
    s_idx = lax.broadcasted_iota(jnp.int32, (chunk, chunk), 1)
    causal = s_idx <= t_idx
    diag = s_idx == t_idx

    def to_col(row):
        return jnp.sum(jnp.where(diag, row, 0.0), axis=1, keepdims=True)

    def to_row(col):
        return jnp.sum(jnp.where(diag, col, 0.0), axis=0, keepdims=True)

    b_col = jnp.sum(jnp.where(causal, lf, 0.0), axis=1, keepdims=True)
    b_row = to_row(b_col)
    m_prev = m_scr[...]
    dlog = jnp.where(causal, b_col - b_row + ig, NEG_BIG)
    inter = b_col + m_prev
    m_t = jnp.maximum(inter, jnp.max(dlog, axis=1, keepdims=True))
    w_intra = jnp.exp(dlog - m_t)
    w_inter = jnp.exp(inter - m_t)

    s = lax.dot_general(q, k, (((1,), (1,)), ((), ())), preferred_element_type=F32) * w_intra
    c_prev = c_scr[...]
    n_prev = n_scr[...]
    num = (jnp.dot(s.astype(BF16), v, preferred_element_type=F32)
           + w_inter * jnp.dot(q, c_prev.astype(BF16), preferred_element_type=F32))
    qn = jnp.sum(q.astype(F32) * n_prev, axis=1, keepdims=True)
    den = jnp.sum(s, axis=1, keepdims=True) + w_inter * qn
    h = num / jnp.maximum(jnp.abs(den), jnp.exp(-m_t))

    b_last = b_col[chunk - 1:chunk, :]
    tail = b_last - b_row + ig
    m_new = jnp.maximum(b_last + m_prev, jnp.max(tail, axis=1, keepdims=True))
    decay = jnp.exp(b_last + m_prev - m_new)
    w_s = jnp.exp(to_col(tail) - m_new)
    kw = k.astype(F32) * w_s
    c_scr[...] = decay * c_prev + jnp.dot(kw.T.astype(BF16), v, preferred_element_type=F32)
    n_scr[...] = decay * n_prev + jnp.sum(kw, axis=0, keepdims=True)
    m_scr[...] = m_new

    gate = _sigmoid(o_ref[...].astype(F32))
    y_ref[...] = (gate * (_rms(h) * gain_ref[...])).astype(BF16)


def _mlstm_scan(q, k, v, gates_t, b_if, proj, o_col_block, head_gain, *, batch, seq):
    n, width = q.shape
    heads = gates_t.shape[1]
    hd = width // heads
    chunk = min(MLSTM_CHUNK, seq)
    nc = seq // chunk

    def qkv_map(b, h, c):
        return (b * nc + c, h)

    return pl.pallas_call(
        _mlstm_scan_kernel,
        grid=(batch, heads, nc),
        in_specs=[
            pl.BlockSpec((chunk, hd), qkv_map),
            pl.BlockSpec((chunk, hd), qkv_map),
            pl.BlockSpec((chunk, hd), qkv_map),
            pl.BlockSpec((1, 1, 2, chunk), lambda b, h, c: (b, h, 0, c)),
            pl.BlockSpec((1, 2, 1), lambda b, h, c: (h, 0, 0)),
            pl.BlockSpec((chunk, hd), lambda b, h, c: (b * nc + c, o_col_block + h)),
            pl.BlockSpec((1, hd), lambda b, h, c: (0, h)),
        ],
        out_specs=pl.BlockSpec((chunk, hd), qkv_map),
        out_shape=jax.ShapeDtypeStruct((n, width), BF16),
        scratch_shapes=[pltpu.VMEM((hd, hd), F32), pltpu.VMEM((1, hd), F32), pltpu.VMEM((1, 1), F32)],
        compiler_params=_params("parallel", "parallel", "arbitrary"),
        name="mlstm_scan",
    )(q, k, v, gates_t, b_if.reshape(2, heads, 1).transpose(1, 0, 2), proj, head_gain.reshape(1, -1))


def _diff_attn_kernel(far_ref, q_ref, k_ref, v_ref, band_ref, lam_ref, gain_ref, y_ref,
                      vt_scr, acc_scr, *, out_scale, lam_init):
    group, n_tiles, rows, tile = vt_scr.shape
    width = q_ref.shape[1] // group
    head0 = pl.program_id(1) * group
    qi = pl.program_id(2)
    half = width // 2

    @pl.when(qi == 0)
    def _():
        ones = jnp.ones((rows - width, tile), BF16)
        for g in range(group):
            for j in range(n_tiles):
                v_tile = v_ref[j * tile:(j + 1) * tile, g * width:(g + 1) * width]
                vt_scr[g, j] = jnp.concatenate([v_tile.astype(F32).T.astype(BF16), ones], axis=0)

    q_all = q_ref[...]
    lane = lax.broadcasted_iota(jnp.int32, (tile, width), 1)
    qs = []
    for g in range(group):
        q = q_all[:, g * width:(g + 1) * width].astype(F32) * (half ** -0.5 * LOG2_E)
        stacked = jnp.concatenate([jnp.where(lane < half, q, 0.0), jnp.where(lane >= half, q, 0.0)], axis=0)
        qs.append(stacked.astype(BF16))

    acc_scr[...] = jnp.zeros_like(acc_scr)

    def scores(kj, g):
        start = pl.multiple_of(kj * tile, tile)
        return lax.dot_general(k_ref[pl.ds(start, tile), g * width:(g + 1) * width], qs[g],
                               (((1,), (1,)), ((), ())), preferred_element_type=F32)

    def block(kjs, maxes, shifts, band_idx):
        units = [(kj, g) for kj in kjs for g in range(group)]
        ready = [scores(*u) for u in units[:ATTN_LOOKAHEAD]]
        maxes = list(maxes)
        for idx, (kj, g) in enumerate(units):
            if True:
                m_prev = maxes[g]
                st = ready[idx]
                if band_idx is not None:
                    band = band_ref[g, band_idx]
                    st = st + jnp.concatenate([band, band], axis=1)
                vt = vt_scr[g, kj]
                m_cols, pvs, alphas = [], [], []
                for c0 in range(0, 2 * tile, ATTN_PV_COLS):
                    p_cols, a_cols = [], []
                    for c in range(c0, c0 + ATTN_PV_COLS, ATTN_STRIP):
                        cols = slice(c, c + ATTN_STRIP)
                        s_c = st[:, cols]
                        m_c = jnp.maximum(m_prev[:, cols], jnp.max(s_c, axis=0, keepdims=True) + shifts[g])
                        a_cols.append(jnp.exp2(m_prev[:, cols] - m_c))
                        p_cols.append(jnp.exp2(s_c - (m_c - shifts[g])).astype(BF16))
                        m_cols.append(m_c)
                    pvs.append(jnp.dot(vt, jnp.concatenate(p_cols, axis=1), preferred_element_type=F32))
                    alphas.append(jnp.concatenate(a_cols, axis=1))
                if idx + ATTN_LOOKAHEAD < len(units):
                    ready.append(scores(*units[idx + ATTN_LOOKAHEAD]))
                for i, c0 in enumerate(range(0, 2 * tile, ATTN_PV_COLS)):
                    wide = slice(c0, c0 + ATTN_PV_COLS)
                    acc_scr[g, :, wide] = alphas[i] * acc_scr[g, :, wide] + pvs[i]
                maxes[g] = jnp.concatenate(m_cols, axis=1)
        return tuple(maxes)

    init = tuple(jnp.full((1, 2 * tile), NEG_BIG, F32) for _ in range(group))
    no_shift = (0.0,) * group

    far_bias = tuple(far_ref[head0 + g] * LOG2_E for g in range(group))
    n_far = jnp.maximum(qi - 1, 0)
    n_pairs = n_far // 2
    maxes = lax.fori_loop(0, n_pairs, lambda i, c: block((2 * i, 2 * i + 1), c, far_bias, None), init)
    maxes = lax.cond(n_far > 2 * n_pairs, lambda: block((n_far - 1,), maxes, far_bias, None), lambda: maxes)
    maxes = lax.cond(qi > 0, lambda: block((qi - 1,), maxes, no_shift, 0), lambda: maxes)
    block((qi,), maxes, no_shift, 1)

    lam_v = lam_ref[...]
    lam = (jnp.exp(jnp.sum(lam_v[0:1, :] * lam_v[1:2, :], axis=1, keepdims=True))
           - jnp.exp(jnp.sum(lam_v[2:3, :] * lam_v[3:4, :], axis=1, keepdims=True)) + lam_init)
    for g in range(group):
        o = acc_scr[g, :width, :] / acc_scr[g, width:width + 1, :]
        out = (o[:, :tile] - lam * o[:, tile:]).T
        gain = gain_ref[:, g * width:(g + 1) * width]
        y_ref[:, g * width:(g + 1) * width] = (_rms(out) * gain * out_scale).astype(BF16)


def _t5_bucket(n):
    max_exact = REL_BUCKETS // 2
    nf = jnp.maximum(n, 1).astype(F32)
    large = max_exact + (jnp.log(nf / max_exact) / math.log(REL_MAX_DIST / max_exact)
                         * (REL_BUCKETS - max_exact)).astype(jnp.int32)
    large = jnp.minimum(large, REL_BUCKETS - 1)
    return jnp.where(n < max_exact, n, large)


def _diff_attn(proj, q_col, k_col, v_col, rel_bias, lam_vecs, head_gain, *, batch, seq):
    n = proj.shape[0]
    heads = rel_bias.shape[1]
    hw = head_gain.shape[0] // heads
    tile = min(ATTN_TILE, seq)
    nq = seq // tile
    assert tile >= REL_MAX_DIST, "distances beyond one tile must all fall in the last bucket"
    group = ATTN_HEAD_GROUP
    gw = group * hw
    assert heads % group == 0 and q_col % group == 0 and k_col % group == 0 and v_col % group == 0

    kk = jnp.arange(tile, dtype=jnp.int32)[:, None]
    qq = jnp.arange(tile, dtype=jnp.int32)[None, :]
    dist = jnp.stack([tile + qq - kk, qq - kk])
    bucket = _t5_bucket(jnp.maximum(dist, 0))
    band = jnp.zeros((heads,) + dist.shape, F32)
    for b in range(REL_BUCKETS):
        band = jnp.where(bucket[None] == b, rel_bias[b].astype(F32)[:, None, None, None], band)
    band = jnp.where(dist[None] >= 0, band * LOG2_E, NEG_BIG)
    far = rel_bias[REL_BUCKETS - 1].astype(F32)

    lam_init = 0.8 - 0.6 * math.exp(-0.3 * LAYER)
    return pl.pallas_call(
        functools.partial(_diff_attn_kernel, out_scale=1.0 - lam_init, lam_init=lam_init),
        grid=(batch, heads // group, nq),
        in_specs=[
            pl.BlockSpec(memory_space=pltpu.SMEM),
            pl.BlockSpec((tile, gw), lambda b, h, i: (b * nq + i, q_col // group + h)),
            pl.BlockSpec((seq, gw), lambda b, h, i: (b, k_col // group + h)),
            pl.BlockSpec((seq, gw), lambda b, h, i: (b, v_col // group + h)),
            pl.BlockSpec((group, 2, tile, tile), lambda b, h, i: (h, 0, 0, 0)),
            _resident(lam_vecs.shape),
            pl.BlockSpec((1, gw), lambda b, h, i: (0, h)),
        ],
        out_specs=pl.BlockSpec((tile, gw), lambda b, h, i: (b * nq + i, h)),
        out_shape=jax.ShapeDtypeStruct((n, heads * hw), BF16),
        scratch_shapes=[
            pltpu.VMEM((group, nq, hw + ATTN_ONES_ROWS, tile), BF16),
            pltpu.VMEM((group, hw + ATTN_ONES_ROWS, 2 * tile), F32),
        ],
        compiler_params=_params("parallel", "parallel", "arbitrary"),
        name="diff_attn",
    )(far, proj, proj, proj, band, lam_vecs, head_gain.reshape(1, -1))


def _mem_kv_kernel(mem_ref, g_ref, w_ref, k_ref, v_ref):
    width = k_ref.shape[-1]
    hb = (_rms(mem_ref[0]) * g_ref[...]).astype(BF16)
    k_ref[0] = jnp.dot(hb, w_ref[:, :width], preferred_element_type=F32).astype(BF16)
    v_ref[0] = jnp.dot(hb, w_ref[:, width:], preferred_element_type=F32).astype(BF16)


def _mem_kv(mem, gain, w_kv):
    batch, mlen, d = mem.shape
    width = w_kv.shape[1] // 2
    out = jax.ShapeDtypeStruct((batch, mlen, width), BF16)
    return pl.pallas_call(
        _mem_kv_kernel,
        grid=(batch,),
        in_specs=[
            pl.BlockSpec((1, mlen, d), lambda b: (b, 0, 0)),
            _resident((1, d)),
            _resident(w_kv.shape),
        ],
        out_specs=[pl.BlockSpec((1, mlen, width), lambda b: (b, 0, 0))] * 2,
        out_shape=[out, out],
        compiler_params=_params("parallel"),
        name="mem_kv",
    )(mem, gain.reshape(1, d), w_kv)


def _merge_kernel(x_ref, qx_ref, g0_ref, g1_ref, g2_ref, ym_ref, ya_ref, kx_ref, vx_ref,
                  bg_ref, wm_ref, wa_ref, wx_ref, wo_ref, o_ref):
    d = x_ref.shape[1]
    xw = qx_ref.shape[1]
    xhd = xw // X_HEADS
    heads = []
    for h in range(X_HEADS):
        cols = slice(h * xhd, (h + 1) * xhd)
        s = lax.dot_general(qx_ref[:, cols], kx_ref[0, :, cols], (((1,), (1,)), ((), ())),
                            preferred_element_type=F32) * (xhd ** -0.5)
        p = jnp.exp(s - jnp.max(s, axis=1, keepdims=True))
        pv = jnp.dot(p.astype(BF16), vx_ref[0, :, cols], preferred_element_type=F32)
        heads.append((pv / jnp.sum(p, axis=1, keepdims=True)).astype(BF16))
    yx = jnp.concatenate(heads, axis=1)

    def gate(g_ref, k):
        return _sigmoid(g_ref[...].astype(F32) + bg_ref[:, k * d:(k + 1) * d])

    merged = (gate(g0_ref, 0) * jnp.dot(ym_ref[...], wm_ref[...], preferred_element_type=F32)
              + gate(g1_ref, 1) * jnp.dot(ya_ref[...], wa_ref[...], preferred_element_type=F32)
              + gate(g2_ref, 2) * jnp.dot(yx, wx_ref[...], preferred_element_type=F32))
    o_ref[...] = x_ref[...] + jnp.dot(merged.astype(BF16), wo_ref[...], preferred_element_type=F32)


def _merge(x, proj, qx_col, g_col, y_m, y_a, kx, vx, b_gate, w_m, w_a, w_x, w_o, *, batch, seq):
    n, d = x.shape
    tm = min(MERGE_ROWS, seq)
    nt = seq // tm

    def row(b, t):
        return b * nt + t

    def proj_spec(col):
        return pl.BlockSpec((tm, d), lambda b, t: (row(b, t), col))

    def mem_spec(a):
        return pl.BlockSpec((1,) + a.shape[1:], lambda b, t: (b, 0, 0))

    return pl.pallas_call(
        _merge_kernel,
        grid=(batch, nt),
        in_specs=[
            pl.BlockSpec((tm, d), lambda b, t: (row(b, t), 0)),
            proj_spec(qx_col), proj_spec(g_col), proj_spec(g_col + 1), proj_spec(g_col + 2),
            pl.BlockSpec((tm, y_m.shape[1]), lambda b, t: (row(b, t), 0)),
            pl.BlockSpec((tm, y_a.shape[1]), lambda b, t: (row(b, t), 0)),
            mem_spec(kx), mem_spec(vx),
            _resident((1, b_gate.shape[0])),
            _resident(w_m.shape), _resident(w_a.shape), _resident(w_x.shape), _resident(w_o.shape),
        ],
        out_specs=pl.BlockSpec((tm, d), lambda b, t: (row(b, t), 0)),
        out_shape=jax.ShapeDtypeStruct((n, d), F32),
        compiler_params=_params("parallel", "parallel"),
        name="merge",
    )(x, proj, proj, proj, proj, y_m, y_a, kx, vx, b_gate.reshape(1, -1), w_m, w_a, w_x, w_o)


def kernel(x, mem, positions, rel_bias, ffn1_norm, ffn1_w_gu, ffn1_w_down, mix_norm, mem_norm, w_in, b_if, b_gate, m_conv_w, m_conv_b, m_wq, m_wk, m_wv, m_head_norm, a_lam_q1, a_lam_k1, a_lam_q2, a_lam_k2, a_head_norm, w_mem_kv, w_branch_m, w_branch_a, w_branch_x, w_out, ffn2_norm, ffn2_w_gu, ffn2_w_down, final_norm):
    del positions
    batch, seq, d = x.shape
    n = batch * seq
    l = LAYER
    m_inner = m_conv_w.shape[-1]
    a_width = a_head_norm.shape[-1]
    x_width = w_mem_kv.shape[-1] // 2
    bf = lambda a: a.astype(BF16)

    gate_lo = 2 * m_inner
    gate_hi = gate_lo + 2 * M_HEADS
    w_main = bf(jnp.concatenate([w_in[l][:, :gate_lo], w_in[l][:, gate_hi:]], axis=1))
    w_gate = bf(jnp.pad(w_in[l][:, gate_lo:gate_hi], ((0, 0), (0, 128 - 2 * M_HEADS))))

    x0 = x.reshape(n, d)
    x1 = _ffn(x0, ffn1_norm[l], bf(ffn1_w_gu[l]), bf(ffn1_w_down[l]), final_norm, final_norm=False)

    proj, gates = _inproj(x1, mix_norm[l], w_main, w_gate)

    q_m, k_m, v_m = _mlstm_qkv(proj, m_conv_w[l], m_conv_b[l], bf(m_wq[l]), bf(m_wk[l]), bf(m_wv[l]),
                               batch=batch, seq=seq)
    gates_t = gates[:, :2 * M_HEADS].reshape(batch, seq, 2, M_HEADS).transpose(0, 3, 2, 1)
    m_hd = m_inner // M_HEADS
    y_m = _mlstm_scan(q_m, k_m, v_m, gates_t, b_if[l], proj, m_inner // m_hd, m_head_norm[l],
                      batch=batch, seq=seq)

    a_hw = a_width // A_HEADS
    qa_col = 2 * m_inner // a_hw
    lam_vecs = jnp.stack([a_lam_q1[l], a_lam_k1[l], a_lam_q2[l], a_lam_k2[l]])
    y_a = _diff_attn(proj, qa_col, qa_col + A_HEADS, qa_col + 2 * A_HEADS, rel_bias, lam_vecs,
                     a_head_norm[l], batch=batch, seq=seq)

    kx, vx = _mem_kv(mem, mem_norm[l], bf(w_mem_kv[l]))
    qx_col = (2 * m_inner + 3 * a_width) // d
    x2 = _merge(x1, proj, qx_col, qx_col + x_width // d, y_m, y_a, kx, vx, b_gate[l],
                bf(w_branch_m[l]), bf(w_branch_a[l]), bf(w_branch_x[l]), bf(w_out[l]),
                batch=batch, seq=seq)

    out = _ffn(x2, ffn2_norm[l], bf(ffn2_w_gu[l]), bf(ffn2_w_down[l]), final_norm, final_norm=True)
    return out.reshape(batch, seq, d)
```

```python
import functools
import math

import jax
import jax.numpy as jnp
from jax import lax
from jax.experimental import pallas as pl
from jax.experimental.pallas import tpu as pltpu

EPS = 1e-6
BF16 = jnp.bfloat16
F32 = jnp.float32
NEG_BIG = -1e30

M_HEADS = 4
M_CONV = 4
A_HEADS = 8
X_HEADS = 4
N_BRANCH = 3
REL_BUCKETS = 32
REL_MAX_DIST = 128
LAYER = 0

VMEM_LIMIT_BYTES = 56 * 1024 * 1024

FFN_ROWS = 512
INPROJ_ROWS = 1024
INPROJ_COLS = 1024
QKV_ROWS = 512
MLSTM_CHUNK = 256
ATTN_TILE = 256
ATTN_HEAD_GROUP = 4
ATTN_STRIP = 128
ATTN_PV_COLS = 256
ATTN_ONES_ROWS = 16
ATTN_LOOKAHEAD = 8
LOG2_E = math.log2(math.e)
MERGE_ROWS = 512
CONV_HALO = 16


def _params(*sem, flags=None):
    return pltpu.CompilerParams(dimension_semantics=sem, vmem_limit_bytes=VMEM_LIMIT_BYTES, flags=flags)


def _resident(shape):
    zeros = (0,) * len(shape)
    return pl.BlockSpec(shape, lambda *_: zeros, pipeline_mode=pl.Buffered(1))


def _rms(x):
    return x * lax.rsqrt(jnp.mean(x * x, axis=-1, keepdims=True) + EPS)


def _sigmoid(x):
    return 1.0 / (1.0 + jnp.exp(-x))


def _ffn_kernel(x_ref, g_ref, wgu_ref, wd_ref, fg_ref, o_ref, *, d_ff, final_norm):
    x = x_ref[...]
    hb = (_rms(x) * g_ref[...]).astype(BF16)
    gate = jnp.dot(hb, wgu_ref[:, :d_ff], preferred_element_type=F32)
    up = jnp.dot(hb, wgu_ref[:, d_ff:], preferred_element_type=F32)
    act = (gate * _sigmoid(gate) * up).astype(BF16)
    y = x + 0.5 * jnp.dot(act, wd_ref[...], preferred_element_type=F32)
    if final_norm:
        y = _rms(y) * fg_ref[...]
    o_ref[...] = y


def _ffn(x, gain, w_gu, w_down, final_gain, *, final_norm):
    n, d = x.shape
    d_ff = w_down.shape[0]
    tm = min(FFN_ROWS, n)
    return pl.pallas_call(
        functools.partial(_ffn_kernel, d_ff=d_ff, final_norm=final_norm),
        grid=(n // tm,),
        in_specs=[
            pl.BlockSpec((tm, d), lambda i: (i, 0)),
            _resident((1, d)),
            _resident((d, 2 * d_ff)),
            _resident((d_ff, d)),
            _resident((1, d)),
        ],
        out_specs=pl.BlockSpec((tm, d), lambda i: (i, 0)),
        out_shape=jax.ShapeDtypeStruct((n, d), F32),
        compiler_params=_params("parallel"),
        name="ffn_final" if final_norm else "ffn",
    )(x, gain.reshape(1, d), w_gu, w_down, final_gain.reshape(1, d))


def _inproj_kernel(x_ref, g_ref, w_ref, wgate_ref, p_ref, gates_ref, h_scr):
    @pl.when(pl.program_id(1) == 0)
    def _():
        hb = (_rms(x_ref[...]) * g_ref[...]).astype(BF16)
        h_scr[...] = hb
        gates_ref[...] = jnp.dot(hb, wgate_ref[...], preferred_element_type=F32)

    p_ref[...] = jnp.dot(h_scr[...], w_ref[...], preferred_element_type=F32).astype(BF16)


def _inproj(x, gain, w_main, w_gate):
    n, d = x.shape
    width = w_main.shape[1]
    tm = min(INPROJ_ROWS, n)
    tn = INPROJ_COLS
    return pl.pallas_call(
        _inproj_kernel,
        grid=(n // tm, width // tn),
        in_specs=[
            pl.BlockSpec((tm, d), lambda i, j: (i, 0)),
            _resident((1, d)),
            pl.BlockSpec((d, tn), lambda i, j: (0, j)),
            _resident(w_gate.shape),
        ],
        out_specs=[
            pl.BlockSpec((tm, tn), lambda i, j: (i, j)),
            pl.BlockSpec((tm, w_gate.shape[1]), lambda i, j: (i, 0)),
        ],
        out_shape=[
            jax.ShapeDtypeStruct((n, width), BF16),
            jax.ShapeDtypeStruct((n, w_gate.shape[1]), F32),
        ],
        scratch_shapes=[pltpu.VMEM((tm, d), BF16)],
        compiler_params=_params("parallel", "arbitrary"),
        name="inproj",
    )(x, gain.reshape(1, d), w_main, w_gate)


def _mlstm_qkv_kernel(x_ref, halo_ref, cw_ref, cb_ref, wq_ref, wk_ref, wv_ref,
                      q_ref, k_ref, v_ref):
    ts = x_ref.shape[0]
    xb = x_ref[...]
    v_ref[...] = jnp.dot(xb, wv_ref[0], preferred_element_type=F32).astype(BF16)
    cur = xb.astype(F32)
    prev = jnp.where(pl.program_id(2) > 0, halo_ref[...].astype(F32), 0.0)
    full = jnp.concatenate([prev, cur], axis=0)
    cw = cw_ref[...]
    conv = cb_ref[...] + cw[M_CONV - 1:M_CONV, :] * cur
    for back in range(1, M_CONV):
        tap = M_CONV - 1 - back
        conv = conv + cw[tap:tap + 1, :] * full[CONV_HALO - back:CONV_HALO - back + ts, :]
    xc = (conv * _sigmoid(conv)).astype(BF16)
    q_ref[...] = jnp.dot(xc, wq_ref[0], preferred_element_type=F32).astype(BF16)
    k_ref[...] = jnp.dot(xc, wk_ref[0], preferred_element_type=F32).astype(BF16)


def _mlstm_qkv(proj, conv_w, conv_b, wq, wk, wv, *, batch, seq):
    heads, hd, _ = wq.shape
    n = batch * seq
    ts = min(QKV_ROWS, seq)
    nt = seq // ts
    halo_per_tile = ts // CONV_HALO

    def x_map(h, b, t):
        return (b * nt + t, h)

    def halo_map(h, b, t):
        return (jnp.maximum((b * nt + t) * halo_per_tile - 1, 0), h)

    def w_map(h, b, t):
        return (h, 0, 0)

    out = jax.ShapeDtypeStruct((n, heads * hd), BF16)
    return pl.pallas_call(
        _mlstm_qkv_kernel,
        grid=(heads, batch, nt),
        in_specs=[
            pl.BlockSpec((ts, hd), x_map),
            pl.BlockSpec((CONV_HALO, hd), halo_map),
            pl.BlockSpec((M_CONV, hd), lambda h, b, t: (0, h)),
            pl.BlockSpec((1, hd), lambda h, b, t: (0, h)),
            pl.BlockSpec((1, hd, hd), w_map),
            pl.BlockSpec((1, hd, hd), w_map),
            pl.BlockSpec((1, hd, hd), w_map),
        ],
        out_specs=[pl.BlockSpec((ts, hd), x_map)] * 3,
        out_shape=[out, out, out],
        compiler_params=_params("parallel", "parallel", "parallel"),
        name="mlstm_qkv",
    )(proj, proj, conv_w, conv_b.reshape(1, -1), wq, wk, wv)


def _mlstm_scan_kernel(q_ref, k_ref, v_ref, g_ref, bias_ref, o_ref, gain_ref, y_ref,
                       c_scr, n_scr, m_scr):
    @pl.when(pl.program_id(2) == 0)
    def _():
        c_scr[...] = jnp.zeros_like(c_scr)
        n_scr[...] = jnp.zeros_like(n_scr)
        m_scr[...] = jnp.zeros_like(m_scr)

    chunk = q_ref.shape[0]
    q = q_ref[...]
    k = k_ref[...]
    v = v_ref[...]
    gates = g_ref[0, 0] + bias_ref[0]
    ig = gates[0:1, :]
    fpre = gates[1:2, :]
    lf = jnp.minimum(fpre, 0.0) - jnp.log1p(jnp.exp(-jnp.abs(fpre)))

    t_idx = lax.broadcasted_iota(jnp.int32, (chunk, chunk), 0)
    s_idx = lax.broadcasted_iota(jnp.int32, (chunk, chunk), 1)
    causal = s_idx <= t_idx
    diag = s_idx == t_idx

    def to_col(row):
        return jnp.sum(jnp.where(diag, row, 0.0), axis=1, keepdims=True)

    def to_row(col):
        return jnp.sum(jnp.where(diag, col, 0.0), axis=0, keepdims=True)

    b_col = jnp.sum(jnp.where(causal, lf, 0.0), axis=1, keepdims=True)
    b_row = to_row(b_col)
    m_prev = m_scr[...]
    dlog = jnp.where(causal, b_col - b_row + ig, NEG_BIG)
    inter = b_col + m_prev
    m_t = jnp.maximum(inter, jnp.max(dlog, axis=1, keepdims=True))
    w_intra = jnp.exp(dlog - m_t)
    w_inter = jnp.exp(inter - m_t)

    s = lax.dot_general(q, k, (((1,), (1,)), ((), ())), preferred_element_type=F32) * w_intra
    c_prev = c_scr[...]
    n_prev = n_scr[...]
    num = (jnp.dot(s.astype(BF16), v, preferred_element_type=F32)
           + w_inter * jnp.dot(q, c_prev.astype(BF16), preferred_element_type=F32))
    qn = jnp.sum(q.astype(F32) * n_prev, axis=1, keepdims=True)
    den = jnp.sum(s, axis=1, keepdims=True) + w_inter * qn
    h = num / jnp.maximum(jnp.abs(den), jnp.exp(-m_t))

    b_last = b_col[chunk - 1:chunk, :]
    tail = b_last - b_row + ig
    m_new = jnp.maximum(b_last + m_prev, jnp.max(tail, axis=1, keepdims=True))
    decay = jnp.exp(b_last + m_prev - m_new)
    w_s = jnp.exp(to_col(tail) - m_new)
    kw = k.astype(F32) * w_s
    c_scr[...] = decay * c_prev + jnp.dot(kw.T.astype(BF16), v, preferred_element_type=F32)
    n_scr[...] = decay * n_prev + jnp.sum(kw, axis=0, keepdims=True)
    m_scr[...] = m_new

    gate = _sigmoid(o_ref[...].astype(F32))
    y_ref[...] = (gate * (_rms(h) * gain_ref[...])).astype(BF16)


def _mlstm_scan(q, k, v, gates_t, b_if, proj, o_col_block, head_gain, *, batch, seq):
    n, width = q.shape
    heads = gates_t.shape[1]
    hd = width // heads
    chunk = min(MLSTM_CHUNK, seq)
    nc = seq // chunk

    def qkv_map(b, h, c):
        return (b * nc + c, h)

    return pl.pallas_call(
        _mlstm_scan_kernel,
        grid=(batch, heads, nc),
        in_specs=[
            pl.BlockSpec((chunk, hd), qkv_map),
            pl.BlockSpec((chunk, hd), qkv_map),
            pl.BlockSpec((chunk, hd), qkv_map),
            pl.BlockSpec((1, 1, 2, chunk), lambda b, h, c: (b, h, 0, c)),
            pl.BlockSpec((1, 2, 1), lambda b, h, c: (h, 0, 0)),
            pl.BlockSpec((chunk, hd), lambda b, h, c: (b * nc + c, o_col_block + h)),
            pl.BlockSpec((1, hd), lambda b, h, c: (0, h)),
        ],
        out_specs=pl.BlockSpec((chunk, hd), qkv_map),
        out_shape=jax.ShapeDtypeStruct((n, width), BF16),
        scratch_shapes=[pltpu.VMEM((hd, hd), F32), pltpu.VMEM((1, hd), F32), pltpu.VMEM((1, 1), F32)],
        compiler_params=_params("parallel", "parallel", "arbitrary"),
        name="mlstm_scan",
    )(q, k, v, gates_t, b_if.reshape(2, heads, 1).transpose(1, 0, 2), proj, head_gain.reshape(1, -1))


def _diff_attn_kernel(far_ref, q_ref, k_ref, v_ref, band_ref, lam_ref, gain_ref, y_ref,
                      vt_scr, acc_scr, st_scr, *, out_scale, lam_init):
    group, n_tiles, rows, tile = vt_scr.shape
    width = q_ref.shape[1] // group
    head0 = pl.program_id(1) * group
    qi = pl.program_id(2)
    half = width // 2

    @pl.when(qi == 0)
    def _():
        ones = jnp.ones((rows - width, tile), BF16)
        for g in range(group):
            for j in range(n_tiles):
                v_tile = v_ref[j * tile:(j + 1) * tile, g * width:(g + 1) * width]
                vt_scr[g, j] = jnp.concatenate([v_tile.astype(F32).T.astype(BF16), ones], axis=0)

    q_all = q_ref[...]
    lane = lax.broadcasted_iota(jnp.int32, (tile, width), 1)
    qs = []
    for g in range(group):
        q = q_all[:, g * width:(g + 1) * width].astype(F32) * (half ** -0.5 * LOG2_E)
        stacked = jnp.concatenate([jnp.where(lane < half, q, 0.0), jnp.where(lane >= half, q, 0.0)], axis=0)
        qs.append(stacked.astype(BF16))

    acc_scr[...] = jnp.zeros_like(acc_scr)

    def scores(kj, g):
        start = pl.multiple_of(kj * tile, tile)
        return lax.dot_general(k_ref[pl.ds(start, tile), g * width:(g + 1) * width], qs[g],
                               (((1,), (1,)), ((), ())), preferred_element_type=F32)

    def block(tiles, maxes):
        units = [(kj, band_idx, g) for kj, band_idx in tiles for g in range(group)]
        all_scores = []
        for u, (kj, band_idx, g) in enumerate(units):
            if band_idx is None:
                all_scores.append(scores(kj, g))
            else:
                st_scr[u] = scores(kj, g)
                all_scores.append(None)
        maxes = list(maxes)
        for u, (st, (kj, band_idx, g)) in enumerate(zip(all_scores, units)):
            m_prev = maxes[g]
            if band_idx is None:
                shift = far_bias[g]
            else:
                shift = 0.0
                band = band_ref[g, band_idx]
                st = st_scr[u] + jnp.concatenate([band, band], axis=1)
            vt = vt_scr[g, kj]
            m_cols = []
            for c0 in range(0, 2 * tile, ATTN_PV_COLS):
                p_cols, a_cols = [], []
                for c in range(c0, c0 + ATTN_PV_COLS, ATTN_STRIP):
                    cols = slice(c, c + ATTN_STRIP)
                    s_c = st[:, cols]
                    m_c = jnp.maximum(m_prev[:, cols], jnp.max(s_c, axis=0, keepdims=True) + shift)
                    a_cols.append(jnp.exp2(m_prev[:, cols] - m_c))
                    p_cols.append(jnp.exp2(s_c - (m_c - shift)).astype(BF16))
                    m_cols.append(m_c)
                pv = jnp.dot(vt, jnp.concatenate(p_cols, axis=1), preferred_element_type=F32)
                wide = slice(c0, c0 + ATTN_PV_COLS)
                acc_scr[g, :, wide] = jnp.concatenate(a_cols, axis=1) * acc_scr[g, :, wide] + pv
            maxes[g] = jnp.concatenate(m_cols, axis=1)
        return tuple(maxes)

    def far(first, count):
        return tuple((first + i, None) for i in range(count))

    far_bias = tuple(far_ref[head0 + g] * LOG2_E for g in range(group))
    n_far = jnp.maximum(qi - 1, 0)
    n_quads = n_far // 4
    rest = n_far - 4 * n_quads
    maxes = tuple(jnp.full((1, 2 * tile), NEG_BIG, F32) for _ in range(group))
    maxes = lax.fori_loop(0, n_quads, lambda i, c: block(far(4 * i, 4), c), maxes)
    maxes = lax.cond(rest >= 2, lambda: block(far(4 * n_quads, 2), maxes), lambda: maxes)
    maxes = lax.cond(rest % 2 == 1, lambda: block(far(n_far - 1, 1), maxes), lambda: maxes)
    lax.cond(qi > 0, lambda: block(((qi - 1, 0), (qi, 1)), maxes), lambda: block(((qi, 1),), maxes))


    lam_v = lam_ref[...]
    lam = (jnp.exp(jnp.sum(lam_v[0:1, :] * lam_v[1:2, :], axis=1, keepdims=True))
           - jnp.exp(jnp.sum(lam_v[2:3, :] * lam_v[3:4, :], axis=1, keepdims=True)) + lam_init)
    for g in range(group):
        o = acc_scr[g, :width, :] / acc_scr[g, width:width + 1, :]
        out = (o[:, :tile] - lam * o[:, tile:]).T
        gain = gain_ref[:, g * width:(g + 1) * width]
        y_ref[:, g * width:(g + 1) * width] = (_rms(out) * gain * out_scale).astype(BF16)


def _t5_bucket(n):
    max_exact = REL_BUCKETS // 2
    nf = jnp.maximum(n, 1).astype(F32)
    large = max_exact + (jnp.log(nf / max_exact) / math.log(REL_MAX_DIST / max_exact)
                         * (REL_BUCKETS - max_exact)).astype(jnp.int32)
    large = jnp.minimum(large, REL_BUCKETS - 1)
    return jnp.where(n < max_exact, n, large)


def _diff_attn(proj, q_col, k_col, v_col, rel_bias, lam_vecs, head_gain, *, batch, seq):
    n = proj.shape[0]
    heads = rel_bias.shape[1]
    hw = head_gain.shape[0] // heads
    tile = min(ATTN_TILE, seq)
    nq = seq // tile
    assert tile >= REL_MAX_DIST, "distances beyond one tile must all fall in the last bucket"
    group = ATTN_HEAD_GROUP
    gw = group * hw
    assert heads % group == 0 and q_col % group == 0 and k_col % group == 0 and v_col % group == 0

    kk = jnp.arange(tile, dtype=jnp.int32)[:, None]
    qq = jnp.arange(tile, dtype=jnp.int32)[None, :]
    dist = jnp.stack([tile + qq - kk, qq - kk])
    bucket = _t5_bucket(jnp.maximum(dist, 0))
    band = jnp.zeros((heads,) + dist.shape, F32)
    for b in range(REL_BUCKETS):
        band = jnp.where(bucket[None] == b, rel_bias[b].astype(F32)[:, None, None, None], band)
    band = jnp.where(dist[None] >= 0, band * LOG2_E, NEG_BIG)
    far = rel_bias[REL_BUCKETS - 1].astype(F32)

    lam_init = 0.8 - 0.6 * math.exp(-0.3 * LAYER)
    return pl.pallas_call(
        functools.partial(_diff_attn_kernel, out_scale=1.0 - lam_init, lam_init=lam_init),
        grid=(batch, heads // group, nq),
        in_specs=[
            pl.BlockSpec(memory_space=pltpu.SMEM),
            pl.BlockSpec((tile, gw), lambda b, h, i: (b * nq + i, q_col // group + h)),
            pl.BlockSpec((seq, gw), lambda b, h, i: (b, k_col // group + h)),
            pl.BlockSpec((seq, gw), lambda b, h, i: (b, v_col // group + h)),
            pl.BlockSpec((group, 2, tile, tile), lambda b, h, i: (h, 0, 0, 0)),
            _resident(lam_vecs.shape),
            pl.BlockSpec((1, gw), lambda b, h, i: (0, h)),
        ],
        out_specs=pl.BlockSpec((tile, gw), lambda b, h, i: (b * nq + i, h)),
        out_shape=jax.ShapeDtypeStruct((n, heads * hw), BF16),
        scratch_shapes=[
            pltpu.VMEM((group, nq, hw + ATTN_ONES_ROWS, tile), BF16),
            pltpu.VMEM((group, hw + ATTN_ONES_ROWS, 2 * tile), F32),
            pltpu.VMEM((2 * group, tile, 2 * tile), F32),
        ],
        compiler_params=_params("parallel", "parallel", "arbitrary"),
        name="diff_attn",
    )(far, proj, proj, proj, band, lam_vecs, head_gain.reshape(1, -1))


def _mem_kv_kernel(mem_ref, g_ref, w_ref, k_ref, v_ref):
    width = k_ref.shape[-1]
    hb = (_rms(mem_ref[0]) * g_ref[...]).astype(BF16)
    k_ref[0] = jnp.dot(hb, w_ref[:, :width], preferred_element_type=F32).astype(BF16)
    v_ref[0] = jnp.dot(hb, w_ref[:, width:], preferred_element_type=F32).astype(BF16)


def _mem_kv(mem, gain, w_kv):
    batch, mlen, d = mem.shape
    width = w_kv.shape[1] // 2
    out = jax.ShapeDtypeStruct((batch, mlen, width), BF16)
    return pl.pallas_call(
        _mem_kv_kernel,
        grid=(batch,),
        in_specs=[
            pl.BlockSpec((1, mlen, d), lambda b: (b, 0, 0)),
            _resident((1, d)),
            _resident(w_kv.shape),
        ],
        out_specs=[pl.BlockSpec((1, mlen, width), lambda b: (b, 0, 0))] * 2,
        out_shape=[out, out],
        compiler_params=_params("parallel"),
        name="mem_kv",
    )(mem, gain.reshape(1, d), w_kv)


def _merge_kernel(x_ref, qx_ref, g0_ref, g1_ref, g2_ref, ym_ref, ya_ref, kx_ref, vx_ref,
                  bg_ref, wm_ref, wa_ref, wx_ref, wo_ref, o_ref):
    d = x_ref.shape[1]
    xw = qx_ref.shape[1]
    xhd = xw // X_HEADS
    heads = []
    for h in range(X_HEADS):
        cols = slice(h * xhd, (h + 1) * xhd)
        s = lax.dot_general(qx_ref[:, cols], kx_ref[0, :, cols], (((1,), (1,)), ((), ())),
                            preferred_element_type=F32) * (xhd ** -0.5)
        p = jnp.exp(s - jnp.max(s, axis=1, keepdims=True))
        pv = jnp.dot(p.astype(BF16), vx_ref[0, :, cols], preferred_element_type=F32)
        heads.append((pv / jnp.sum(p, axis=1, keepdims=True)).astype(BF16))
    yx = jnp.concatenate(heads, axis=1)

    def gate(g_ref, k):
        return _sigmoid(g_ref[...].astype(F32) + bg_ref[:, k * d:(k + 1) * d])

    merged = (gate(g0_ref, 0) * jnp.dot(ym_ref[...], wm_ref[...], preferred_element_type=F32)
              + gate(g1_ref, 1) * jnp.dot(ya_ref[...], wa_ref[...], preferred_element_type=F32)
              + gate(g2_ref, 2) * jnp.dot(yx, wx_ref[...], preferred_element_type=F32))
    o_ref[...] = x_ref[...] + jnp.dot(merged.astype(BF16), wo_ref[...], preferred_element_type=F32)


def _merge(x, proj, qx_col, g_col, y_m, y_a, kx, vx, b_gate, w_m, w_a, w_x, w_o, *, batch, seq):
    n, d = x.shape
    tm = min(MERGE_ROWS, seq)
    nt = seq // tm

    def row(b, t):
        return b * nt + t

    def proj_spec(col):
        return pl.BlockSpec((tm, d), lambda b, t: (row(b, t), col))

    def mem_spec(a):
        return pl.BlockSpec((1,) + a.shape[1:], lambda b, t: (b, 0, 0))

    return pl.pallas_call(
        _merge_kernel,
        grid=(batch, nt),
        in_specs=[
            pl.BlockSpec((tm, d), lambda b, t: (row(b, t), 0)),
            proj_spec(qx_col), proj_spec(g_col), proj_spec(g_col + 1), proj_spec(g_col + 2),
            pl.BlockSpec((tm, y_m.shape[1]), lambda b, t: (row(b, t), 0)),
            pl.BlockSpec((tm, y_a.shape[1]), lambda b, t: (row(b, t), 0)),
            mem_spec(kx), mem_spec(vx),
            _resident((1, b_gate.shape[0])),
            _resident(w_m.shape), _resident(w_a.shape), _resident(w_x.shape), _resident(w_o.shape),
        ],
        out_specs=pl.BlockSpec((tm, d), lambda b, t: (row(b, t), 0)),
        out_shape=jax.ShapeDtypeStruct((n, d), F32),
        compiler_params=_params("parallel", "parallel"),
        name="merge",
    )(x, proj, proj, proj, proj, y_m, y_a, kx, vx, b_gate.reshape(1, -1), w_m, w_a, w_x, w_o)


def kernel(x, mem, positions, rel_bias, ffn1_norm, ffn1_w_gu, ffn1_w_down, mix_norm, mem_norm, w_in, b_if, b_gate, m_conv_w, m_conv_b, m_wq, m_wk, m_wv, m_head_norm, a_lam_q1, a_lam_k1, a_lam_q2, a_lam_k2, a_head_norm, w_mem_kv, w_branch_m, w_branch_a, w_branch_x, w_out, ffn2_norm, ffn2_w_gu, ffn2_w_down, final_norm):
    del positions
    batch, seq, d = x.shape
    n = batch * seq
    l = LAYER
    m_inner = m_conv_w.shape[-1]
    a_width = a_head_norm.shape[-1]
    x_width = w_mem_kv.shape[-1] // 2
    bf = lambda a: a.astype(BF16)

    gate_lo = 2 * m_inner
    gate_hi = gate_lo + 2 * M_HEADS
    w_main = bf(jnp.concatenate([w_in[l][:, :gate_lo], w_in[l][:, gate_hi:]], axis=1))
    w_gate = bf(jnp.pad(w_in[l][:, gate_lo:gate_hi], ((0, 0), (0, 128 - 2 * M_HEADS))))

    x0 = x.reshape(n, d)
    x1 = _ffn(x0, ffn1_norm[l], bf(ffn1_w_gu[l]), bf(ffn1_w_down[l]), final_norm, final_norm=False)

    proj, gates = _inproj(x1, mix_norm[l], w_main, w_gate)

    m_hd = m_inner // M_HEADS
    q_m, k_m, v_m = _mlstm_qkv(proj, m_conv_w[l], m_conv_b[l], bf(m_wq[l]), bf(m_wk[l] * m_hd ** -0.5),
                               bf(m_wv[l]), batch=batch, seq=seq)
    gates_t = gates[:, :2 * M_HEADS].reshape(batch, seq, 2, M_HEADS).transpose(0, 3, 2, 1)
    y_m = _mlstm_scan(q_m, k_m, v_m, gates_t, b_if[l], proj, m_inner // m_hd, m_head_norm[l],
                      batch=batch, seq=seq)

    a_hw = a_width // A_HEADS
    qa_col = 2 * m_inner // a_hw
    lam_vecs = jnp.stack([a_lam_q1[l], a_lam_k1[l], a_lam_q2[l], a_lam_k2[l]])
    y_a = _diff_attn(proj, qa_col, qa_col + A_HEADS, qa_col + 2 * A_HEADS, rel_bias, lam_vecs,
                     a_head_norm[l], batch=batch, seq=seq)

    kx, vx = _mem_kv(mem, mem_norm[l], bf(w_mem_kv[l]))
    qx_col = (2 * m_inner + 3 * a_width) // d
    x2 = _merge(x1, proj, qx_col, qx_col + x_width // d, y_m, y_a, kx, vx, b_gate[l],
                bf(w_branch_m[l]), bf(w_branch_a[l]), bf(w_branch_x[l]), bf(w_out[l]),
                batch=batch, seq=seq)

    out = _ffn(x2, ffn2_norm[l], bf(ffn2_w_gu[l]), bf(ffn2_w_down[l]), final_norm, final_norm=True)
    return out.reshape(batch, seq, d)
```

```python
import functools
import math

import jax
import jax.numpy as jnp
from jax import lax
from jax.experimental import pallas as pl
from jax.experimental.pallas import tpu as pltpu

EPS = 1e-6
BF16 = jnp.bfloat16
F32 = jnp.float32
NEG_BIG = -1e30

M_HEADS = 4
M_CONV = 4
A_HEADS = 8
X_HEADS = 4
N_BRANCH = 3
REL_BUCKETS = 32
REL_MAX_DIST = 128
LAYER = 0

VMEM_LIMIT_BYTES = 56 * 1024 * 1024

FFN_ROWS = 512
INPROJ_ROWS = 1024
INPROJ_COLS = 1024
QKV_ROWS = 512
QKV_SUBTILES = 4
MLSTM_CHUNK = 256
ATTN_TILE = 256
ATTN_HEAD_GROUP = 4
ATTN_STRIP = 128
ATTN_PV_COLS = 256
ATTN_ONES_ROWS = 16
ATTN_LOOKAHEAD = 8
LOG2_E = math.log2(math.e)
MERGE_ROWS = 512
CONV_HALO = 16


def _params(*sem, flags=None):
    return pltpu.CompilerParams(dimension_semantics=sem, vmem_limit_bytes=VMEM_LIMIT_BYTES, flags=flags)


def _resident(shape):
    zeros = (0,) * len(shape)
    return pl.BlockSpec(shape, lambda *_: zeros, pipeline_mode=pl.Buffered(1))


def _rms(x):
    return x * lax.rsqrt(jnp.mean(x * x, axis=-1, keepdims=True) + EPS)


def _sigmoid(x):
    return 1.0 / (1.0 + jnp.exp(-x))


def _ffn_kernel(x_ref, g_ref, wgu_ref, wd_ref, fg_ref, o_ref, *, d_ff, final_norm):
    x = x_ref[...]
    hb = (_rms(x) * g_ref[...]).astype(BF16)
    gate = jnp.dot(hb, wgu_ref[:, :d_ff], preferred_element_type=F32)
    up = jnp.dot(hb, wgu_ref[:, d_ff:], preferred_element_type=F32)
    act = (gate * _sigmoid(gate) * up).astype(BF16)
    y = x + 0.5 * jnp.dot(act, wd_ref[...], preferred_element_type=F32)
    if final_norm:
        y = _rms(y) * fg_ref[...]
    o_ref[...] = y


def _ffn(x, gain, w_gu, w_down, final_gain, *, final_norm):
    n, d = x.shape
    d_ff = w_down.shape[0]
    tm = min(FFN_ROWS, n)
    return pl.pallas_call(
        functools.partial(_ffn_kernel, d_ff=d_ff, final_norm=final_norm),
        grid=(n // tm,),
        in_specs=[
            pl.BlockSpec((tm, d), lambda i: (i, 0)),
            _resident((1, d)),
            _resident((d, 2 * d_ff)),
            _resident((d_ff, d)),
            _resident((1, d)),
        ],
        out_specs=pl.BlockSpec((tm, d), lambda i: (i, 0)),
        out_shape=jax.ShapeDtypeStruct((n, d), F32),
        compiler_params=_params("parallel"),
        name="ffn_final" if final_norm else "ffn",
    )(x, gain.reshape(1, d), w_gu, w_down, final_gain.reshape(1, d))


def _inproj_kernel(x_ref, g_ref, w_head_ref, w_tail_ref, wgate_ref, p_ref, gates_ref, h_scr, *, head_blocks):
    j = pl.program_id(1)

    @pl.when(j == 0)
    def _():
        hb = (_rms(x_ref[...]) * g_ref[...]).astype(BF16)
        h_scr[...] = hb
        gates_ref[...] = jnp.dot(hb, wgate_ref[...], preferred_element_type=F32)

    @pl.when(j < head_blocks)
    def _():
        p_ref[...] = jnp.dot(h_scr[...], w_head_ref[...], preferred_element_type=F32).astype(BF16)

    @pl.when(j >= head_blocks)
    def _():
        p_ref[...] = jnp.dot(h_scr[...], w_tail_ref[...], preferred_element_type=F32).astype(BF16)


def _inproj(x, gain, w_all, head_width, w_tail, w_gate):
    n, d = x.shape
    tm = min(INPROJ_ROWS, n)
    tn = INPROJ_COLS
    head_blocks = head_width // tn
    width = head_width + w_tail.shape[1]
    return pl.pallas_call(
        functools.partial(_inproj_kernel, head_blocks=head_blocks),
        grid=(n // tm, width // tn),
        in_specs=[
            pl.BlockSpec((tm, d), lambda i, j: (i, 0)),
            _resident((1, d)),
            pl.BlockSpec((d, tn), lambda i, j: (0, jnp.minimum(j, head_blocks - 1))),
            pl.BlockSpec((d, tn), lambda i, j: (0, jnp.maximum(j - head_blocks, 0))),
            _resident(w_gate.shape),
        ],
        out_specs=[
            pl.BlockSpec((tm, tn), lambda i, j: (i, j)),
            pl.BlockSpec((tm, w_gate.shape[1]), lambda i, j: (i, 0)),
        ],
        out_shape=[
            jax.ShapeDtypeStruct((n, width), BF16),
            jax.ShapeDtypeStruct((n, w_gate.shape[1]), F32),
        ],
        scratch_shapes=[pltpu.VMEM((tm, d), BF16)],
        compiler_params=_params("parallel", "arbitrary"),
        name="inproj",
    )(x, gain.reshape(1, d), w_all, w_tail, w_gate)


def _mlstm_qkv_kernel(x_ref, halo_ref, cw_ref, cb_ref, wq_ref, wk_ref, wv_ref,
                      q_ref, k_ref, v_ref):
    ts = x_ref.shape[0]
    xb = x_ref[...]
    v_ref[...] = jnp.dot(xb, wv_ref[0], preferred_element_type=F32).astype(BF16)
    prev = jnp.where(pl.program_id(2) > 0, halo_ref[...].astype(F32), 0.0)
    full = jnp.concatenate([prev, xb.astype(F32)], axis=0)
    cw = cw_ref[...]
    sub = ts // QKV_SUBTILES
    for r0 in range(0, ts, sub):
        conv = cb_ref[...]
        for back in range(M_CONV):
            tap = M_CONV - 1 - back
            start = CONV_HALO + r0 - back
            conv = conv + cw[tap:tap + 1, :] * full[start:start + sub, :]
        xc = (conv * _sigmoid(conv)).astype(BF16)
        q_ref[r0:r0 + sub, :] = jnp.dot(xc, wq_ref[0], preferred_element_type=F32).astype(BF16)
        k_ref[r0:r0 + sub, :] = jnp.dot(xc, wk_ref[0], preferred_element_type=F32).astype(BF16)


def _mlstm_qkv(proj, conv_w, conv_b, wq, wk, wv, *, batch, seq):
    heads, hd, _ = wq.shape
    n = batch * seq
    ts = min(QKV_ROWS, seq)
    nt = seq // ts
    halo_per_tile = ts // CONV_HALO

    def x_map(h, b, t):
        return (b * nt + t, h)

    def halo_map(h, b, t):
        return (jnp.maximum((b * nt + t) * halo_per_tile - 1, 0), h)

    def w_map(h, b, t):
        return (h, 0, 0)

    out = jax.ShapeDtypeStruct((n, heads * hd), BF16)
    return pl.pallas_call(
        _mlstm_qkv_kernel,
        grid=(heads, batch, nt),
        in_specs=[
            pl.BlockSpec((ts, hd), x_map),
            pl.BlockSpec((CONV_HALO, hd), halo_map),
            pl.BlockSpec((M_CONV, hd), lambda h, b, t: (0, h)),
            pl.BlockSpec((1, hd), lambda h, b, t: (0, h)),
            pl.BlockSpec((1, hd, hd), w_map),
            pl.BlockSpec((1, hd, hd), w_map),
            pl.BlockSpec((1, hd, hd), w_map),
        ],
        out_specs=[pl.BlockSpec((ts, hd), x_map)] * 3,
        out_shape=[out, out, out],
        compiler_params=_params("parallel", "parallel", "parallel"),
        name="mlstm_qkv",
    )(proj, proj, conv_w, conv_b.reshape(1, -1), wq, wk, wv)


def _mlstm_scan_kernel(q_ref, k_ref, v_ref, g_ref, bias_ref, o_ref, gain_ref, y_ref,
                       c_scr, n_scr, m_scr):
    @pl.when(pl.program_id(2) == 0)
    def _():
        c_scr[...] = jnp.zeros_like(c_scr)
        n_scr[...] = jnp.zeros_like(n_scr)
        m_scr[...] = jnp.zeros_like(m_scr)

    chunk = q_ref.shape[0]
    q = q_ref[...]
    k = k_ref[...]
    v = v_ref[...]
    gates = g_ref[0, 0] + bias_ref[0]
    ig = gates[0:1, :]
    fpre = gates[1:2, :]
    lf = jnp.minimum(fpre, 0.0) - jnp.log1p(jnp.exp(-jnp.abs(fpre)))

    t_idx = lax.broadcasted_iota(jnp.int32, (chunk, chunk), 0)
    s_idx = lax.broadcasted_iota(jnp.int32, (chunk, chunk), 1)
    causal = s_idx <= t_idx
    diag = s_idx == t_idx

    def to_col(row):
        return jnp.sum(jnp.where(diag, row, 0.0), axis=1, keepdims=True)

    def to_row(col):
        return jnp.sum(jnp.where(diag, col, 0.0), axis=0, keepdims=True)

    b_col = jnp.sum(jnp.where(causal, lf, 0.0), axis=1, keepdims=True)
    b_row = to_row(b_col)
    m_prev = m_scr[...]
    dlog = jnp.where(causal, b_col - b_row + ig, NEG_BIG)
    inter = b_col + m_prev
    m_t = jnp.maximum(inter, jnp.max(dlog, axis=1, keepdims=True))
    w_intra = jnp.exp(dlog - m_t)
    w_inter = jnp.exp(inter - m_t)

    s = lax.dot_general(q, k, (((1,), (1,)), ((), ())), preferred_element_type=F32) * w_intra
    c_prev = c_scr[...]
    n_prev = n_scr[...]
    num = (jnp.dot(s.astype(BF16), v, preferred_element_type=F32)
           + w_inter * jnp.dot(q, c_prev.astype(BF16), preferred_element_type=F32))
    qn = jnp.sum(q.astype(F32) * n_prev, axis=1, keepdims=True)
    den = jnp.sum(s, axis=1, keepdims=True) + w_inter * qn
    h = num / jnp.maximum(jnp.abs(den), jnp.exp(-m_t))

    b_last = b_col[chunk - 1:chunk, :]
    tail = b_last - b_row + ig
    m_new = jnp.maximum(b_last + m_prev, jnp.max(tail, axis=1, keepdims=True))
    decay = jnp.exp(b_last + m_prev - m_new)
    w_s = jnp.exp(to_col(tail) - m_new)
    kw = k.astype(F32) * w_s
    c_scr[...] = decay * c_prev + jnp.dot(kw.T.astype(BF16), v, preferred_element_type=F32)
    n_scr[...] = decay * n_prev + jnp.sum(kw, axis=0, keepdims=True)
    m_scr[...] = m_new

    gate = _sigmoid(o_ref[...].astype(F32))
    y_ref[...] = (gate * (_rms(h) * gain_ref[...])).astype(BF16)


def _mlstm_scan(q, k, v, gates_t, b_if, proj, o_col_block, head_gain, *, batch, seq):
    n, width = q.shape
    heads = gates_t.shape[1]
    hd = width // heads
    chunk = min(MLSTM_CHUNK, seq)
    nc = seq // chunk

    def qkv_map(b, h, c):
        return (b * nc + c, h)

    return pl.pallas_call(
        _mlstm_scan_kernel,
        grid=(batch, heads, nc),
        in_specs=[
            pl.BlockSpec((chunk, hd), qkv_map),
            pl.BlockSpec((chunk, hd), qkv_map),
            pl.BlockSpec((chunk, hd), qkv_map),
            pl.BlockSpec((1, 1, 2, chunk), lambda b, h, c: (b, h, 0, c)),
            pl.BlockSpec((1, 2, 1), lambda b, h, c: (h, 0, 0)),
            pl.BlockSpec((chunk, hd), lambda b, h, c: (b * nc + c, o_col_block + h)),
            pl.BlockSpec((1, hd), lambda b, h, c: (0, h)),
        ],
        out_specs=pl.BlockSpec((chunk, hd), qkv_map),
        out_shape=jax.ShapeDtypeStruct((n, width), BF16),
        scratch_shapes=[pltpu.VMEM((hd, hd), F32), pltpu.VMEM((1, hd), F32), pltpu.VMEM((1, 1), F32)],
        compiler_params=_params("parallel", "parallel", "arbitrary"),
        name="mlstm_scan",
    )(q, k, v, gates_t, b_if.reshape(2, heads, 1).transpose(1, 0, 2), proj, head_gain.reshape(1, -1))


def _diff_attn_kernel(far_ref, q_ref, k_ref, v_ref, band_ref, lam_ref, gain_ref, y_ref,
                      vt_scr, acc_scr, st_scr, *, out_scale, lam_init):
    group, n_tiles, rows, tile = vt_scr.shape
    width = q_ref.shape[1] // group
    head0 = pl.program_id(1) * group
    qi = pl.program_id(2)
    half = width // 2

    @pl.when(qi == 0)
    def _():
        ones = jnp.ones((rows - width, tile), BF16)
        for g in range(group):
            for j in range(n_tiles):
                v_tile = v_ref[j * tile:(j + 1) * tile, g * width:(g + 1) * width]
                vt_scr[g, j] = jnp.concatenate([v_tile.astype(F32).T.astype(BF16), ones], axis=0)

    q_all = q_ref[...]
    lane = lax.broadcasted_iota(jnp.int32, (tile, width), 1)
    qs = []
    for g in range(group):
        q = q_all[:, g * width:(g + 1) * width].astype(F32) * (half ** -0.5 * LOG2_E)
        stacked = jnp.concatenate([jnp.where(lane < half, q, 0.0), jnp.where(lane >= half, q, 0.0)], axis=0)
        qs.append(stacked.astype(BF16))

    acc_scr[...] = jnp.zeros_like(acc_scr)

    def scores(kj, g):
        start = pl.multiple_of(kj * tile, tile)
        return lax.dot_general(k_ref[pl.ds(start, tile), g * width:(g + 1) * width], qs[g],
                               (((1,), (1,)), ((), ())), preferred_element_type=F32)

    def block(tiles, maxes):
        units = [(kj, band_idx, g) for kj, band_idx in tiles for g in range(group)]
        all_scores = []
        for u, (kj, band_idx, g) in enumerate(units):
            if band_idx is None:
                all_scores.append(scores(kj, g))
            else:
                st_scr[u] = scores(kj, g)
                all_scores.append(None)
        maxes = list(maxes)
        for u, (st, (kj, band_idx, g)) in enumerate(zip(all_scores, units)):
            m_prev = maxes[g]
            if band_idx is None:
                shift = far_bias[g]
            else:
                shift = 0.0
                band = band_ref[g, band_idx]
                st = st_scr[u] + jnp.concatenate([band, band], axis=1)
            vt = vt_scr[g, kj]
            m_cols = []
            for c0 in range(0, 2 * tile, ATTN_PV_COLS):
                p_cols, a_cols = [], []
                for c in range(c0, c0 + ATTN_PV_COLS, ATTN_STRIP):
                    cols = slice(c, c + ATTN_STRIP)
                    s_c = st[:, cols]
                    m_c = jnp.maximum(m_prev[:, cols], jnp.max(s_c, axis=0, keepdims=True) + shift)
                    a_cols.append(jnp.exp2(m_prev[:, cols] - m_c))
                    p_cols.append(jnp.exp2(s_c - (m_c - shift)).astype(BF16))
                    m_cols.append(m_c)
                pv = jnp.dot(vt, jnp.concatenate(p_cols, axis=1), preferred_element_type=F32)
                wide = slice(c0, c0 + ATTN_PV_COLS)
                acc_scr[g, :, wide] = jnp.concatenate(a_cols, axis=1) * acc_scr[g, :, wide] + pv
            maxes[g] = jnp.concatenate(m_cols, axis=1)
        return tuple(maxes)

    def far(first, count):
        return tuple((first + i, None) for i in range(count))

    far_bias = tuple(far_ref[head0 + g] * LOG2_E for g in range(group))
    n_far = jnp.maximum(qi - 1, 0)
    n_quads = n_far // 4
    rest = n_far - 4 * n_quads
    maxes = tuple(jnp.full((1, 2 * tile), NEG_BIG, F32) for _ in range(group))
    maxes = lax.fori_loop(0, n_quads, lambda i, c: block(far(4 * i, 4), c), maxes)
    maxes = lax.cond(rest >= 2, lambda: block(far(4 * n_quads, 2), maxes), lambda: maxes)
    maxes = lax.cond(rest % 2 == 1, lambda: block(far(n_far - 1, 1), maxes), lambda: maxes)
    lax.cond(qi > 0, lambda: block(((qi - 1, 0), (qi, 1)), maxes), lambda: block(((qi, 1),), maxes))


    lam_v = lam_ref[...]
    lam = (jnp.exp(jnp.sum(lam_v[0:1, :] * lam_v[1:2, :], axis=1, keepdims=True))
           - jnp.exp(jnp.sum(lam_v[2:3, :] * lam_v[3:4, :], axis=1, keepdims=True)) + lam_init)
    for g in range(group):
        o = acc_scr[g, :width, :] / acc_scr[g, width:width + 1, :]
        out = (o[:, :tile] - lam * o[:, tile:]).T
        gain = gain_ref[:, g * width:(g + 1) * width]
        y_ref[:, g * width:(g + 1) * width] = (_rms(out) * gain * out_scale).astype(BF16)


def _t5_bucket(n):
    max_exact = REL_BUCKETS // 2
    nf = jnp.maximum(n, 1).astype(F32)
    large = max_exact + (jnp.log(nf / max_exact) / math.log(REL_MAX_DIST / max_exact)
                         * (REL_BUCKETS - max_exact)).astype(jnp.int32)
    large = jnp.minimum(large, REL_BUCKETS - 1)
    return jnp.where(n < max_exact, n, large)


def _diff_attn(proj, q_col, k_col, v_col, rel_bias, lam_vecs, head_gain, *, batch, seq):
    n = proj.shape[0]
    heads = rel_bias.shape[1]
    hw = head_gain.shape[0] // heads
    tile = min(ATTN_TILE, seq)
    nq = seq // tile
    assert tile >= REL_MAX_DIST, "distances beyond one tile must all fall in the last bucket"
    group = ATTN_HEAD_GROUP
    gw = group * hw
    assert heads % group == 0 and q_col % group == 0 and k_col % group == 0 and v_col % group == 0

    kk = jnp.arange(tile, dtype=jnp.int32)[:, None]
    qq = jnp.arange(tile, dtype=jnp.int32)[None, :]
    dist = jnp.stack([tile + qq - kk, qq - kk])
    bucket = _t5_bucket(jnp.maximum(dist, 0))
    band = jnp.zeros((heads,) + dist.shape, F32)
    for b in range(REL_BUCKETS):
        band = jnp.where(bucket[None] == b, rel_bias[b].astype(F32)[:, None, None, None], band)
    band = jnp.where(dist[None] >= 0, band * LOG2_E, NEG_BIG)
    far = rel_bias[REL_BUCKETS - 1].astype(F32)

    lam_init = 0.8 - 0.6 * math.exp(-0.3 * LAYER)
    return pl.pallas_call(
        functools.partial(_diff_attn_kernel, out_scale=1.0 - lam_init, lam_init=lam_init),
        grid=(batch, heads // group, nq),
        in_specs=[
            pl.BlockSpec(memory_space=pltpu.SMEM),
            pl.BlockSpec((tile, gw), lambda b, h, i: (b * nq + i, q_col // group + h)),
            pl.BlockSpec((seq, gw), lambda b, h, i: (b, k_col // group + h)),
            pl.BlockSpec((seq, gw), lambda b, h, i: (b, v_col // group + h)),
            pl.BlockSpec((group, 2, tile, tile), lambda b, h, i: (h, 0, 0, 0)),
            _resident(lam_vecs.shape),
            pl.BlockSpec((1, gw), lambda b, h, i: (0, h)),
        ],
        out_specs=pl.BlockSpec((tile, gw), lambda b, h, i: (b * nq + i, h)),
        out_shape=jax.ShapeDtypeStruct((n, heads * hw), BF16),
        scratch_shapes=[
            pltpu.VMEM((group, nq, hw + ATTN_ONES_ROWS, tile), BF16),
            pltpu.VMEM((group, hw + ATTN_ONES_ROWS, 2 * tile), F32),
            pltpu.VMEM((2 * group, tile, 2 * tile), F32),
        ],
        compiler_params=_params("parallel", "parallel", "arbitrary"),
        name="diff_attn",
    )(far, proj, proj, proj, band, lam_vecs, head_gain.reshape(1, -1))


def _mem_kv_kernel(mem_ref, g_ref, w_ref, k_ref, v_ref):
    width = k_ref.shape[-1]
    hb = (_rms(mem_ref[0]) * g_ref[...]).astype(BF16)
    k_ref[0] = jnp.dot(hb, w_ref[:, :width], preferred_element_type=F32).astype(BF16)
    v_ref[0] = jnp.dot(hb, w_ref[:, width:], preferred_element_type=F32).astype(BF16)


def _mem_kv(mem, gain, w_kv):
    batch, mlen, d = mem.shape
    width = w_kv.shape[1] // 2
    out = jax.ShapeDtypeStruct((batch, mlen, width), BF16)
    return pl.pallas_call(
        _mem_kv_kernel,
        grid=(batch,),
        in_specs=[
            pl.BlockSpec((1, mlen, d), lambda b: (b, 0, 0)),
            _resident((1, d)),
            _resident(w_kv.shape),
        ],
        out_specs=[pl.BlockSpec((1, mlen, width), lambda b: (b, 0, 0))] * 2,
        out_shape=[out, out],
        compiler_params=_params("parallel"),
        name="mem_kv",
    )(mem, gain.reshape(1, d), w_kv)


def _merge_kernel(x_ref, qx_ref, g0_ref, g1_ref, g2_ref, ym_ref, ya_ref, kx_ref, vx_ref,
                  bg_ref, wm_ref, wa_ref, wx_ref, wo_ref, o_ref):
    d = x_ref.shape[1]
    xw = qx_ref.shape[1]
    xhd = xw // X_HEADS
    heads = []
    for h in range(X_HEADS):
        cols = slice(h * xhd, (h + 1) * xhd)
        s = lax.dot_general(qx_ref[:, cols], kx_ref[0, :, cols], (((1,), (1,)), ((), ())),
                            preferred_element_type=F32) * (xhd ** -0.5)
        p = jnp.exp(s - jnp.max(s, axis=1, keepdims=True))
        pv = jnp.dot(p.astype(BF16), vx_ref[0, :, cols], preferred_element_type=F32)
        heads.append((pv / jnp.sum(p, axis=1, keepdims=True)).astype(BF16))
    yx = jnp.concatenate(heads, axis=1)

    def gate(g_ref, k):
        return _sigmoid(g_ref[...].astype(F32) + bg_ref[:, k * d:(k + 1) * d])

    merged = (gate(g0_ref, 0) * jnp.dot(ym_ref[...], wm_ref[...], preferred_element_type=F32)
              + gate(g1_ref, 1) * jnp.dot(ya_ref[...], wa_ref[...], preferred_element_type=F32)
              + gate(g2_ref, 2) * jnp.dot(yx, wx_ref[...], preferred_element_type=F32))
    o_ref[...] = x_ref[...] + jnp.dot(merged.astype(BF16), wo_ref[...], preferred_element_type=F32)


def _merge(x, proj, qx_col, g_col, y_m, y_a, kx, vx, b_gate, w_m, w_a, w_x, w_o, *, batch, seq):
    n, d = x.shape
    tm = min(MERGE_ROWS, seq)
    nt = seq // tm

    def row(b, t):
        return b * nt + t

    def proj_spec(col):
        return pl.BlockSpec((tm, d), lambda b, t: (row(b, t), col))

    def mem_spec(a):
        return pl.BlockSpec((1,) + a.shape[1:], lambda b, t: (b, 0, 0))

    return pl.pallas_call(
        _merge_kernel,
        grid=(batch, nt),
        in_specs=[
            pl.BlockSpec((tm, d), lambda b, t: (row(b, t), 0)),
            proj_spec(qx_col), proj_spec(g_col), proj_spec(g_col + 1), proj_spec(g_col + 2),
            pl.BlockSpec((tm, y_m.shape[1]), lambda b, t: (row(b, t), 0)),
            pl.BlockSpec((tm, y_a.shape[1]), lambda b, t: (row(b, t), 0)),
            mem_spec(kx), mem_spec(vx),
            _resident((1, b_gate.shape[0])),
            _resident(w_m.shape), _resident(w_a.shape), _resident(w_x.shape), _resident(w_o.shape),
        ],
        out_specs=pl.BlockSpec((tm, d), lambda b, t: (row(b, t), 0)),
        out_shape=jax.ShapeDtypeStruct((n, d), F32),
        compiler_params=_params("parallel", "parallel"),
        name="merge",
    )(x, proj, proj, proj, proj, y_m, y_a, kx, vx, b_gate.reshape(1, -1), w_m, w_a, w_x, w_o)


def kernel(x, mem, positions, rel_bias, ffn1_norm, ffn1_w_gu, ffn1_w_down, mix_norm, mem_norm, w_in, b_if, b_gate, m_conv_w, m_conv_b, m_wq, m_wk, m_wv, m_head_norm, a_lam_q1, a_lam_k1, a_lam_q2, a_lam_k2, a_head_norm, w_mem_kv, w_branch_m, w_branch_a, w_branch_x, w_out, ffn2_norm, ffn2_w_gu, ffn2_w_down, final_norm):
    del positions
    batch, seq, d = x.shape
    n = batch * seq
    l = LAYER
    m_inner = m_conv_w.shape[-1]
    a_width = a_head_norm.shape[-1]
    x_width = w_mem_kv.shape[-1] // 2
    bf = lambda a: a.astype(BF16)

    gate_lo = 2 * m_inner
    gate_hi = gate_lo + 2 * M_HEADS
    w_all = bf(w_in[l])
    w_tail = w_all[:, gate_hi:]
    w_gate = jnp.pad(w_all[:, gate_lo:gate_hi], ((0, 0), (0, 128 - 2 * M_HEADS)))

    x0 = x.reshape(n, d)
    x1 = _ffn(x0, ffn1_norm[l], bf(ffn1_w_gu[l]), bf(ffn1_w_down[l]), final_norm, final_norm=False)

    proj, gates = _inproj(x1, mix_norm[l], w_all, gate_lo, w_tail, w_gate)

    m_hd = m_inner // M_HEADS
    q_m, k_m, v_m = _mlstm_qkv(proj, m_conv_w[l], m_conv_b[l], bf(m_wq[l]), bf(m_wk[l] * m_hd ** -0.5),
                               bf(m_wv[l]), batch=batch, seq=seq)
    gates_t = gates[:, :2 * M_HEADS].reshape(batch, seq, 2, M_HEADS).transpose(0, 3, 2, 1)
    y_m = _mlstm_scan(q_m, k_m, v_m, gates_t, b_if[l], proj, m_inner // m_hd, m_head_norm[l],
                      batch=batch, seq=seq)

    a_hw = a_width // A_HEADS
    qa_col = 2 * m_inner // a_hw
    lam_vecs = jnp.stack([a_lam_q1[l], a_lam_k1[l], a_lam_q2[l], a_lam_k2[l]])
    y_a = _diff_attn(proj, qa_col, qa_col + A_HEADS, qa_col + 2 * A_HEADS, rel_bias, lam_vecs,
                     a_head_norm[l], batch=batch, seq=seq)

    kx, vx = _mem_kv(mem, mem_norm[l], bf(w_mem_kv[l]))
    qx_col = (2 * m_inner + 3 * a_width) // d
    x2 = _merge(x1, proj, qx_col, qx_col + x_width // d, y_m, y_a, kx, vx, b_gate[l],
                bf(w_branch_m[l]), bf(w_branch_a[l]), bf(w_branch_x[l]), bf(w_out[l]),
                batch=batch, seq=seq)

    out = _ffn(x2, ffn2_norm[l], bf(ffn2_w_gu[l]), bf(ffn2_w_down[l]), final_norm, final_norm=True)
    return out.reshape(batch, seq, d)
```

```python
import functools
import math

import jax
import jax.numpy as jnp
from jax import lax
from jax.experimental import pallas as pl
from jax.experimental.pallas import tpu as pltpu

EPS = 1e-6
BF16 = jnp.bfloat16
F32 = jnp.float32
NEG_BIG = -1e30

M_HEADS = 4
M_CONV = 4
A_HEADS = 8
X_HEADS = 4
N_BRANCH = 3
REL_BUCKETS = 32
REL_MAX_DIST = 128
LAYER = 0

VMEM_LIMIT_BYTES = 56 * 1024 * 1024

FFN_ROWS = 512
INPROJ_ROWS = 1024
INPROJ_COLS = 1024
QKV_ROWS = 512
QKV_SUBTILES = 4
MLSTM_CHUNK = 256
MLSTM_HEAD_GROUP = 2
ATTN_TILE = 256
ATTN_HEAD_GROUP = 4
ATTN_STRIP = 128
ATTN_PV_COLS = 256
ATTN_ONES_ROWS = 16
ATTN_LOOKAHEAD = 8
LOG2_E = math.log2(math.e)
MERGE_ROWS = 512
CONV_HALO = 16


def _params(*sem, flags=None):
    return pltpu.CompilerParams(dimension_semantics=sem, vmem_limit_bytes=VMEM_LIMIT_BYTES, flags=flags)


def _resident(shape):
    zeros = (0,) * len(shape)
    return pl.BlockSpec(shape, lambda *_: zeros, pipeline_mode=pl.Buffered(1))


def _rms(x):
    return x * lax.rsqrt(jnp.mean(x * x, axis=-1, keepdims=True) + EPS)


def _sigmoid(x):
    return 1.0 / (1.0 + jnp.exp(-x))


def _ffn_kernel(x_ref, g_ref, wgu_ref, wd_ref, fg_ref, o_ref, *, d_ff, final_norm):
    x = x_ref[...]
    hb = (_rms(x) * g_ref[...]).astype(BF16)
    gate = jnp.dot(hb, wgu_ref[:, :d_ff], preferred_element_type=F32)
    up = jnp.dot(hb, wgu_ref[:, d_ff:], preferred_element_type=F32)
    act = (gate * _sigmoid(gate) * up).astype(BF16)
    y = x + 0.5 * jnp.dot(act, wd_ref[...], preferred_element_type=F32)
    if final_norm:
        y = _rms(y) * fg_ref[...]
    o_ref[...] = y


def _ffn(x, gain, w_gu, w_down, final_gain, *, final_norm):
    n, d = x.shape
    d_ff = w_down.shape[0]
    tm = min(FFN_ROWS, n)
    return pl.pallas_call(
        functools.partial(_ffn_kernel, d_ff=d_ff, final_norm=final_norm),
        grid=(n // tm,),
        in_specs=[
            pl.BlockSpec((tm, d), lambda i: (i, 0)),
            _resident((1, d)),
            _resident((d, 2 * d_ff)),
            _resident((d_ff, d)),
            _resident((1, d)),
        ],
        out_specs=pl.BlockSpec((tm, d), lambda i: (i, 0)),
        out_shape=jax.ShapeDtypeStruct((n, d), F32),
        compiler_params=_params("parallel"),
        name="ffn_final" if final_norm else "ffn",
    )(x, gain.reshape(1, d), w_gu, w_down, final_gain.reshape(1, d))


def _inproj_kernel(x_ref, g_ref, w_head_ref, w_tail_ref, wgate_ref, p_ref, gates_ref, h_scr, *, head_blocks):
    j = pl.program_id(1)

    @pl.when(j == 0)
    def _():
        hb = (_rms(x_ref[...]) * g_ref[...]).astype(BF16)
        h_scr[...] = hb
        gates_ref[...] = jnp.dot(hb, wgate_ref[...], preferred_element_type=F32)

    @pl.when(j < head_blocks)
    def _():
        p_ref[...] = jnp.dot(h_scr[...], w_head_ref[...], preferred_element_type=F32).astype(BF16)

    @pl.when(j >= head_blocks)
    def _():
        p_ref[...] = jnp.dot(h_scr[...], w_tail_ref[...], preferred_element_type=F32).astype(BF16)


def _inproj(x, gain, w_all, head_width, w_tail, w_gate):
    n, d = x.shape
    tm = min(INPROJ_ROWS, n)
    tn = INPROJ_COLS
    head_blocks = head_width // tn
    width = head_width + w_tail.shape[1]
    return pl.pallas_call(
        functools.partial(_inproj_kernel, head_blocks=head_blocks),
        grid=(n // tm, width // tn),
        in_specs=[
            pl.BlockSpec((tm, d), lambda i, j: (i, 0)),
            _resident((1, d)),
            pl.BlockSpec((d, tn), lambda i, j: (0, jnp.minimum(j, head_blocks - 1))),
            pl.BlockSpec((d, tn), lambda i, j: (0, jnp.maximum(j - head_blocks, 0))),
            _resident(w_gate.shape),
        ],
        out_specs=[
            pl.BlockSpec((tm, tn), lambda i, j: (i, j)),
            pl.BlockSpec((tm, w_gate.shape[1]), lambda i, j: (i, 0)),
        ],
        out_shape=[
            jax.ShapeDtypeStruct((n, width), BF16),
            jax.ShapeDtypeStruct((n, w_gate.shape[1]), F32),
        ],
        scratch_shapes=[pltpu.VMEM((tm, d), BF16)],
        compiler_params=_params("parallel", "arbitrary"),
        name="inproj",
    )(x, gain.reshape(1, d), w_all, w_tail, w_gate)


def _mlstm_qkv_kernel(x_ref, halo_ref, cw_ref, cb_ref, wq_ref, wk_ref, wv_ref,
                      q_ref, k_ref, v_ref):
    ts = x_ref.shape[0]
    xb = x_ref[...]
    v_ref[...] = jnp.dot(xb, wv_ref[0], preferred_element_type=F32).astype(BF16)
    prev = jnp.where(pl.program_id(2) > 0, halo_ref[...].astype(F32), 0.0)
    full = jnp.concatenate([prev, xb.astype(F32)], axis=0)
    cw = cw_ref[...]
    sub = ts // QKV_SUBTILES
    for r0 in range(0, ts, sub):
        conv = cb_ref[...]
        for back in range(M_CONV):
            tap = M_CONV - 1 - back
            start = CONV_HALO + r0 - back
            conv = conv + cw[tap:tap + 1, :] * full[start:start + sub, :]
        xc = (conv * _sigmoid(conv)).astype(BF16)
        q_ref[r0:r0 + sub, :] = jnp.dot(xc, wq_ref[0], preferred_element_type=F32).astype(BF16)
        k_ref[r0:r0 + sub, :] = jnp.dot(xc, wk_ref[0], preferred_element_type=F32).astype(BF16)


def _mlstm_qkv(proj, conv_w, conv_b, wq, wk, wv, *, batch, seq):
    heads, hd, _ = wq.shape
    n = batch * seq
    ts = min(QKV_ROWS, seq)
    nt = seq // ts
    halo_per_tile = ts // CONV_HALO

    def x_map(h, b, t):
        return (b * nt + t, h)

    def halo_map(h, b, t):
        return (jnp.maximum((b * nt + t) * halo_per_tile - 1, 0), h)

    def w_map(h, b, t):
        return (h, 0, 0)

    out = jax.ShapeDtypeStruct((n, heads * hd), BF16)
    return pl.pallas_call(
        _mlstm_qkv_kernel,
        grid=(heads, batch, nt),
        in_specs=[
            pl.BlockSpec((ts, hd), x_map),
            pl.BlockSpec((CONV_HALO, hd), halo_map),
            pl.BlockSpec((M_CONV, hd), lambda h, b, t: (0, h)),
            pl.BlockSpec((1, hd), lambda h, b, t: (0, h)),
            pl.BlockSpec((1, hd, hd), w_map),
            pl.BlockSpec((1, hd, hd), w_map),
            pl.BlockSpec((1, hd, hd), w_map),
        ],
        out_specs=[pl.BlockSpec((ts, hd), x_map)] * 3,
        out_shape=[out, out, out],
        compiler_params=_params("parallel", "parallel", "parallel"),
        name="mlstm_qkv",
    )(proj, proj, conv_w, conv_b.reshape(1, -1), wq, wk, wv)


def _mlstm_scan_kernel(q_ref, k_ref, v_ref, g_ref, bias_ref, o_ref, gain_ref, y_ref,
                       c_scr, n_scr, m_scr):
    @pl.when(pl.program_id(2) == 0)
    def _():
        c_scr[...] = jnp.zeros_like(c_scr)
        n_scr[...] = jnp.zeros_like(n_scr)
        m_scr[...] = jnp.zeros_like(m_scr)

    group, hd, _ = c_scr.shape
    chunk = q_ref.shape[0]
    heads = range(group)
    cols = [slice(g * hd, (g + 1) * hd) for g in heads]
    q = [q_ref[:, cols[g]] for g in heads]
    k = [k_ref[:, cols[g]] for g in heads]
    v = [v_ref[:, cols[g]] for g in heads]
    c_prev = [c_scr[g] for g in heads]
    n_prev = [n_scr[g] for g in heads]
    m_prev = [m_scr[g] for g in heads]

    qk = [lax.dot_general(q[g], k[g], (((1,), (1,)), ((), ())), preferred_element_type=F32) for g in heads]
    qc = [jnp.dot(q[g], c_prev[g].astype(BF16), preferred_element_type=F32) for g in heads]

    t_idx = lax.broadcasted_iota(jnp.int32, (chunk, chunk), 0)
    s_idx = lax.broadcasted_iota(jnp.int32, (chunk, chunk), 1)
    causal = s_idx <= t_idx
    diag = s_idx == t_idx

    def to_col(row):
        return jnp.sum(jnp.where(diag, row, 0.0), axis=1, keepdims=True)

    def to_row(col):
        return jnp.sum(jnp.where(diag, col, 0.0), axis=0, keepdims=True)

    s, w_inter, m_t, w_s, decay, m_new = [], [], [], [], [], []
    for g in heads:
        gates = g_ref[0, g] + bias_ref[g]
        ig = gates[0:1, :]
        fpre = gates[1:2, :]
        lf = jnp.minimum(fpre, 0.0) - jnp.log1p(jnp.exp(-jnp.abs(fpre)))
        b_col = jnp.sum(jnp.where(causal, lf, 0.0), axis=1, keepdims=True)
        b_row = to_row(b_col)
        dlog = jnp.where(causal, b_col - b_row + ig, NEG_BIG)
        inter = b_col + m_prev[g]
        m_t.append(jnp.maximum(inter, jnp.max(dlog, axis=1, keepdims=True)))
        s.append(qk[g] * jnp.exp(dlog - m_t[g]))
        w_inter.append(jnp.exp(inter - m_t[g]))
        b_last = b_col[chunk - 1:chunk, :]
        tail = b_last - b_row + ig
        m_new.append(jnp.maximum(b_last + m_prev[g], jnp.max(tail, axis=1, keepdims=True)))
        decay.append(jnp.exp(b_last + m_prev[g] - m_new[g]))
        w_s.append(jnp.exp(to_col(tail) - m_new[g]))

    sv = [jnp.dot(s[g].astype(BF16), v[g], preferred_element_type=F32) for g in heads]
    kw = [k[g].astype(F32) * w_s[g] for g in heads]
    kv = [jnp.dot(kw[g].T.astype(BF16), v[g], preferred_element_type=F32) for g in heads]

    for g in heads:
        num = sv[g] + w_inter[g] * qc[g]
        qn = jnp.sum(q[g].astype(F32) * n_prev[g], axis=1, keepdims=True)
        den = jnp.sum(s[g], axis=1, keepdims=True) + w_inter[g] * qn
        h = num / jnp.maximum(jnp.abs(den), jnp.exp(-m_t[g]))
        c_scr[g] = decay[g] * c_prev[g] + kv[g]
        n_scr[g] = decay[g] * n_prev[g] + jnp.sum(kw[g], axis=0, keepdims=True)
        m_scr[g] = m_new[g]
        gate = _sigmoid(o_ref[:, cols[g]].astype(F32))
        y_ref[:, cols[g]] = (gate * (_rms(h) * gain_ref[:, cols[g]])).astype(BF16)


def _mlstm_scan(q, k, v, gates_t, b_if, proj, o_col_block, head_gain, *, batch, seq):
    n, width = q.shape
    heads = gates_t.shape[1]
    hd = width // heads
    group = MLSTM_HEAD_GROUP
    gw = group * hd
    assert heads % group == 0 and o_col_block % group == 0
    chunk = min(MLSTM_CHUNK, seq)
    nc = seq // chunk

    def qkv_map(b, h, c):
        return (b * nc + c, h)

    return pl.pallas_call(
        _mlstm_scan_kernel,
        grid=(batch, heads // group, nc),
        in_specs=[
            pl.BlockSpec((chunk, gw), qkv_map),
            pl.BlockSpec((chunk, gw), qkv_map),
            pl.BlockSpec((chunk, gw), qkv_map),
            pl.BlockSpec((1, group, 2, chunk), lambda b, h, c: (b, h, 0, c)),
            pl.BlockSpec((group, 2, 1), lambda b, h, c: (h, 0, 0)),
            pl.BlockSpec((chunk, gw), lambda b, h, c: (b * nc + c, o_col_block // group + h)),
            pl.BlockSpec((1, gw), lambda b, h, c: (0, h)),
        ],
        out_specs=pl.BlockSpec((chunk, gw), qkv_map),
        out_shape=jax.ShapeDtypeStruct((n, width), BF16),
        scratch_shapes=[pltpu.VMEM((group, hd, hd), F32), pltpu.VMEM((group, 1, hd), F32),
                        pltpu.VMEM((group, 1, 1), F32)],
        compiler_params=_params("parallel", "parallel", "arbitrary"),
        name="mlstm_scan",
    )(q, k, v, gates_t, b_if.reshape(2, heads, 1).transpose(1, 0, 2), proj, head_gain.reshape(1, -1))


def _diff_attn_kernel(far_ref, q_ref, k_ref, v_ref, band_ref, lam_ref, gain_ref, y_ref,
                      vt_scr, acc_scr, st_scr, *, out_scale, lam_init):
    group, n_tiles, rows, tile = vt_scr.shape
    width = q_ref.shape[1] // group
    head0 = pl.program_id(1) * group
    qi = pl.program_id(2)
    half = width // 2

    @pl.when(qi == 0)
    def _():
        ones = jnp.ones((rows - width, tile), BF16)
        for g in range(group):
            for j in range(n_tiles):
                v_tile = v_ref[j * tile:(j + 1) * tile, g * width:(g + 1) * width]
                vt_scr[g, j] = jnp.concatenate([v_tile.astype(F32).T.astype(BF16), ones], axis=0)

    q_all = q_ref[...]
    lane = lax.broadcasted_iota(jnp.int32, (tile, width), 1)
    qs = []
    for g in range(group):
        q = q_all[:, g * width:(g + 1) * width].astype(F32) * (half ** -0.5 * LOG2_E)
        stacked = jnp.concatenate([jnp.where(lane < half, q, 0.0), jnp.where(lane >= half, q, 0.0)], axis=0)
        qs.append(stacked.astype(BF16))

    acc_scr[...] = jnp.zeros_like(acc_scr)

    def scores(kj, g):
        start = pl.multiple_of(kj * tile, tile)
        return lax.dot_general(k_ref[pl.ds(start, tile), g * width:(g + 1) * width], qs[g],
                               (((1,), (1,)), ((), ())), preferred_element_type=F32)

    def block(tiles, maxes):
        units = [(kj, band_idx, g) for kj, band_idx in tiles for g in range(group)]
        all_scores = []
        for u, (kj, band_idx, g) in enumerate(units):
            if band_idx is None:
                all_scores.append(scores(kj, g))
            else:
                st_scr[u] = scores(kj, g)
                all_scores.append(None)
        maxes = list(maxes)
        for u, (st, (kj, band_idx, g)) in enumerate(zip(all_scores, units)):
            m_prev = maxes[g]
            if band_idx is None:
                shift = far_bias[g]
            else:
                shift = 0.0
                band = band_ref[g, band_idx]
                st = st_scr[u] + jnp.concatenate([band, band], axis=1)
            vt = vt_scr[g, kj]
            m_cols = []
            for c0 in range(0, 2 * tile, ATTN_PV_COLS):
                p_cols, a_cols = [], []
                for c in range(c0, c0 + ATTN_PV_COLS, ATTN_STRIP):
                    cols = slice(c, c + ATTN_STRIP)
                    s_c = st[:, cols]
                    m_c = jnp.maximum(m_prev[:, cols], jnp.max(s_c, axis=0, keepdims=True) + shift)
                    a_cols.append(jnp.exp2(m_prev[:, cols] - m_c))
                    p_cols.append(jnp.exp2(s_c - (m_c - shift)).astype(BF16))
                    m_cols.append(m_c)
                pv = jnp.dot(vt, jnp.concatenate(p_cols, axis=1), preferred_element_type=F32)
                wide = slice(c0, c0 + ATTN_PV_COLS)
                acc_scr[g, :, wide] = jnp.concatenate(a_cols, axis=1) * acc_scr[g, :, wide] + pv
            maxes[g] = jnp.concatenate(m_cols, axis=1)
        return tuple(maxes)

    def far(first, count):
        return tuple((first + i, None) for i in range(count))

    far_bias = tuple(far_ref[head0 + g] * LOG2_E for g in range(group))
    n_far = jnp.maximum(qi - 1, 0)
    n_quads = n_far // 4
    rest = n_far - 4 * n_quads
    maxes = tuple(jnp.full((1, 2 * tile), NEG_BIG, F32) for _ in range(group))
    maxes = lax.fori_loop(0, n_quads, lambda i, c: block(far(4 * i, 4), c), maxes)
    maxes = lax.cond(rest >= 2, lambda: block(far(4 * n_quads, 2), maxes), lambda: maxes)
    maxes = lax.cond(rest % 2 == 1, lambda: block(far(n_far - 1, 1), maxes), lambda: maxes)
    lax.cond(qi > 0, lambda: block(((qi - 1, 0), (qi, 1)), maxes), lambda: block(((qi, 1),), maxes))


    lam_v = lam_ref[...]
    lam = (jnp.exp(jnp.sum(lam_v[0:1, :] * lam_v[1:2, :], axis=1, keepdims=True))
           - jnp.exp(jnp.sum(lam_v[2:3, :] * lam_v[3:4, :], axis=1, keepdims=True)) + lam_init)
    for g in range(group):
        o = acc_scr[g, :width, :] / acc_scr[g, width:width + 1, :]
        out = (o[:, :tile] - lam * o[:, tile:]).T
        gain = gain_ref[:, g * width:(g + 1) * width]
        y_ref[:, g * width:(g + 1) * width] = (_rms(out) * gain * out_scale).astype(BF16)


def _t5_bucket(n):
    max_exact = REL_BUCKETS // 2
    nf = jnp.maximum(n, 1).astype(F32)
    large = max_exact + (jnp.log(nf / max_exact) / math.log(REL_MAX_DIST / max_exact)
                         * (REL_BUCKETS - max_exact)).astype(jnp.int32)
    large = jnp.minimum(large, REL_BUCKETS - 1)
    return jnp.where(n < max_exact, n, large)


def _diff_attn(proj, q_col, k_col, v_col, rel_bias, lam_vecs, head_gain, *, batch, seq):
    n = proj.shape[0]
    heads = rel_bias.shape[1]
    hw = head_gain.shape[0] // heads
    tile = min(ATTN_TILE, seq)
    nq = seq // tile
    assert tile >= REL_MAX_DIST, "distances beyond one tile must all fall in the last bucket"
    group = ATTN_HEAD_GROUP
    gw = group * hw
    assert heads % group == 0 and q_col % group == 0 and k_col % group == 0 and v_col % group == 0

    bucket = _t5_bucket(jnp.arange(2 * tile, dtype=jnp.int32))
    by_dist = jnp.zeros((heads, 2 * tile), F32)
    for b in range(REL_BUCKETS):
        by_dist = jnp.where(bucket[None] == b, rel_bias[b].astype(F32)[:, None], by_dist)
    masked = jnp.full((heads, tile - 1), NEG_BIG, F32)
    line = jnp.concatenate([masked, by_dist * LOG2_E, masked[:, :2]], axis=1)
    length = line.shape[1] - 1
    skew = jnp.tile(line, (1, tile))[:, :tile * length].reshape(heads, tile, length)
    band = jnp.stack([skew[:, :, 2 * tile - 1:3 * tile - 1], skew[:, :, tile - 1:2 * tile - 1]], axis=1)
    far = rel_bias[REL_BUCKETS - 1].astype(F32)

    lam_init = 0.8 - 0.6 * math.exp(-0.3 * LAYER)
    return pl.pallas_call(
        functools.partial(_diff_attn_kernel, out_scale=1.0 - lam_init, lam_init=lam_init),
        grid=(batch, heads // group, nq),
        in_specs=[
            pl.BlockSpec(memory_space=pltpu.SMEM),
            pl.BlockSpec((tile, gw), lambda b, h, i: (b * nq + i, q_col // group + h)),
            pl.BlockSpec((seq, gw), lambda b, h, i: (b, k_col // group + h)),
            pl.BlockSpec((seq, gw), lambda b, h, i: (b, v_col // group + h)),
            pl.BlockSpec((group, 2, tile, tile), lambda b, h, i: (h, 0, 0, 0)),
            _resident(lam_vecs.shape),
            pl.BlockSpec((1, gw), lambda b, h, i: (0, h)),
        ],
        out_specs=pl.BlockSpec((tile, gw), lambda b, h, i: (b * nq + i, h)),
        out_shape=jax.ShapeDtypeStruct((n, heads * hw), BF16),
        scratch_shapes=[
            pltpu.VMEM((group, nq, hw + ATTN_ONES_ROWS, tile), BF16),
            pltpu.VMEM((group, hw + ATTN_ONES_ROWS, 2 * tile), F32),
            pltpu.VMEM((2 * group, tile, 2 * tile), F32),
        ],
        compiler_params=_params("parallel", "parallel", "arbitrary"),
        name="diff_attn",
    )(far, proj, proj, proj, band, lam_vecs, head_gain.reshape(1, -1))


def _mem_kv_kernel(mem_ref, g_ref, w_ref, k_ref, v_ref):
    width = k_ref.shape[-1]
    hb = (_rms(mem_ref[0]) * g_ref[...]).astype(BF16)
    k_ref[0] = jnp.dot(hb, w_ref[:, :width], preferred_element_type=F32).astype(BF16)
    v_ref[0] = jnp.dot(hb, w_ref[:, width:], preferred_element_type=F32).astype(BF16)


def _mem_kv(mem, gain, w_kv):
    batch, mlen, d = mem.shape
    width = w_kv.shape[1] // 2
    out = jax.ShapeDtypeStruct((batch, mlen, width), BF16)
    return pl.pallas_call(
        _mem_kv_kernel,
        grid=(batch,),
        in_specs=[
            pl.BlockSpec((1, mlen, d), lambda b: (b, 0, 0)),
            _resident((1, d)),
            _resident(w_kv.shape),
        ],
        out_specs=[pl.BlockSpec((1, mlen, width), lambda b: (b, 0, 0))] * 2,
        out_shape=[out, out],
        compiler_params=_params("parallel"),
        name="mem_kv",
    )(mem, gain.reshape(1, d), w_kv)


def _merge_kernel(x_ref, qx_ref, g0_ref, g1_ref, g2_ref, ym_ref, ya_ref, kx_ref, vx_ref,
                  bg_ref, wm_ref, wa_ref, wx_ref, wo_ref, o_ref):
    d = x_ref.shape[1]
    xw = qx_ref.shape[1]
    xhd = xw // X_HEADS
    heads = []
    for h in range(X_HEADS):
        cols = slice(h * xhd, (h + 1) * xhd)
        s = lax.dot_general(qx_ref[:, cols], kx_ref[0, :, cols], (((1,), (1,)), ((), ())),
                            preferred_element_type=F32) * (xhd ** -0.5)
        p = jnp.exp(s - jnp.max(s, axis=1, keepdims=True))
        pv = jnp.dot(p.astype(BF16), vx_ref[0, :, cols], preferred_element_type=F32)
        heads.append((pv / jnp.sum(p, axis=1, keepdims=True)).astype(BF16))
    yx = jnp.concatenate(heads, axis=1)

    def gate(g_ref, k):
        return _sigmoid(g_ref[...].astype(F32) + bg_ref[:, k * d:(k + 1) * d])

    merged = (gate(g0_ref, 0) * jnp.dot(ym_ref[...], wm_ref[...], preferred_element_type=F32)
              + gate(g1_ref, 1) * jnp.dot(ya_ref[...], wa_ref[...], preferred_element_type=F32)
              + gate(g2_ref, 2) * jnp.dot(yx, wx_ref[...], preferred_element_type=F32))
    o_ref[...] = x_ref[...] + jnp.dot(merged.astype(BF16), wo_ref[...], preferred_element_type=F32)


def _merge(x, proj, qx_col, g_col, y_m, y_a, kx, vx, b_gate, w_m, w_a, w_x, w_o, *, batch, seq):
    n, d = x.shape
    tm = min(MERGE_ROWS, seq)
    nt = seq // tm

    def row(b, t):
        return b * nt + t

    def proj_spec(col):
        return pl.BlockSpec((tm, d), lambda b, t: (row(b, t), col))

    def mem_spec(a):
        return pl.BlockSpec((1,) + a.shape[1:], lambda b, t: (b, 0, 0))

    return pl.pallas_call(
        _merge_kernel,
        grid=(batch, nt),
        in_specs=[
            pl.BlockSpec((tm, d), lambda b, t: (row(b, t), 0)),
            proj_spec(qx_col), proj_spec(g_col), proj_spec(g_col + 1), proj_spec(g_col + 2),
            pl.BlockSpec((tm, y_m.shape[1]), lambda b, t: (row(b, t), 0)),
            pl.BlockSpec((tm, y_a.shape[1]), lambda b, t: (row(b, t), 0)),
            mem_spec(kx), mem_spec(vx),
            _resident((1, b_gate.shape[0])),
            _resident(w_m.shape), _resident(w_a.shape), _resident(w_x.shape), _resident(w_o.shape),
        ],
        out_specs=pl.BlockSpec((tm, d), lambda b, t: (row(b, t), 0)),
        out_shape=jax.ShapeDtypeStruct((n, d), F32),
        compiler_params=_params("parallel", "parallel"),
        name="merge",
    )(x, proj, proj, proj, proj, y_m, y_a, kx, vx, b_gate.reshape(1, -1), w_m, w_a, w_x, w_o)


def kernel(x, mem, positions, rel_bias, ffn1_norm, ffn1_w_gu, ffn1_w_down, mix_norm, mem_norm, w_in, b_if, b_gate, m_conv_w, m_conv_b, m_wq, m_wk, m_wv, m_head_norm, a_lam_q1, a_lam_k1, a_lam_q2, a_lam_k2, a_head_norm, w_mem_kv, w_branch_m, w_branch_a, w_branch_x, w_out, ffn2_norm, ffn2_w_gu, ffn2_w_down, final_norm):
    del positions
    batch, seq, d = x.shape
    n = batch * seq
    l = LAYER
    m_inner = m_conv_w.shape[-1]
    a_width = a_head_norm.shape[-1]
    x_width = w_mem_kv.shape[-1] // 2
    bf = lambda a: a.astype(BF16)

    gate_lo = 2 * m_inner
    gate_hi = gate_lo + 2 * M_HEADS
    w_all = bf(w_in[l])
    w_tail = w_all[:, gate_hi:]
    w_gate = jnp.pad(w_all[:, gate_lo:gate_hi], ((0, 0), (0, 128 - 2 * M_HEADS)))

    x0 = x.reshape(n, d)
    x1 = _ffn(x0, ffn1_norm[l], bf(ffn1_w_gu[l]), bf(ffn1_w_down[l]), final_norm, final_norm=False)

    proj, gates = _inproj(x1, mix_norm[l], w_all, gate_lo, w_tail, w_gate)

    m_hd = m_inner // M_HEADS
    q_m, k_m, v_m = _mlstm_qkv(proj, m_conv_w[l], m_conv_b[l], bf(m_wq[l]), bf(m_wk[l] * m_hd ** -0.5),
                               bf(m_wv[l]), batch=batch, seq=seq)
    gates_t = gates[:, :2 * M_HEADS].reshape(batch, seq, 2, M_HEADS).transpose(0, 3, 2, 1)
    y_m = _mlstm_scan(q_m, k_m, v_m, gates_t, b_if[l], proj, m_inner // m_hd, m_head_norm[l],
                      batch=batch, seq=seq)

    a_hw = a_width // A_HEADS
    qa_col = 2 * m_inner // a_hw
    lam_vecs = jnp.stack([a_lam_q1[l], a_lam_k1[l], a_lam_q2[l], a_lam_k2[l]])
    y_a = _diff_attn(proj, qa_col, qa_col + A_HEADS, qa_col + 2 * A_HEADS, rel_bias, lam_vecs,
                     a_head_norm[l], batch=batch, seq=seq)

    kx, vx = _mem_kv(mem, mem_norm[l], bf(w_mem_kv[l]))
    qx_col = (2 * m_inner + 3 * a_width) // d
    x2 = _merge(x1, proj, qx_col, qx_col + x_width // d, y_m, y_a, kx, vx, b_gate[l],
                bf(w_branch_m[l]), bf(w_branch_a[l]), bf(w_branch_x[l]), bf(w_out[l]),
                batch=batch, seq=seq)

    out = _ffn(x2, ffn2_norm[l], bf(ffn2_w_gu[l]), bf(ffn2_w_down[l]), final_norm, final_norm=True)
    return out.reshape(batch, seq, d)
```

```python
import functools
import math

import jax
import jax.numpy as jnp
from jax import lax
from jax.experimental import pallas as pl
from jax.experimental.pallas import tpu as pltpu

EPS = 1e-6
BF16 = jnp.bfloat16
F32 = jnp.float32
NEG_BIG = -1e30

M_HEADS = 4
M_CONV = 4
A_HEADS = 8
X_HEADS = 4
N_BRANCH = 3
REL_BUCKETS = 32
REL_MAX_DIST = 128
LAYER = 0

VMEM_LIMIT_BYTES = 56 * 1024 * 1024

FFN_ROWS = 512
INPROJ_ROWS = 1024
INPROJ_COLS = 1024
QKV_ROWS = 2048
QKV_SUBTILES = 8
MLSTM_CHUNK = 256
MLSTM_HEAD_GROUP = 2
ATTN_TILE = 256
ATTN_HEAD_GROUP = 4
ATTN_STRIP = 128
ATTN_PV_COLS = 256
ATTN_ONES_ROWS = 16
ATTN_LOOKAHEAD = 8
LOG2_E = math.log2(math.e)
MERGE_ROWS = 512
CONV_HALO = 16


def _params(*sem, flags=None):
    return pltpu.CompilerParams(dimension_semantics=sem, vmem_limit_bytes=VMEM_LIMIT_BYTES, flags=flags)


def _resident(shape):
    zeros = (0,) * len(shape)
    return pl.BlockSpec(shape, lambda *_: zeros, pipeline_mode=pl.Buffered(1))


def _rms(x):
    return x * lax.rsqrt(jnp.mean(x * x, axis=-1, keepdims=True) + EPS)


def _sigmoid(x):
    return 1.0 / (1.0 + jnp.exp(-x))


def _ffn_kernel(x_ref, g_ref, wgu_ref, wd_ref, fg_ref, o_ref, *, d_ff, final_norm):
    x = x_ref[...]
    hb = (_rms(x) * g_ref[...]).astype(BF16)
    gate = jnp.dot(hb, wgu_ref[:, :d_ff], preferred_element_type=F32)
    up = jnp.dot(hb, wgu_ref[:, d_ff:], preferred_element_type=F32)
    act = (gate * _sigmoid(gate) * up).astype(BF16)
    y = x + 0.5 * jnp.dot(act, wd_ref[...], preferred_element_type=F32)
    if final_norm:
        y = _rms(y) * fg_ref[...]
    o_ref[...] = y


def _ffn(x, gain, w_gu, w_down, final_gain, *, final_norm):
    n, d = x.shape
    d_ff = w_down.shape[0]
    tm = min(FFN_ROWS, n)
    return pl.pallas_call(
        functools.partial(_ffn_kernel, d_ff=d_ff, final_norm=final_norm),
        grid=(n // tm,),
        in_specs=[
            pl.BlockSpec((tm, d), lambda i: (i, 0)),
            _resident((1, d)),
            _resident((d, 2 * d_ff)),
            _resident((d_ff, d)),
            _resident((1, d)),
        ],
        out_specs=pl.BlockSpec((tm, d), lambda i: (i, 0)),
        out_shape=jax.ShapeDtypeStruct((n, d), F32),
        compiler_params=_params("parallel"),
        name="ffn_final" if final_norm else "ffn",
    )(x, gain.reshape(1, d), w_gu, w_down, final_gain.reshape(1, d))


def _inproj_kernel(x_ref, g_ref, w_head_ref, w_tail_ref, wgate_ref, p_ref, gates_ref, h_scr, *, head_blocks):
    j = pl.program_id(1)

    @pl.when(j == 0)
    def _():
        hb = (_rms(x_ref[...]) * g_ref[...]).astype(BF16)
        h_scr[...] = hb
        gates_ref[...] = jnp.dot(hb, wgate_ref[...], preferred_element_type=F32)

    @pl.when(j < head_blocks)
    def _():
        p_ref[...] = jnp.dot(h_scr[...], w_head_ref[...], preferred_element_type=F32).astype(BF16)

    @pl.when(j >= head_blocks)
    def _():
        p_ref[...] = jnp.dot(h_scr[...], w_tail_ref[...], preferred_element_type=F32).astype(BF16)


def _inproj(x, gain, w_all, head_width, w_tail, w_gate):
    n, d = x.shape
    tm = min(INPROJ_ROWS, n)
    tn = INPROJ_COLS
    head_blocks = head_width // tn
    tail_blocks = w_tail.shape[1] // tn
    width = head_width + w_tail.shape[1]
    return pl.pallas_call(
        functools.partial(_inproj_kernel, head_blocks=head_blocks),
        grid=(n // tm, width // tn),
        in_specs=[
            pl.BlockSpec((tm, d), lambda i, j: (i, 0)),
            _resident((1, d)),
            pl.BlockSpec((d, tn), lambda i, j: (0, jnp.minimum(j, head_blocks - 1))),
            pl.BlockSpec((d, tn), lambda i, j: (0, jnp.where(j < head_blocks, tail_blocks - 1, j - head_blocks))),
            _resident(w_gate.shape),
        ],
        out_specs=[
            pl.BlockSpec((tm, tn), lambda i, j: (i, j)),
            pl.BlockSpec((tm, w_gate.shape[1]), lambda i, j: (i, 0)),
        ],
        out_shape=[
            jax.ShapeDtypeStruct((n, width), BF16),
            jax.ShapeDtypeStruct((n, w_gate.shape[1]), F32),
        ],
        scratch_shapes=[pltpu.VMEM((tm, d), BF16)],
        compiler_params=_params("parallel", "arbitrary"),
        name="inproj",
    )(x, gain.reshape(1, d), w_all, w_tail, w_gate)


def _mlstm_qkv_kernel(x_ref, halo_ref, cw_ref, cb_ref, wq_ref, wk_ref, wv_ref,
                      q_ref, k_ref, v_ref):
    ts = x_ref.shape[0]
    xb = x_ref[...]
    v_ref[...] = jnp.dot(xb, wv_ref[0], preferred_element_type=F32).astype(BF16)
    prev = jnp.where(pl.program_id(2) > 0, halo_ref[...].astype(F32), 0.0)
    full = jnp.concatenate([prev, xb.astype(F32)], axis=0)
    cw = cw_ref[...]
    sub = ts // QKV_SUBTILES
    for r0 in range(0, ts, sub):
        conv = cb_ref[...]
        for back in range(M_CONV):
            tap = M_CONV - 1 - back
            start = CONV_HALO + r0 - back
            conv = conv + cw[tap:tap + 1, :] * full[start:start + sub, :]
        xc = (conv * _sigmoid(conv)).astype(BF16)
        q_ref[r0:r0 + sub, :] = jnp.dot(xc, wq_ref[0], preferred_element_type=F32).astype(BF16)
        k_ref[r0:r0 + sub, :] = jnp.dot(xc, wk_ref[0], preferred_element_type=F32).astype(BF16)


def _mlstm_qkv(proj, conv_w, conv_b, wq, wk, wv, *, batch, seq):
    heads, hd, _ = wq.shape
    n = batch * seq
    ts = min(QKV_ROWS, seq)
    nt = seq // ts
    halo_per_tile = ts // CONV_HALO

    def x_map(h, b, t):
        return (b * nt + t, h)

    def halo_map(h, b, t):
        return (jnp.maximum((b * nt + t) * halo_per_tile - 1, 0), h)

    def w_map(h, b, t):
        return (h, 0, 0)

    out = jax.ShapeDtypeStruct((n, heads * hd), BF16)
    return pl.pallas_call(
        _mlstm_qkv_kernel,
        grid=(heads, batch, nt),
        in_specs=[
            pl.BlockSpec((ts, hd), x_map),
            pl.BlockSpec((CONV_HALO, hd), halo_map),
            pl.BlockSpec((M_CONV, hd), lambda h, b, t: (0, h)),
            pl.BlockSpec((1, hd), lambda h, b, t: (0, h)),
            pl.BlockSpec((1, hd, hd), w_map),
            pl.BlockSpec((1, hd, hd), w_map),
            pl.BlockSpec((1, hd, hd), w_map),
        ],
        out_specs=[pl.BlockSpec((ts, hd), x_map)] * 3,
        out_shape=[out, out, out],
        compiler_params=_params("parallel", "parallel", "parallel"),
        name="mlstm_qkv",
    )(proj, proj, conv_w, conv_b.reshape(1, -1), wq, wk, wv)


def _mlstm_scan_kernel(q_ref, k_ref, v_ref, g_ref, bias_ref, o_ref, gain_ref, y_ref,
                       c_scr, n_scr, m_scr):
    @pl.when(pl.program_id(2) == 0)
    def _():
        c_scr[...] = jnp.zeros_like(c_scr)
        n_scr[...] = jnp.zeros_like(n_scr)
        m_scr[...] = jnp.zeros_like(m_scr)

    group, hd, _ = c_scr.shape
    chunk = q_ref.shape[0]
    heads = range(group)
    cols = [slice(g * hd, (g + 1) * hd) for g in heads]
    q = [q_ref[:, cols[g]] for g in heads]
    k = [k_ref[:, cols[g]] for g in heads]
    v = [v_ref[:, cols[g]] for g in heads]
    c_prev = [c_scr[g] for g in heads]
    n_prev = [n_scr[g] for g in heads]
    m_prev = [m_scr[g] for g in heads]

    qk = [lax.dot_general(q[g], k[g], (((1,), (1,)), ((), ())), preferred_element_type=F32) for g in heads]
    qc = [jnp.dot(q[g], c_prev[g].astype(BF16), preferred_element_type=F32) for g in heads]

    t_idx = lax.broadcasted_iota(jnp.int32, (chunk, chunk), 0)
    s_idx = lax.broadcasted_iota(jnp.int32, (chunk, chunk), 1)
    causal = s_idx <= t_idx
    diag = s_idx == t_idx

    def to_col(row):
        return jnp.sum(jnp.where(diag, row, 0.0), axis=1, keepdims=True)

    def to_row(col):
        return jnp.sum(jnp.where(diag, col, 0.0), axis=0, keepdims=True)

    s, w_inter, m_t, w_s, decay, m_new = [], [], [], [], [], []
    for g in heads:
        gates = g_ref[0, g] + bias_ref[g]
        ig = gates[0:1, :]
        fpre = gates[1:2, :]
        lf = jnp.minimum(fpre, 0.0) - jnp.log1p(jnp.exp(-jnp.abs(fpre)))
        b_col = jnp.sum(jnp.where(causal, lf, 0.0), axis=1, keepdims=True)
        b_row = to_row(b_col)
        dlog = jnp.where(causal, b_col - b_row + ig, NEG_BIG)
        inter = b_col + m_prev[g]
        m_t.append(jnp.maximum(inter, jnp.max(dlog, axis=1, keepdims=True)))
        s.append(qk[g] * jnp.exp(dlog - m_t[g]))
        w_inter.append(jnp.exp(inter - m_t[g]))
        b_last = b_col[chunk - 1:chunk, :]
        tail = b_last - b_row + ig
        m_new.append(jnp.maximum(b_last + m_prev[g], jnp.max(tail, axis=1, keepdims=True)))
        decay.append(jnp.exp(b_last + m_prev[g] - m_new[g]))
        w_s.append(jnp.exp(to_col(tail) - m_new[g]))

    sv = [jnp.dot(s[g].astype(BF16), v[g], preferred_element_type=F32) for g in heads]
    kw = [k[g].astype(F32) * w_s[g] for g in heads]
    kv = [jnp.dot(kw[g].T.astype(BF16), v[g], preferred_element_type=F32) for g in heads]

    for g in heads:
        num = sv[g] + w_inter[g] * qc[g]
        qn = jnp.sum(q[g].astype(F32) * n_prev[g], axis=1, keepdims=True)
        den = jnp.sum(s[g], axis=1, keepdims=True) + w_inter[g] * qn
        h = num / jnp.maximum(jnp.abs(den), jnp.exp(-m_t[g]))
        c_scr[g] = decay[g] * c_prev[g] + kv[g]
        n_scr[g] = decay[g] * n_prev[g] + jnp.sum(kw[g], axis=0, keepdims=True)
        m_scr[g] = m_new[g]
        gate = _sigmoid(o_ref[:, cols[g]].astype(F32))
        y_ref[:, cols[g]] = (gate * (_rms(h) * gain_ref[:, cols[g]])).astype(BF16)


def _mlstm_scan(q, k, v, gates_t, b_if, proj, o_col_block, head_gain, *, batch, seq):
    n, width = q.shape
    heads = gates_t.shape[1]
    hd = width // heads
    group = MLSTM_HEAD_GROUP
    gw = group * hd
    assert heads % group == 0 and o_col_block % group == 0
    chunk = min(MLSTM_CHUNK, seq)
    nc = seq // chunk

    def qkv_map(b, h, c):
        return (b * nc + c, h)

    return pl.pallas_call(
        _mlstm_scan_kernel,
        grid=(batch, heads // group, nc),
        in_specs=[
            pl.BlockSpec((chunk, gw), qkv_map),
            pl.BlockSpec((chunk, gw), qkv_map),
            pl.BlockSpec((chunk, gw), qkv_map),
            pl.BlockSpec((1, group, 2, chunk), lambda b, h, c: (b, h, 0, c)),
            pl.BlockSpec((group, 2, 1), lambda b, h, c: (h, 0, 0)),
            pl.BlockSpec((chunk, gw), lambda b, h, c: (b * nc + c, o_col_block // group + h)),
            pl.BlockSpec((1, gw), lambda b, h, c: (0, h)),
        ],
        out_specs=pl.BlockSpec((chunk, gw), qkv_map),
        out_shape=jax.ShapeDtypeStruct((n, width), BF16),
        scratch_shapes=[pltpu.VMEM((group, hd, hd), F32), pltpu.VMEM((group, 1, hd), F32),
                        pltpu.VMEM((group, 1, 1), F32)],
        compiler_params=_params("parallel", "parallel", "arbitrary"),
        name="mlstm_scan",
    )(q, k, v, gates_t, b_if.reshape(2, heads, 1).transpose(1, 0, 2), proj, head_gain.reshape(1, -1))


def _diff_attn_kernel(far_ref, q_ref, k_ref, v_ref, band_ref, lam_ref, gain_ref, y_ref,
                      vt_scr, acc_scr, st_scr, *, out_scale, lam_init):
    group, n_tiles, rows, tile = vt_scr.shape
    width = q_ref.shape[1] // group
    head0 = pl.program_id(1) * group
    qi = pl.program_id(2)
    half = width // 2

    @pl.when(qi == 0)
    def _():
        ones = jnp.ones((rows - width, tile), BF16)
        for g in range(group):
            for j in range(n_tiles):
                v_tile = v_ref[j * tile:(j + 1) * tile, g * width:(g + 1) * width]
                vt_scr[g, j] = jnp.concatenate([v_tile.astype(F32).T.astype(BF16), ones], axis=0)

    q_all = q_ref[...]
    lane = lax.broadcasted_iota(jnp.int32, (tile, width), 1)
    qs = []
    for g in range(group):
        q = q_all[:, g * width:(g + 1) * width].astype(F32) * (half ** -0.5 * LOG2_E)
        stacked = jnp.concatenate([jnp.where(lane < half, q, 0.0), jnp.where(lane >= half, q, 0.0)], axis=0)
        qs.append(stacked.astype(BF16))

    acc_scr[...] = jnp.zeros_like(acc_scr)

    def scores(kj, g):
        start = pl.multiple_of(kj * tile, tile)
        return lax.dot_general(k_ref[pl.ds(start, tile), g * width:(g + 1) * width], qs[g],
                               (((1,), (1,)), ((), ())), preferred_element_type=F32)

    def block(tiles, maxes):
        units = [(kj, band_idx, g) for kj, band_idx in tiles for g in range(group)]
        all_scores = []
        for u, (kj, band_idx, g) in enumerate(units):
            if band_idx is None:
                all_scores.append(scores(kj, g))
            else:
                st_scr[u] = scores(kj, g)
                all_scores.append(None)
        maxes = list(maxes)
        for u, (st, (kj, band_idx, g)) in enumerate(zip(all_scores, units)):
            m_prev = maxes[g]
            if band_idx is None:
                shift = far_bias[g]
            else:
                shift = 0.0
                band = band_ref[g, band_idx]
                st = st_scr[u] + jnp.concatenate([band, band], axis=1)
            vt = vt_scr[g, kj]
            m_cols = []
            for c0 in range(0, 2 * tile, ATTN_PV_COLS):
                p_cols, a_cols = [], []
                for c in range(c0, c0 + ATTN_PV_COLS, ATTN_STRIP):
                    cols = slice(c, c + ATTN_STRIP)
                    s_c = st[:, cols]
                    m_c = jnp.maximum(m_prev[:, cols], jnp.max(s_c, axis=0, keepdims=True) + shift)
                    a_cols.append(jnp.exp2(m_prev[:, cols] - m_c))
                    p_cols.append(jnp.exp2(s_c - (m_c - shift)).astype(BF16))
                    m_cols.append(m_c)
                pv = jnp.dot(vt, jnp.concatenate(p_cols, axis=1), preferred_element_type=F32)
                wide = slice(c0, c0 + ATTN_PV_COLS)
                acc_scr[g, :, wide] = jnp.concatenate(a_cols, axis=1) * acc_scr[g, :, wide] + pv
            maxes[g] = jnp.concatenate(m_cols, axis=1)
        return tuple(maxes)

    def far(first, count):
        return tuple((first + i, None) for i in range(count))

    far_bias = tuple(far_ref[head0 + g] * LOG2_E for g in range(group))
    n_far = jnp.maximum(qi - 1, 0)
    n_quads = n_far // 4
    rest = n_far - 4 * n_quads
    maxes = tuple(jnp.full((1, 2 * tile), NEG_BIG, F32) for _ in range(group))
    maxes = lax.fori_loop(0, n_quads, lambda i, c: block(far(4 * i, 4), c), maxes)
    maxes = lax.cond(rest >= 2, lambda: block(far(4 * n_quads, 2), maxes), lambda: maxes)
    maxes = lax.cond(rest % 2 == 1, lambda: block(far(n_far - 1, 1), maxes), lambda: maxes)
    lax.cond(qi > 0, lambda: block(((qi - 1, 0), (qi, 1)), maxes), lambda: block(((qi, 1),), maxes))


    lam_v = lam_ref[...]
    lam = (jnp.exp(jnp.sum(lam_v[0:1, :] * lam_v[1:2, :], axis=1, keepdims=True))
           - jnp.exp(jnp.sum(lam_v[2:3, :] * lam_v[3:4, :], axis=1, keepdims=True)) + lam_init)
    for g in range(group):
        o = acc_scr[g, :width, :] / acc_scr[g, width:width + 1, :]
        out = (o[:, :tile] - lam * o[:, tile:]).T
        gain = gain_ref[:, g * width:(g + 1) * width]
        y_ref[:, g * width:(g + 1) * width] = (_rms(out) * gain * out_scale).astype(BF16)


def _t5_bucket(n):
    max_exact = REL_BUCKETS // 2
    nf = jnp.maximum(n, 1).astype(F32)
    large = max_exact + (jnp.log(nf / max_exact) / math.log(REL_MAX_DIST / max_exact)
                         * (REL_BUCKETS - max_exact)).astype(jnp.int32)
    large = jnp.minimum(large, REL_BUCKETS - 1)
    return jnp.where(n < max_exact, n, large)


def _diff_attn(proj, q_col, k_col, v_col, rel_bias, lam_vecs, head_gain, *, batch, seq):
    n = proj.shape[0]
    heads = rel_bias.shape[1]
    hw = head_gain.shape[0] // heads
    tile = min(ATTN_TILE, seq)
    nq = seq // tile
    assert tile >= REL_MAX_DIST, "distances beyond one tile must all fall in the last bucket"
    group = ATTN_HEAD_GROUP
    gw = group * hw
    assert heads % group == 0 and q_col % group == 0 and k_col % group == 0 and v_col % group == 0

    bucket = _t5_bucket(jnp.arange(2 * tile, dtype=jnp.int32))
    by_dist = jnp.zeros((heads, 2 * tile), F32)
    for b in range(REL_BUCKETS):
        by_dist = jnp.where(bucket[None] == b, rel_bias[b].astype(F32)[:, None], by_dist)
    masked = jnp.full((heads, tile - 1), NEG_BIG, F32)
    line = jnp.concatenate([masked, by_dist * LOG2_E, masked[:, :2]], axis=1)
    length = line.shape[1] - 1
    skew = jnp.tile(line, (1, tile))[:, :tile * length].reshape(heads, tile, length)
    band = jnp.stack([skew[:, :, 2 * tile - 1:3 * tile - 1], skew[:, :, tile - 1:2 * tile - 1]], axis=1)
    far = rel_bias[REL_BUCKETS - 1].astype(F32)

    lam_init = 0.8 - 0.6 * math.exp(-0.3 * LAYER)
    return pl.pallas_call(
        functools.partial(_diff_attn_kernel, out_scale=1.0 - lam_init, lam_init=lam_init),
        grid=(batch, heads // group, nq),
        in_specs=[
            pl.BlockSpec(memory_space=pltpu.SMEM),
            pl.BlockSpec((tile, gw), lambda b, h, i: (b * nq + i, q_col // group + h)),
            pl.BlockSpec((seq, gw), lambda b, h, i: (b, k_col // group + h)),
            pl.BlockSpec((seq, gw), lambda b, h, i: (b, v_col // group + h)),
            pl.BlockSpec((group, 2, tile, tile), lambda b, h, i: (h, 0, 0, 0)),
            _resident(lam_vecs.shape),
            pl.BlockSpec((1, gw), lambda b, h, i: (0, h)),
        ],
        out_specs=pl.BlockSpec((tile, gw), lambda b, h, i: (b * nq + i, h)),
        out_shape=jax.ShapeDtypeStruct((n, heads * hw), BF16),
        scratch_shapes=[
            pltpu.VMEM((group, nq, hw + ATTN_ONES_ROWS, tile), BF16),
            pltpu.VMEM((group, hw + ATTN_ONES_ROWS, 2 * tile), F32),
            pltpu.VMEM((2 * group, tile, 2 * tile), F32),
        ],
        compiler_params=_params("parallel", "parallel", "arbitrary"),
        name="diff_attn",
    )(far, proj, proj, proj, band, lam_vecs, head_gain.reshape(1, -1))


def _mem_kv_kernel(mem_ref, g_ref, w_ref, k_ref, v_ref):
    width = k_ref.shape[-1]
    hb = (_rms(mem_ref[0]) * g_ref[...]).astype(BF16)
    k_ref[0] = jnp.dot(hb, w_ref[:, :width], preferred_element_type=F32).astype(BF16)
    v_ref[0] = jnp.dot(hb, w_ref[:, width:], preferred_element_type=F32).astype(BF16)


def _mem_kv(mem, gain, w_kv):
    batch, mlen, d = mem.shape
    width = w_kv.shape[1] // 2
    out = jax.ShapeDtypeStruct((batch, mlen, width), BF16)
    return pl.pallas_call(
        _mem_kv_kernel,
        grid=(batch,),
        in_specs=[
            pl.BlockSpec((1, mlen, d), lambda b: (b, 0, 0)),
            _resident((1, d)),
            _resident(w_kv.shape),
        ],
        out_specs=[pl.BlockSpec((1, mlen, width), lambda b: (b, 0, 0))] * 2,
        out_shape=[out, out],
        compiler_params=_params("parallel"),
        name="mem_kv",
    )(mem, gain.reshape(1, d), w_kv)


def _merge_kernel(x_ref, qx_ref, g0_ref, g1_ref, g2_ref, ym_ref, ya_ref, kx_ref, vx_ref,
                  bg_ref, wm_ref, wa_ref, wx_ref, wo_ref, o_ref):
    d = x_ref.shape[1]
    xw = qx_ref.shape[1]
    xhd = xw // X_HEADS
    heads = []
    for h in range(X_HEADS):
        cols = slice(h * xhd, (h + 1) * xhd)
        s = lax.dot_general(qx_ref[:, cols], kx_ref[0, :, cols], (((1,), (1,)), ((), ())),
                            preferred_element_type=F32) * (xhd ** -0.5)
        p = jnp.exp(s - jnp.max(s, axis=1, keepdims=True))
        pv = jnp.dot(p.astype(BF16), vx_ref[0, :, cols], preferred_element_type=F32)
        heads.append((pv / jnp.sum(p, axis=1, keepdims=True)).astype(BF16))
    yx = jnp.concatenate(heads, axis=1)

    def gate(g_ref, k):
        return _sigmoid(g_ref[...].astype(F32) + bg_ref[:, k * d:(k + 1) * d])

    merged = (gate(g0_ref, 0) * jnp.dot(ym_ref[...], wm_ref[...], preferred_element_type=F32)
              + gate(g1_ref, 1) * jnp.dot(ya_ref[...], wa_ref[...], preferred_element_type=F32)
              + gate(g2_ref, 2) * jnp.dot(yx, wx_ref[...], preferred_element_type=F32))
    o_ref[...] = x_ref[...] + jnp.dot(merged.astype(BF16), wo_ref[...], preferred_element_type=F32)


def _merge(x, proj, qx_col, g_col, y_m, y_a, kx, vx, b_gate, w_m, w_a, w_x, w_o, *, batch, seq):
    n, d = x.shape
    tm = min(MERGE_ROWS, seq)
    nt = seq // tm

    def row(b, t):
        return b * nt + t

    def proj_spec(col):
        return pl.BlockSpec((tm, d), lambda b, t: (row(b, t), col))

    def mem_spec(a):
        return pl.BlockSpec((1,) + a.shape[1:], lambda b, t: (b, 0, 0))

    return pl.pallas_call(
        _merge_kernel,
        grid=(batch, nt),
        in_specs=[
            pl.BlockSpec((tm, d), lambda b, t: (row(b, t), 0)),
            proj_spec(qx_col), proj_spec(g_col), proj_spec(g_col + 1), proj_spec(g_col + 2),
            pl.BlockSpec((tm, y_m.shape[1]), lambda b, t: (row(b, t), 0)),
            pl.BlockSpec((tm, y_a.shape[1]), lambda b, t: (row(b, t), 0)),
            mem_spec(kx), mem_spec(vx),
            _resident((1, b_gate.shape[0])),
            _resident(w_m.shape), _resident(w_a.shape), _resident(w_x.shape), _resident(w_o.shape),
        ],
        out_specs=pl.BlockSpec((tm, d), lambda b, t: (row(b, t), 0)),
        out_shape=jax.ShapeDtypeStruct((n, d), F32),
        compiler_params=_params("parallel", "parallel"),
        name="merge",
    )(x, proj, proj, proj, proj, y_m, y_a, kx, vx, b_gate.reshape(1, -1), w_m, w_a, w_x, w_o)


def kernel(x, mem, positions, rel_bias, ffn1_norm, ffn1_w_gu, ffn1_w_down, mix_norm, mem_norm, w_in, b_if, b_gate, m_conv_w, m_conv_b, m_wq, m_wk, m_wv, m_head_norm, a_lam_q1, a_lam_k1, a_lam_q2, a_lam_k2, a_head_norm, w_mem_kv, w_branch_m, w_branch_a, w_branch_x, w_out, ffn2_norm, ffn2_w_gu, ffn2_w_down, final_norm):
    del positions
    batch, seq, d = x.shape
    n = batch * seq
    l = LAYER
    m_inner = m_conv_w.shape[-1]
    a_width = a_head_norm.shape[-1]
    x_width = w_mem_kv.shape[-1] // 2
    bf = lambda a: a.astype(BF16)

    gate_lo = 2 * m_inner
    gate_hi = gate_lo + 2 * M_HEADS
    w_all = bf(w_in[l])
    w_tail = w_all[:, gate_hi:]
    w_gate = jnp.pad(w_all[:, gate_lo:gate_hi], ((0, 0), (0, 128 - 2 * M_HEADS)))

    x0 = x.reshape(n, d)
    x1 = _ffn(x0, ffn1_norm[l], bf(ffn1_w_gu[l]), bf(ffn1_w_down[l]), final_norm, final_norm=False)

    proj, gates = _inproj(x1, mix_norm[l], w_all, gate_lo, w_tail, w_gate)

    m_hd = m_inner // M_HEADS
    q_m, k_m, v_m = _mlstm_qkv(proj, m_conv_w[l], m_conv_b[l], bf(m_wq[l]), bf(m_wk[l] * m_hd ** -0.5),
                               bf(m_wv[l]), batch=batch, seq=seq)
    gates_t = gates[:, :2 * M_HEADS].reshape(batch, seq, 2, M_HEADS).transpose(0, 3, 2, 1)
    y_m = _mlstm_scan(q_m, k_m, v_m, gates_t, b_if[l], proj, m_inner // m_hd, m_head_norm[l],
                      batch=batch, seq=seq)

    a_hw = a_width // A_HEADS
    qa_col = 2 * m_inner // a_hw
    lam_vecs = jnp.stack([a_lam_q1[l], a_lam_k1[l], a_lam_q2[l], a_lam_k2[l]])
    y_a = _diff_attn(proj, qa_col, qa_col + A_HEADS, qa_col + 2 * A_HEADS, rel_bias, lam_vecs,
                     a_head_norm[l], batch=batch, seq=seq)

    kx, vx = _mem_kv(mem, mem_norm[l], bf(w_mem_kv[l]))
    qx_col = (2 * m_inner + 3 * a_width) // d
    x2 = _merge(x1, proj, qx_col, qx_col + x_width // d, y_m, y_a, kx, vx, b_gate[l],
                bf(w_branch_m[l]), bf(w_branch_a[l]), bf(w_branch_x[l]), bf(w_out[l]),
                batch=batch, seq=seq)

    out = _ffn(x2, ffn2_norm[l], bf(ffn2_w_gu[l]), bf(ffn2_w_down[l]), final_norm, final_norm=True)
    return out.reshape(batch, seq, d)
```

```python
import functools
import math

import jax
import jax.numpy as jnp
from jax import lax
from jax.experimental import pallas as pl
from jax.experimental.pallas import tpu as pltpu

EPS = 1e-6
BF16 = jnp.bfloat16
F32 = jnp.float32
NEG_BIG = -1e30

M_HEADS = 4
M_CONV = 4
A_HEADS = 8
X_HEADS = 4
N_BRANCH = 3
REL_BUCKETS = 32
REL_MAX_DIST = 128
LAYER = 0

VMEM_LIMIT_BYTES = 56 * 1024 * 1024

FFN_ROWS = 512
INPROJ_ROWS = 2048
INPROJ_COLS = 1024
QKV_ROWS = 2048
QKV_SUBTILES = 8
MLSTM_CHUNK = 256
MLSTM_HEAD_GROUP = 2
ATTN_TILE = 256
ATTN_HEAD_GROUP = 4
ATTN_STRIP = 128
ATTN_PV_COLS = 256
ATTN_ONES_ROWS = 16
ATTN_LOOKAHEAD = 8
LOG2_E = math.log2(math.e)
MERGE_ROWS = 512
CONV_HALO = 16


def _params(*sem, flags=None):
    return pltpu.CompilerParams(dimension_semantics=sem, vmem_limit_bytes=VMEM_LIMIT_BYTES, flags=flags)


def _resident(shape):
    zeros = (0,) * len(shape)
    return pl.BlockSpec(shape, lambda *_: zeros, pipeline_mode=pl.Buffered(1))


def _rms(x):
    return x * lax.rsqrt(jnp.mean(x * x, axis=-1, keepdims=True) + EPS)


def _sigmoid(x):
    return 1.0 / (1.0 + jnp.exp(-x))


def _ffn_kernel(x_ref, g_ref, wgu_ref, wd_ref, fg_ref, o_ref, *, d_ff, final_norm):
    x = x_ref[...]
    hb = (_rms(x) * g_ref[...]).astype(BF16)
    gate = jnp.dot(hb, wgu_ref[:, :d_ff], preferred_element_type=F32)
    up = jnp.dot(hb, wgu_ref[:, d_ff:], preferred_element_type=F32)
    act = (gate * _sigmoid(gate) * up).astype(BF16)
    y = x + 0.5 * jnp.dot(act, wd_ref[...], preferred_element_type=F32)
    if final_norm:
        y = _rms(y) * fg_ref[...]
    o_ref[...] = y


def _ffn(x, gain, w_gu, w_down, final_gain, *, final_norm):
    n, d = x.shape
    d_ff = w_down.shape[0]
    tm = min(FFN_ROWS, n)
    return pl.pallas_call(
        functools.partial(_ffn_kernel, d_ff=d_ff, final_norm=final_norm),
        grid=(n // tm,),
        in_specs=[
            pl.BlockSpec((tm, d), lambda i: (i, 0)),
            _resident((1, d)),
            _resident((d, 2 * d_ff)),
            _resident((d_ff, d)),
            _resident((1, d)),
        ],
        out_specs=pl.BlockSpec((tm, d), lambda i: (i, 0)),
        out_shape=jax.ShapeDtypeStruct((n, d), F32),
        compiler_params=_params("parallel"),
        name="ffn_final" if final_norm else "ffn",
    )(x, gain.reshape(1, d), w_gu, w_down, final_gain.reshape(1, d))


def _inproj_kernel(x_ref, g_ref, w_head_ref, w_tail_ref, wgate_ref, p_ref, gates_ref, h_scr, *, head_blocks):
    j = pl.program_id(1)

    @pl.when(j == 0)
    def _():
        hb = (_rms(x_ref[...]) * g_ref[...]).astype(BF16)
        h_scr[...] = hb
        gates_ref[...] = jnp.dot(hb, wgate_ref[...], preferred_element_type=F32)

    @pl.when(j < head_blocks)
    def _():
        p_ref[...] = jnp.dot(h_scr[...], w_head_ref[...], preferred_element_type=F32).astype(BF16)

    @pl.when(j >= head_blocks)
    def _():
        p_ref[...] = jnp.dot(h_scr[...], w_tail_ref[...], preferred_element_type=F32).astype(BF16)


def _inproj(x, gain, w_all, head_width, w_tail, w_gate):
    n, d = x.shape
    tm = min(INPROJ_ROWS, n)
    tn = INPROJ_COLS
    head_blocks = head_width // tn
    tail_blocks = w_tail.shape[1] // tn
    width = head_width + w_tail.shape[1]
    return pl.pallas_call(
        functools.partial(_inproj_kernel, head_blocks=head_blocks),
        grid=(n // tm, width // tn),
        in_specs=[
            pl.BlockSpec((tm, d), lambda i, j: (i, 0)),
            _resident((1, d)),
            pl.BlockSpec((d, tn), lambda i, j: (0, jnp.minimum(j, head_blocks - 1))),
            pl.BlockSpec((d, tn), lambda i, j: (0, jnp.where(j < head_blocks, tail_blocks - 1, j - head_blocks))),
            _resident(w_gate.shape),
        ],
        out_specs=[
            pl.BlockSpec((tm, tn), lambda i, j: (i, j)),
            pl.BlockSpec((tm, w_gate.shape[1]), lambda i, j: (i, 0)),
        ],
        out_shape=[
            jax.ShapeDtypeStruct((n, width), BF16),
            jax.ShapeDtypeStruct((n, w_gate.shape[1]), F32),
        ],
        scratch_shapes=[pltpu.VMEM((tm, d), BF16)],
        compiler_params=_params("parallel", "arbitrary"),
        name="inproj",
    )(x, gain.reshape(1, d), w_all, w_tail, w_gate)


def _mlstm_qkv_kernel(x_ref, halo_ref, cw_ref, cb_ref, wq_ref, wk_ref, wv_ref,
                      q_ref, k_ref, v_ref):
    ts = x_ref.shape[0]
    xb = x_ref[...]
    v_ref[...] = jnp.dot(xb, wv_ref[0], preferred_element_type=F32).astype(BF16)
    prev = jnp.where(pl.program_id(2) > 0, halo_ref[...].astype(F32), 0.0)
    full = jnp.concatenate([prev, xb.astype(F32)], axis=0)
    cw = cw_ref[...]
    sub = ts // QKV_SUBTILES
    for r0 in range(0, ts, sub):
        conv = cb_ref[...]
        for back in range(M_CONV):
            tap = M_CONV - 1 - back
            start = CONV_HALO + r0 - back
            conv = conv + cw[tap:tap + 1, :] * full[start:start + sub, :]
        xc = (conv * _sigmoid(conv)).astype(BF16)
        q_ref[r0:r0 + sub, :] = jnp.dot(xc, wq_ref[0], preferred_element_type=F32).astype(BF16)
        k_ref[r0:r0 + sub, :] = jnp.dot(xc, wk_ref[0], preferred_element_type=F32).astype(BF16)


def _mlstm_qkv(proj, conv_w, conv_b, wq, wk, wv, *, batch, seq):
    heads, hd, _ = wq.shape
    n = batch * seq
    ts = min(QKV_ROWS, seq)
    nt = seq // ts
    halo_per_tile = ts // CONV_HALO

    def x_map(h, b, t):
        return (b * nt + t, h)

    def halo_map(h, b, t):
        return (jnp.maximum((b * nt + t) * halo_per_tile - 1, 0), h)

    def w_map(h, b, t):
        return (h, 0, 0)

    out = jax.ShapeDtypeStruct((n, heads * hd), BF16)
    return pl.pallas_call(
        _mlstm_qkv_kernel,
        grid=(heads, batch, nt),
        in_specs=[
            pl.BlockSpec((ts, hd), x_map),
            pl.BlockSpec((CONV_HALO, hd), halo_map),
            pl.BlockSpec((M_CONV, hd), lambda h, b, t: (0, h)),
            pl.BlockSpec((1, hd), lambda h, b, t: (0, h)),
            pl.BlockSpec((1, hd, hd), w_map),
            pl.BlockSpec((1, hd, hd), w_map),
            pl.BlockSpec((1, hd, hd), w_map),
        ],
        out_specs=[pl.BlockSpec((ts, hd), x_map)] * 3,
        out_shape=[out, out, out],
        compiler_params=_params("parallel", "parallel", "parallel"),
        name="mlstm_qkv",
    )(proj, proj, conv_w, conv_b.reshape(1, -1), wq, wk, wv)


def _mlstm_scan_kernel(q_ref, k_ref, v_ref, g_ref, bias_ref, o_ref, gain_ref, y_ref,
                       c_scr, n_scr, m_scr):
    @pl.when(pl.program_id(2) == 0)
    def _():
        c_scr[...] = jnp.zeros_like(c_scr)
        n_scr[...] = jnp.zeros_like(n_scr)
        m_scr[...] = jnp.zeros_like(m_scr)

    group, hd, _ = c_scr.shape
    chunk = q_ref.shape[0]
    heads = range(group)
    cols = [slice(g * hd, (g + 1) * hd) for g in heads]
    q = [q_ref[:, cols[g]] for g in heads]
    k = [k_ref[:, cols[g]] for g in heads]
    v = [v_ref[:, cols[g]] for g in heads]
    c_prev = [c_scr[g] for g in heads]
    n_prev = [n_scr[g] for g in heads]
    m_prev = [m_scr[g] for g in heads]

    qk = [lax.dot_general(q[g], k[g], (((1,), (1,)), ((), ())), preferred_element_type=F32) for g in heads]
    qc = [jnp.dot(q[g], c_prev[g].astype(BF16), preferred_element_type=F32) for g in heads]

    t_idx = lax.broadcasted_iota(jnp.int32, (chunk, chunk), 0)
    s_idx = lax.broadcasted_iota(jnp.int32, (chunk, chunk), 1)
    causal = s_idx <= t_idx
    diag = s_idx == t_idx

    def to_col(row):
        return jnp.sum(jnp.where(diag, row, 0.0), axis=1, keepdims=True)

    def to_row(col):
        return jnp.sum(jnp.where(diag, col, 0.0), axis=0, keepdims=True)

    s, w_inter, m_t, w_s, decay, m_new = [], [], [], [], [], []
    for g in heads:
        gates = g_ref[0, g] + bias_ref[g]
        ig = gates[0:1, :]
        fpre = gates[1:2, :]
        lf = jnp.minimum(fpre, 0.0) - jnp.log1p(jnp.exp(-jnp.abs(fpre)))
        b_col = jnp.sum(jnp.where(causal, lf, 0.0), axis=1, keepdims=True)
        b_row = to_row(b_col)
        dlog = jnp.where(causal, b_col - b_row + ig, NEG_BIG)
        inter = b_col + m_prev[g]
        m_t.append(jnp.maximum(inter, jnp.max(dlog, axis=1, keepdims=True)))
        s.append(qk[g] * jnp.exp(dlog - m_t[g]))
        w_inter.append(jnp.exp(inter - m_t[g]))
        b_last = b_col[chunk - 1:chunk, :]
        tail = b_last - b_row + ig
        m_new.append(jnp.maximum(b_last + m_prev[g], jnp.max(tail, axis=1, keepdims=True)))
        decay.append(jnp.exp(b_last + m_prev[g] - m_new[g]))
        w_s.append(jnp.exp(to_col(tail) - m_new[g]))

    sv = [jnp.dot(s[g].astype(BF16), v[g], preferred_element_type=F32) for g in heads]
    kw = [k[g].astype(F32) * w_s[g] for g in heads]
    kv = [jnp.dot(kw[g].T.astype(BF16), v[g], preferred_element_type=F32) for g in heads]

    for g in heads:
        num = sv[g] + w_inter[g] * qc[g]
        qn = jnp.sum(q[g].astype(F32) * n_prev[g], axis=1, keepdims=True)
        den = jnp.sum(s[g], axis=1, keepdims=True) + w_inter[g] * qn
        h = num / jnp.maximum(jnp.abs(den), jnp.exp(-m_t[g]))
        c_scr[g] = decay[g] * c_prev[g] + kv[g]
        n_scr[g] = decay[g] * n_prev[g] + jnp.sum(kw[g], axis=0, keepdims=True)
        m_scr[g] = m_new[g]
        gate = _sigmoid(o_ref[:, cols[g]].astype(F32))
        y_ref[:, cols[g]] = (gate * (_rms(h) * gain_ref[:, cols[g]])).astype(BF16)


def _mlstm_scan(q, k, v, gates_t, b_if, proj, o_col_block, head_gain, *, batch, seq):
    n, width = q.shape
    heads = gates_t.shape[1]
    hd = width // heads
    group = MLSTM_HEAD_GROUP
    gw = group * hd
    assert heads % group == 0 and o_col_block % group == 0
    chunk = min(MLSTM_CHUNK, seq)
    nc = seq // chunk

    def qkv_map(b, h, c):
        return (b * nc + c, h)

    return pl.pallas_call(
        _mlstm_scan_kernel,
        grid=(batch, heads // group, nc),
        in_specs=[
            pl.BlockSpec((chunk, gw), qkv_map),
            pl.BlockSpec((chunk, gw), qkv_map),
            pl.BlockSpec((chunk, gw), qkv_map),
            pl.BlockSpec((1, group, 2, chunk), lambda b, h, c: (b, h, 0, c)),
            pl.BlockSpec((group, 2, 1), lambda b, h, c: (h, 0, 0)),
            pl.BlockSpec((chunk, gw), lambda b, h, c: (b * nc + c, o_col_block // group + h)),
            pl.BlockSpec((1, gw), lambda b, h, c: (0, h)),
        ],
        out_specs=pl.BlockSpec((chunk, gw), qkv_map),
        out_shape=jax.ShapeDtypeStruct((n, width), BF16),
        scratch_shapes=[pltpu.VMEM((group, hd, hd), F32), pltpu.VMEM((group, 1, hd), F32),
                        pltpu.VMEM((group, 1, 1), F32)],
        compiler_params=_params("parallel", "parallel", "arbitrary"),
        name="mlstm_scan",
    )(q, k, v, gates_t, b_if.reshape(2, heads, 1).transpose(1, 0, 2), proj, head_gain.reshape(1, -1))


def _diff_attn_kernel(far_ref, q_ref, k_ref, v_ref, band_ref, lam_ref, gain_ref, y_ref,
                      vt_scr, acc_scr, st_scr, *, out_scale, lam_init):
    group, n_tiles, rows, tile = vt_scr.shape
    width = q_ref.shape[1] // group
    head0 = pl.program_id(1) * group
    qi = pl.program_id(2)
    half = width // 2

    @pl.when(qi == 0)
    def _():
        ones = jnp.ones((rows - width, tile), BF16)
        for g in range(group):
            for j in range(n_tiles):
                v_tile = v_ref[j * tile:(j + 1) * tile, g * width:(g + 1) * width]
                vt_scr[g, j] = jnp.concatenate([v_tile.astype(F32).T.astype(BF16), ones], axis=0)

    q_all = q_ref[...]
    lane = lax.broadcasted_iota(jnp.int32, (tile, width), 1)
    qs = []
    for g in range(group):
        q = q_all[:, g * width:(g + 1) * width].astype(F32) * (half ** -0.5 * LOG2_E)
        stacked = jnp.concatenate([jnp.where(lane < half, q, 0.0), jnp.where(lane >= half, q, 0.0)], axis=0)
        qs.append(stacked.astype(BF16))

    acc_scr[...] = jnp.zeros_like(acc_scr)

    def scores(kj, g):
        start = pl.multiple_of(kj * tile, tile)
        return lax.dot_general(k_ref[pl.ds(start, tile), g * width:(g + 1) * width], qs[g],
                               (((1,), (1,)), ((), ())), preferred_element_type=F32)

    def block(tiles, maxes):
        units = [(kj, band_idx, g) for kj, band_idx in tiles for g in range(group)]
        all_scores, parked = [], 0
        for kj, band_idx, g in units:
            if band_idx is None:
                all_scores.append(scores(kj, g))
            else:
                st_scr[parked] = scores(kj, g)
                all_scores.append(parked)
                parked += 1
        maxes = list(maxes)
        for st, (kj, band_idx, g) in zip(all_scores, units):
            m_prev = maxes[g]
            if band_idx is None:
                shift = far_bias[g]
            else:
                shift = 0.0
                band = band_ref[g, band_idx]
                st = st_scr[st] + jnp.concatenate([band, band], axis=1)
            vt = vt_scr[g, kj]
            m_cols = []
            for c0 in range(0, 2 * tile, ATTN_PV_COLS):
                p_cols, a_cols = [], []
                for c in range(c0, c0 + ATTN_PV_COLS, ATTN_STRIP):
                    cols = slice(c, c + ATTN_STRIP)
                    s_c = st[:, cols]
                    m_c = jnp.maximum(m_prev[:, cols], jnp.max(s_c, axis=0, keepdims=True) + shift)
                    a_cols.append(jnp.exp2(m_prev[:, cols] - m_c))
                    p_cols.append(jnp.exp2(s_c - (m_c - shift)).astype(BF16))
                    m_cols.append(m_c)
                pv = jnp.dot(vt, jnp.concatenate(p_cols, axis=1), preferred_element_type=F32)
                wide = slice(c0, c0 + ATTN_PV_COLS)
                acc_scr[g, :, wide] = jnp.concatenate(a_cols, axis=1) * acc_scr[g, :, wide] + pv
            maxes[g] = jnp.concatenate(m_cols, axis=1)
        return tuple(maxes)

    def far(first, count):
        return tuple((first + i, None) for i in range(count))

    far_bias = tuple(far_ref[head0 + g] * LOG2_E for g in range(group))
    n_far = jnp.maximum(qi - 1, 0)
    n_quads = n_far // 4
    rest = n_far - 4 * n_quads
    maxes = tuple(jnp.full((1, 2 * tile), NEG_BIG, F32) for _ in range(group))
    maxes = lax.fori_loop(0, n_quads, lambda i, c: block(far(4 * i, 4), c), maxes)
    maxes = lax.cond(rest >= 2, lambda: block(far(4 * n_quads, 2), maxes), lambda: maxes)
    near = ((qi - 1, 0), (qi, 1))

    def last_tiles():
        return lax.cond(rest % 2 == 1, lambda: block(far(n_far - 1, 1) + near, maxes), lambda: block(near, maxes))

    lax.cond(qi > 0, last_tiles, lambda: block(((qi, 1),), maxes))

    lam_v = lam_ref[...]
    lam = (jnp.exp(jnp.sum(lam_v[0:1, :] * lam_v[1:2, :], axis=1, keepdims=True))
           - jnp.exp(jnp.sum(lam_v[2:3, :] * lam_v[3:4, :], axis=1, keepdims=True)) + lam_init)
    for g in range(group):
        o = acc_scr[g, :width, :] / acc_scr[g, width:width + 1, :]
        out = (o[:, :tile] - lam * o[:, tile:]).T
        gain = gain_ref[:, g * width:(g + 1) * width]
        y_ref[:, g * width:(g + 1) * width] = (_rms(out) * gain * out_scale).astype(BF16)


def _t5_bucket(n):
    max_exact = REL_BUCKETS // 2
    nf = jnp.maximum(n, 1).astype(F32)
    large = max_exact + (jnp.log(nf / max_exact) / math.log(REL_MAX_DIST / max_exact)
                         * (REL_BUCKETS - max_exact)).astype(jnp.int32)
    large = jnp.minimum(large, REL_BUCKETS - 1)
    return jnp.where(n < max_exact, n, large)


def _diff_attn(proj, q_col, k_col, v_col, rel_bias, lam_vecs, head_gain, *, batch, seq):
    n = proj.shape[0]
    heads = rel_bias.shape[1]
    hw = head_gain.shape[0] // heads
    tile = min(ATTN_TILE, seq)
    nq = seq // tile
    assert tile >= REL_MAX_DIST, "distances beyond one tile must all fall in the last bucket"
    group = ATTN_HEAD_GROUP
    gw = group * hw
    assert heads % group == 0 and q_col % group == 0 and k_col % group == 0 and v_col % group == 0

    bucket = _t5_bucket(jnp.arange(2 * tile, dtype=jnp.int32))
    by_dist = jnp.zeros((heads, 2 * tile), F32)
    for b in range(REL_BUCKETS):
        by_dist = jnp.where(bucket[None] == b, rel_bias[b].astype(F32)[:, None], by_dist)
    masked = jnp.full((heads, tile - 1), NEG_BIG, F32)
    line = jnp.concatenate([masked, by_dist * LOG2_E, masked[:, :2]], axis=1)
    length = line.shape[1] - 1
    skew = jnp.tile(line, (1, tile))[:, :tile * length].reshape(heads, tile, length)
    band = jnp.stack([skew[:, :, 2 * tile - 1:3 * tile - 1], skew[:, :, tile - 1:2 * tile - 1]], axis=1)
    far = rel_bias[REL_BUCKETS - 1].astype(F32)

    lam_init = 0.8 - 0.6 * math.exp(-0.3 * LAYER)
    return pl.pallas_call(
        functools.partial(_diff_attn_kernel, out_scale=1.0 - lam_init, lam_init=lam_init),
        grid=(batch, heads // group, nq),
        in_specs=[
            pl.BlockSpec(memory_space=pltpu.SMEM),
            pl.BlockSpec((tile, gw), lambda b, h, i: (b * nq + i, q_col // group + h)),
            pl.BlockSpec((seq, gw), lambda b, h, i: (b, k_col // group + h)),
            pl.BlockSpec((seq, gw), lambda b, h, i: (b, v_col // group + h)),
            pl.BlockSpec((group, 2, tile, tile), lambda b, h, i: (h, 0, 0, 0)),
            _resident(lam_vecs.shape),
            pl.BlockSpec((1, gw), lambda b, h, i: (0, h)),
        ],
        out_specs=pl.BlockSpec((tile, gw), lambda b, h, i: (b * nq + i, h)),
        out_shape=jax.ShapeDtypeStruct((n, heads * hw), BF16),
        scratch_shapes=[
            pltpu.VMEM((group, nq, hw + ATTN_ONES_ROWS, tile), BF16),
            pltpu.VMEM((group, hw + ATTN_ONES_ROWS, 2 * tile), F32),
            pltpu.VMEM((2 * group, tile, 2 * tile), F32),
        ],
        compiler_params=_params("parallel", "parallel", "arbitrary"),
        name="diff_attn",
    )(far, proj, proj, proj, band, lam_vecs, head_gain.reshape(1, -1))


def _mem_kv_kernel(mem_ref, g_ref, w_ref, k_ref, v_ref):
    width = k_ref.shape[-1]
    hb = (_rms(mem_ref[0]) * g_ref[...]).astype(BF16)
    k_ref[0] = jnp.dot(hb, w_ref[:, :width], preferred_element_type=F32).astype(BF16)
    v_ref[0] = jnp.dot(hb, w_ref[:, width:], preferred_element_type=F32).astype(BF16)


def _mem_kv(mem, gain, w_kv):
    batch, mlen, d = mem.shape
    width = w_kv.shape[1] // 2
    out = jax.ShapeDtypeStruct((batch, mlen, width), BF16)
    return pl.pallas_call(
        _mem_kv_kernel,
        grid=(batch,),
        in_specs=[
            pl.BlockSpec((1, mlen, d), lambda b: (b, 0, 0)),
            _resident((1, d)),
            _resident(w_kv.shape),
        ],
        out_specs=[pl.BlockSpec((1, mlen, width), lambda b: (b, 0, 0))] * 2,
        out_shape=[out, out],
        compiler_params=_params("parallel"),
        name="mem_kv",
    )(mem, gain.reshape(1, d), w_kv)


def _merge_kernel(x_ref, qx_ref, g0_ref, g1_ref, g2_ref, ym_ref, ya_ref, kx_ref, vx_ref,
                  bg_ref, wm_ref, wa_ref, wx_ref, wo_ref, o_ref):
    d = x_ref.shape[1]
    xw = qx_ref.shape[1]
    xhd = xw // X_HEADS
    heads = []
    for h in range(X_HEADS):
        cols = slice(h * xhd, (h + 1) * xhd)
        s = lax.dot_general(qx_ref[:, cols], kx_ref[0, :, cols], (((1,), (1,)), ((), ())),
                            preferred_element_type=F32) * (xhd ** -0.5)
        p = jnp.exp(s - jnp.max(s, axis=1, keepdims=True))
        pv = jnp.dot(p.astype(BF16), vx_ref[0, :, cols], preferred_element_type=F32)
        heads.append((pv / jnp.sum(p, axis=1, keepdims=True)).astype(BF16))
    yx = jnp.concatenate(heads, axis=1)

    def gate(g_ref, k):
        return _sigmoid(g_ref[...].astype(F32) + bg_ref[:, k * d:(k + 1) * d])

    merged = (gate(g0_ref, 0) * jnp.dot(ym_ref[...], wm_ref[...], preferred_element_type=F32)
              + gate(g1_ref, 1) * jnp.dot(ya_ref[...], wa_ref[...], preferred_element_type=F32)
              + gate(g2_ref, 2) * jnp.dot(yx, wx_ref[...], preferred_element_type=F32))
    o_ref[...] = x_ref[...] + jnp.dot(merged.astype(BF16), wo_ref[...], preferred_element_type=F32)


def _merge(x, proj, qx_col, g_col, y_m, y_a, kx, vx, b_gate, w_m, w_a, w_x, w_o, *, batch, seq):
    n, d = x.shape
    tm = min(MERGE_ROWS, seq)
    nt = seq // tm

    def row(b, t):
        return b * nt + t

    def proj_spec(col):
        return pl.BlockSpec((tm, d), lambda b, t: (row(b, t), col))

    def mem_spec(a):
        return pl.BlockSpec((1,) + a.shape[1:], lambda b, t: (b, 0, 0))

    return pl.pallas_call(
        _merge_kernel,
        grid=(batch, nt),
        in_specs=[
            pl.BlockSpec((tm, d), lambda b, t: (row(b, t), 0)),
            proj_spec(qx_col), proj_spec(g_col), proj_spec(g_col + 1), proj_spec(g_col + 2),
            pl.BlockSpec((tm, y_m.shape[1]), lambda b, t: (row(b, t), 0)),
            pl.BlockSpec((tm, y_a.shape[1]), lambda b, t: (row(b, t), 0)),
            mem_spec(kx), mem_spec(vx),
            _resident((1, b_gate.shape[0])),
            _resident(w_m.shape), _resident(w_a.shape), _resident(w_x.shape), _resident(w_o.shape),
        ],
        out_specs=pl.BlockSpec((tm, d), lambda b, t: (row(b, t), 0)),
        out_shape=jax.ShapeDtypeStruct((n, d), F32),
        compiler_params=_params("parallel", "parallel"),
        name="merge",
    )(x, proj, proj, proj, proj, y_m, y_a, kx, vx, b_gate.reshape(1, -1), w_m, w_a, w_x, w_o)


def kernel(x, mem, positions, rel_bias, ffn1_norm, ffn1_w_gu, ffn1_w_down, mix_norm, mem_norm, w_in, b_if, b_gate, m_conv_w, m_conv_b, m_wq, m_wk, m_wv, m_head_norm, a_lam_q1, a_lam_k1, a_lam_q2, a_lam_k2, a_head_norm, w_mem_kv, w_branch_m, w_branch_a, w_branch_x, w_out, ffn2_norm, ffn2_w_gu, ffn2_w_down, final_norm):
    del positions
    batch, seq, d = x.shape
    n = batch * seq
    l = LAYER
    m_inner = m_conv_w.shape[-1]
    a_width = a_head_norm.shape[-1]
    x_width = w_mem_kv.shape[-1] // 2
    bf = lambda a: a.astype(BF16)

    gate_lo = 2 * m_inner
    gate_hi = gate_lo + 2 * M_HEADS
    w_all = bf(w_in[l])
    w_tail = w_all[:, gate_hi:]
    w_gate = jnp.pad(w_all[:, gate_lo:gate_hi], ((0, 0), (0, 128 - 2 * M_HEADS)))

    x0 = x.reshape(n, d)
    x1 = _ffn(x0, ffn1_norm[l], bf(ffn1_w_gu[l]), bf(ffn1_w_down[l]), final_norm, final_norm=False)

    proj, gates = _inproj(x1, mix_norm[l], w_all, gate_lo, w_tail, w_gate)

    m_hd = m_inner // M_HEADS
    q_m, k_m, v_m = _mlstm_qkv(proj, m_conv_w[l], m_conv_b[l], bf(m_wq[l]), bf(m_wk[l] * m_hd ** -0.5),
                               bf(m_wv[l]), batch=batch, seq=seq)
    gates_t = gates[:, :2 * M_HEADS].reshape(batch, seq, 2, M_HEADS).transpose(0, 3, 2, 1)
    y_m = _mlstm_scan(q_m, k_m, v_m, gates_t, b_if[l], proj, m_inner // m_hd, m_head_norm[l],
                      batch=batch, seq=seq)

    a_hw = a_width // A_HEADS
    qa_col = 2 * m_inner // a_hw
    lam_vecs = jnp.stack([a_lam_q1[l], a_lam_k1[l], a_lam_q2[l], a_lam_k2[l]])
    y_a = _diff_attn(proj, qa_col, qa_col + A_HEADS, qa_col + 2 * A_HEADS, rel_bias, lam_vecs,
                     a_head_norm[l], batch=batch, seq=seq)

    kx, vx = _mem_kv(mem, mem_norm[l], bf(w_mem_kv[l]))
    qx_col = (2 * m_inner + 3 * a_width) // d
    x2 = _merge(x1, proj, qx_col, qx_col + x_width // d, y_m, y_a, kx, vx, b_gate[l],
                bf(w_branch_m[l]), bf(w_branch_a[l]), bf(w_branch_x[l]), bf(w_out[l]),
                batch=batch, seq=seq)

    out = _ffn(x2, ffn2_norm[l], bf(ffn2_w_gu[l]), bf(ffn2_w_down[l]), final_norm, final_norm=True)
    return out.reshape(batch, seq, d)
```

```python
import functools
import math

import jax
import jax.numpy as jnp
from jax import lax
from jax.experimental import pallas as pl
from jax.experimental.pallas import tpu as pltpu

EPS = 1e-6
BF16 = jnp.bfloat16
F32 = jnp.float32
NEG_BIG = -1e30

M_HEADS = 4
M_CONV = 4
A_HEADS = 8
X_HEADS = 4
N_BRANCH = 3
REL_BUCKETS = 32
REL_MAX_DIST = 128
LAYER = 0

VMEM_LIMIT_BYTES = 56 * 1024 * 1024

FFN_ROWS = 512
INPROJ_ROWS = 2048
INPROJ_COLS = 1024
QKV_ROWS = 2048
QKV_SUBTILES = 8
MLSTM_CHUNK = 256
MLSTM_HEAD_GROUP = 4
ATTN_TILE = 256
ATTN_HEAD_GROUP = 4
ATTN_STRIP = 128
ATTN_PV_COLS = 256
ATTN_ONES_ROWS = 16
ATTN_LOOKAHEAD = 8
LOG2_E = math.log2(math.e)
MERGE_ROWS = 512
CONV_HALO = 16


def _params(*sem, flags=None):
    return pltpu.CompilerParams(dimension_semantics=sem, vmem_limit_bytes=VMEM_LIMIT_BYTES, flags=flags)


def _resident(shape):
    zeros = (0,) * len(shape)
    return pl.BlockSpec(shape, lambda *_: zeros, pipeline_mode=pl.Buffered(1))


def _rms(x):
    return x * lax.rsqrt(jnp.mean(x * x, axis=-1, keepdims=True) + EPS)


def _sigmoid(x):
    return 1.0 / (1.0 + jnp.exp(-x))


def _ffn_kernel(x_ref, g_ref, wgu_ref, wd_ref, fg_ref, o_ref, *, d_ff, final_norm):
    x = x_ref[...]
    hb = (_rms(x) * g_ref[...]).astype(BF16)
    gate = jnp.dot(hb, wgu_ref[:, :d_ff], preferred_element_type=F32)
    up = jnp.dot(hb, wgu_ref[:, d_ff:], preferred_element_type=F32)
    act = (gate * _sigmoid(gate) * up).astype(BF16)
    y = x + 0.5 * jnp.dot(act, wd_ref[...], preferred_element_type=F32)
    if final_norm:
        y = _rms(y) * fg_ref[...]
    o_ref[...] = y


def _ffn(x, gain, w_gu, w_down, final_gain, *, final_norm):
    n, d = x.shape
    d_ff = w_down.shape[0]
    tm = min(FFN_ROWS, n)
    return pl.pallas_call(
        functools.partial(_ffn_kernel, d_ff=d_ff, final_norm=final_norm),
        grid=(n // tm,),
        in_specs=[
            pl.BlockSpec((tm, d), lambda i: (i, 0)),
            _resident((1, d)),
            _resident((d, 2 * d_ff)),
            _resident((d_ff, d)),
            _resident((1, d)),
        ],
        out_specs=pl.BlockSpec((tm, d), lambda i: (i, 0)),
        out_shape=jax.ShapeDtypeStruct((n, d), F32),
        compiler_params=_params("parallel"),
        name="ffn_final" if final_norm else "ffn",
    )(x, gain.reshape(1, d), w_gu, w_down, final_gain.reshape(1, d))


def _inproj_kernel(x_ref, g_ref, w_head_ref, w_tail_ref, wgate_ref, p_ref, gates_ref, h_scr, *, head_blocks):
    j = pl.program_id(1)

    @pl.when(j == 0)
    def _():
        hb = (_rms(x_ref[...]) * g_ref[...]).astype(BF16)
        h_scr[...] = hb
        gates_ref[...] = jnp.dot(hb, wgate_ref[...], preferred_element_type=F32)

    @pl.when(j < head_blocks)
    def _():
        p_ref[...] = jnp.dot(h_scr[...], w_head_ref[...], preferred_element_type=F32).astype(BF16)

    @pl.when(j >= head_blocks)
    def _():
        p_ref[...] = jnp.dot(h_scr[...], w_tail_ref[...], preferred_element_type=F32).astype(BF16)


def _inproj(x, gain, w_all, head_width, w_tail, w_gate):
    n, d = x.shape
    tm = min(INPROJ_ROWS, n)
    tn = INPROJ_COLS
    head_blocks = head_width // tn
    tail_blocks = w_tail.shape[1] // tn
    width = head_width + w_tail.shape[1]
    return pl.pallas_call(
        functools.partial(_inproj_kernel, head_blocks=head_blocks),
        grid=(n // tm, width // tn),
        in_specs=[
            pl.BlockSpec((tm, d), lambda i, j: (i, 0)),
            _resident((1, d)),
            pl.BlockSpec((d, tn), lambda i, j: (0, jnp.minimum(j, head_blocks - 1))),
            pl.BlockSpec((d, tn), lambda i, j: (0, jnp.where(j < head_blocks, tail_blocks - 1, j - head_blocks))),
            _resident(w_gate.shape),
        ],
        out_specs=[
            pl.BlockSpec((tm, tn), lambda i, j: (i, j)),
            pl.BlockSpec((tm, w_gate.shape[1]), lambda i, j: (i, 0)),
        ],
        out_shape=[
            jax.ShapeDtypeStruct((n, width), BF16),
            jax.ShapeDtypeStruct((n, w_gate.shape[1]), F32),
        ],
        scratch_shapes=[pltpu.VMEM((tm, d), BF16)],
        compiler_params=_params("parallel", "arbitrary"),
        name="inproj",
    )(x, gain.reshape(1, d), w_all, w_tail, w_gate)


def _mlstm_qkv_kernel(x_ref, halo_ref, cw_ref, cb_ref, wq_ref, wk_ref, wv_ref,
                      q_ref, k_ref, v_ref):
    ts = x_ref.shape[0]
    xb = x_ref[...]
    v_ref[...] = jnp.dot(xb, wv_ref[0], preferred_element_type=F32).astype(BF16)
    prev = jnp.where(pl.program_id(2) > 0, halo_ref[...].astype(F32), 0.0)
    full = jnp.concatenate([prev, xb.astype(F32)], axis=0)
    cw = cw_ref[...]
    sub = ts // QKV_SUBTILES
    for r0 in range(0, ts, sub):
        conv = cb_ref[...]
        for back in range(M_CONV):
            tap = M_CONV - 1 - back
            start = CONV_HALO + r0 - back
            conv = conv + cw[tap:tap + 1, :] * full[start:start + sub, :]
        xc = (conv * _sigmoid(conv)).astype(BF16)
        q_ref[r0:r0 + sub, :] = jnp.dot(xc, wq_ref[0], preferred_element_type=F32).astype(BF16)
        k_ref[r0:r0 + sub, :] = jnp.dot(xc, wk_ref[0], preferred_element_type=F32).astype(BF16)


def _mlstm_qkv(proj, conv_w, conv_b, wq, wk, wv, *, batch, seq):
    heads, hd, _ = wq.shape
    n = batch * seq
    ts = min(QKV_ROWS, seq)
    nt = seq // ts
    halo_per_tile = ts // CONV_HALO

    def x_map(h, b, t):
        return (b * nt + t, h)

    def halo_map(h, b, t):
        return (jnp.maximum((b * nt + t) * halo_per_tile - 1, 0), h)

    def w_map(h, b, t):
        return (h, 0, 0)

    out = jax.ShapeDtypeStruct((n, heads * hd), BF16)
    return pl.pallas_call(
        _mlstm_qkv_kernel,
        grid=(heads, batch, nt),
        in_specs=[
            pl.BlockSpec((ts, hd), x_map),
            pl.BlockSpec((CONV_HALO, hd), halo_map),
            pl.BlockSpec((M_CONV, hd), lambda h, b, t: (0, h)),
            pl.BlockSpec((1, hd), lambda h, b, t: (0, h)),
            pl.BlockSpec((1, hd, hd), w_map),
            pl.BlockSpec((1, hd, hd), w_map),
            pl.BlockSpec((1, hd, hd), w_map),
        ],
        out_specs=[pl.BlockSpec((ts, hd), x_map)] * 3,
        out_shape=[out, out, out],
        compiler_params=_params("parallel", "parallel", "parallel"),
        name="mlstm_qkv",
    )(proj, proj, conv_w, conv_b.reshape(1, -1), wq, wk, wv)


def _mlstm_scan_kernel(q_ref, k_ref, v_ref, g_ref, bias_ref, o_ref, gain_ref, y_ref,
                       c_scr, n_scr, m_scr):
    @pl.when(pl.program_id(2) == 0)
    def _():
        c_scr[...] = jnp.zeros_like(c_scr)
        n_scr[...] = jnp.zeros_like(n_scr)
        m_scr[...] = jnp.zeros_like(m_scr)

    group, hd, _ = c_scr.shape
    chunk = q_ref.shape[0]
    heads = range(group)
    cols = [slice(g * hd, (g + 1) * hd) for g in heads]
    q = [q_ref[:, cols[g]] for g in heads]
    k = [k_ref[:, cols[g]] for g in heads]
    v = [v_ref[:, cols[g]] for g in heads]
    c_prev = [c_scr[g] for g in heads]
    n_prev = [n_scr[g] for g in heads]
    m_prev = [m_scr[g] for g in heads]

    qk = [lax.dot_general(q[g], k[g], (((1,), (1,)), ((), ())), preferred_element_type=F32) for g in heads]
    qc = [jnp.dot(q[g], c_prev[g].astype(BF16), preferred_element_type=F32) for g in heads]

    t_idx = lax.broadcasted_iota(jnp.int32, (chunk, chunk), 0)
    s_idx = lax.broadcasted_iota(jnp.int32, (chunk, chunk), 1)
    causal = s_idx <= t_idx
    diag = s_idx == t_idx

    def to_col(row):
        return jnp.sum(jnp.where(diag, row, 0.0), axis=1, keepdims=True)

    def to_row(col):
        return jnp.sum(jnp.where(diag, col, 0.0), axis=0, keepdims=True)

    s, w_inter, m_t, w_s, decay, m_new = [], [], [], [], [], []
    for g in heads:
        gates = g_ref[0, g] + bias_ref[g]
        ig = gates[0:1, :]
        fpre = gates[1:2, :]
        lf = jnp.minimum(fpre, 0.0) - jnp.log1p(jnp.exp(-jnp.abs(fpre)))
        b_col = jnp.sum(jnp.where(causal, lf, 0.0), axis=1, keepdims=True)
        b_row = to_row(b_col)
        dlog = jnp.where(causal, b_col - b_row + ig, NEG_BIG)
        inter = b_col + m_prev[g]
        m_t.append(jnp.maximum(inter, jnp.max(dlog, axis=1, keepdims=True)))
        s.append(qk[g] * jnp.exp(dlog - m_t[g]))
        w_inter.append(jnp.exp(inter - m_t[g]))
        b_last = b_col[chunk - 1:chunk, :]
        tail = b_last - b_row + ig
        m_new.append(jnp.maximum(b_last + m_prev[g], jnp.max(tail, axis=1, keepdims=True)))
        decay.append(jnp.exp(b_last + m_prev[g] - m_new[g]))
        w_s.append(jnp.exp(to_col(tail) - m_new[g]))

    sv = [jnp.dot(s[g].astype(BF16), v[g], preferred_element_type=F32) for g in heads]
    kw = [k[g].astype(F32) * w_s[g] for g in heads]
    kv = [jnp.dot(kw[g].T.astype(BF16), v[g], preferred_element_type=F32) for g in heads]

    for g in heads:
        num = sv[g] + w_inter[g] * qc[g]
        qn = jnp.sum(q[g].astype(F32) * n_prev[g], axis=1, keepdims=True)
        den = jnp.sum(s[g], axis=1, keepdims=True) + w_inter[g] * qn
        h = num / jnp.maximum(jnp.abs(den), jnp.exp(-m_t[g]))
        c_scr[g] = decay[g] * c_prev[g] + kv[g]
        n_scr[g] = decay[g] * n_prev[g] + jnp.sum(kw[g], axis=0, keepdims=True)
        m_scr[g] = m_new[g]
        gate = _sigmoid(o_ref[:, cols[g]].astype(F32))
        y_ref[:, cols[g]] = (gate * (_rms(h) * gain_ref[:, cols[g]])).astype(BF16)


def _mlstm_scan(q, k, v, gates_t, b_if, proj, o_col_block, head_gain, *, batch, seq):
    n, width = q.shape
    heads = gates_t.shape[1]
    hd = width // heads
    group = MLSTM_HEAD_GROUP
    gw = group * hd
    assert heads % group == 0 and o_col_block % group == 0
    chunk = min(MLSTM_CHUNK, seq)
    nc = seq // chunk

    def qkv_map(b, h, c):
        return (b * nc + c, h)

    return pl.pallas_call(
        _mlstm_scan_kernel,
        grid=(batch, heads // group, nc),
        in_specs=[
            pl.BlockSpec((chunk, gw), qkv_map),
            pl.BlockSpec((chunk, gw), qkv_map),
            pl.BlockSpec((chunk, gw), qkv_map),
            pl.BlockSpec((1, group, 2, chunk), lambda b, h, c: (b, h, 0, c)),
            pl.BlockSpec((group, 2, 1), lambda b, h, c: (h, 0, 0)),
            pl.BlockSpec((chunk, gw), lambda b, h, c: (b * nc + c, o_col_block // group + h)),
            pl.BlockSpec((1, gw), lambda b, h, c: (0, h)),
        ],
        out_specs=pl.BlockSpec((chunk, gw), qkv_map),
        out_shape=jax.ShapeDtypeStruct((n, width), BF16),
        scratch_shapes=[pltpu.VMEM((group, hd, hd), F32), pltpu.VMEM((group, 1, hd), F32),
                        pltpu.VMEM((group, 1, 1), F32)],
        compiler_params=_params("parallel", "parallel", "arbitrary"),
        name="mlstm_scan",
    )(q, k, v, gates_t, b_if.reshape(2, heads, 1).transpose(1, 0, 2), proj, head_gain.reshape(1, -1))


def _diff_attn_query_tile(qi, far_ref, q_ref, k_ref, band_ref, lam_ref, gain_ref, y_ref,
                          vt_scr, acc_scr, st_scr, *, out_scale, lam_init):
    group, _, _, tile = vt_scr.shape
    width = q_ref.shape[1] // group
    head0 = pl.program_id(1) * group
    half = width // 2
    q_start = pl.multiple_of(qi * tile, tile)

    q_all = q_ref[pl.ds(q_start, tile), :]
    lane = lax.broadcasted_iota(jnp.int32, (tile, width), 1)
    qs = []
    for g in range(group):
        q = q_all[:, g * width:(g + 1) * width].astype(F32) * (half ** -0.5 * LOG2_E)
        stacked = jnp.concatenate([jnp.where(lane < half, q, 0.0), jnp.where(lane >= half, q, 0.0)], axis=0)
        qs.append(stacked.astype(BF16))

    acc_scr[...] = jnp.zeros_like(acc_scr)

    def scores(kj, g):
        start = pl.multiple_of(kj * tile, tile)
        return lax.dot_general(k_ref[pl.ds(start, tile), g * width:(g + 1) * width], qs[g],
                               (((1,), (1,)), ((), ())), preferred_element_type=F32)

    def block(tiles, maxes):
        units = [(kj, band_idx, g) for kj, band_idx in tiles for g in range(group)]
        all_scores, parked = [], 0
        for kj, band_idx, g in units:
            if band_idx is None:
                all_scores.append(scores(kj, g))
            else:
                st_scr[parked] = scores(kj, g)
                all_scores.append(parked)
                parked += 1
        maxes = list(maxes)
        for st, (kj, band_idx, g) in zip(all_scores, units):
            m_prev = maxes[g]
            if band_idx is None:
                shift = far_bias[g]
            else:
                shift = 0.0
                band = band_ref[g, band_idx]
                st = st_scr[st] + jnp.concatenate([band, band], axis=1)
            vt = vt_scr[g, kj]
            m_cols = []
            for c0 in range(0, 2 * tile, ATTN_PV_COLS):
                p_cols, a_cols = [], []
                for c in range(c0, c0 + ATTN_PV_COLS, ATTN_STRIP):
                    cols = slice(c, c + ATTN_STRIP)
                    s_c = st[:, cols]
                    m_c = jnp.maximum(m_prev[:, cols], jnp.max(s_c, axis=0, keepdims=True) + shift)
                    a_cols.append(jnp.exp2(m_prev[:, cols] - m_c))
                    p_cols.append(jnp.exp2(s_c - (m_c - shift)).astype(BF16))
                    m_cols.append(m_c)
                pv = jnp.dot(vt, jnp.concatenate(p_cols, axis=1), preferred_element_type=F32)
                wide = slice(c0, c0 + ATTN_PV_COLS)
                acc_scr[g, :, wide] = jnp.concatenate(a_cols, axis=1) * acc_scr[g, :, wide] + pv
            maxes[g] = jnp.concatenate(m_cols, axis=1)
        return tuple(maxes)

    def far(first, count):
        return tuple((first + i, None) for i in range(count))

    far_bias = tuple(far_ref[head0 + g] * LOG2_E for g in range(group))
    n_far = jnp.maximum(qi - 1, 0)
    n_quads = n_far // 4
    rest = n_far - 4 * n_quads
    maxes = tuple(jnp.full((1, 2 * tile), NEG_BIG, F32) for _ in range(group))
    maxes = lax.fori_loop(0, n_quads, lambda i, c: block(far(4 * i, 4), c), maxes)
    maxes = lax.cond(rest >= 2, lambda: block(far(4 * n_quads, 2), maxes), lambda: maxes)
    near = ((qi - 1, 0), (qi, 1))

    def last_tiles():
        return lax.cond(rest % 2 == 1, lambda: block(far(n_far - 1, 1) + near, maxes), lambda: block(near, maxes))

    lax.cond(qi > 0, last_tiles, lambda: block(((qi, 1),), maxes))

    lam_v = lam_ref[...]
    lam = (jnp.exp(jnp.sum(lam_v[0:1, :] * lam_v[1:2, :], axis=1, keepdims=True))
           - jnp.exp(jnp.sum(lam_v[2:3, :] * lam_v[3:4, :], axis=1, keepdims=True)) + lam_init)
    for g in range(group):
        o = acc_scr[g, :width, :] / acc_scr[g, width:width + 1, :]
        out = (o[:, :tile] - lam * o[:, tile:]).T
        gain = gain_ref[:, g * width:(g + 1) * width]
        y_ref[pl.ds(q_start, tile), g * width:(g + 1) * width] = (_rms(out) * gain * out_scale).astype(BF16)


def _diff_attn_kernel(far_ref, q_ref, k_ref, v_ref, band_ref, lam_ref, gain_ref, y_ref,
                      vt_scr, acc_scr, st_scr, **consts):
    group, n_tiles, rows, tile = vt_scr.shape
    width = v_ref.shape[1] // group
    ones = jnp.ones((rows - width, tile), BF16)
    for g in range(group):
        for j in range(n_tiles):
            v_tile = v_ref[j * tile:(j + 1) * tile, g * width:(g + 1) * width]
            vt_scr[g, j] = jnp.concatenate([v_tile.astype(F32).T.astype(BF16), ones], axis=0)

    def query_tile(qi, carry):
        _diff_attn_query_tile(qi, far_ref, q_ref, k_ref, band_ref, lam_ref, gain_ref, y_ref,
                              vt_scr, acc_scr, st_scr, **consts)
        return carry

    lax.fori_loop(0, n_tiles, query_tile, 0)


def _t5_bucket(n):
    max_exact = REL_BUCKETS // 2
    nf = jnp.maximum(n, 1).astype(F32)
    large = max_exact + (jnp.log(nf / max_exact) / math.log(REL_MAX_DIST / max_exact)
                         * (REL_BUCKETS - max_exact)).astype(jnp.int32)
    large = jnp.minimum(large, REL_BUCKETS - 1)
    return jnp.where(n < max_exact, n, large)


def _diff_attn(proj, q_col, k_col, v_col, rel_bias, lam_vecs, head_gain, *, batch, seq):
    n = proj.shape[0]
    heads = rel_bias.shape[1]
    hw = head_gain.shape[0] // heads
    tile = min(ATTN_TILE, seq)
    nq = seq // tile
    assert tile >= REL_MAX_DIST, "distances beyond one tile must all fall in the last bucket"
    group = ATTN_HEAD_GROUP
    gw = group * hw
    assert heads % group == 0 and q_col % group == 0 and k_col % group == 0 and v_col % group == 0

    bucket = _t5_bucket(jnp.arange(2 * tile, dtype=jnp.int32))
    by_dist = jnp.zeros((heads, 2 * tile), F32)
    for b in range(REL_BUCKETS):
        by_dist = jnp.where(bucket[None] == b, rel_bias[b].astype(F32)[:, None], by_dist)
    masked = jnp.full((heads, tile - 1), NEG_BIG, F32)
    line = jnp.concatenate([masked, by_dist * LOG2_E, masked[:, :2]], axis=1)
    length = line.shape[1] - 1
    skew = jnp.tile(line, (1, tile))[:, :tile * length].reshape(heads, tile, length)
    band = jnp.stack([skew[:, :, 2 * tile - 1:3 * tile - 1], skew[:, :, tile - 1:2 * tile - 1]], axis=1)
    far = rel_bias[REL_BUCKETS - 1].astype(F32)

    lam_init = 0.8 - 0.6 * math.exp(-0.3 * LAYER)
    return pl.pallas_call(
        functools.partial(_diff_attn_kernel, out_scale=1.0 - lam_init, lam_init=lam_init),
        grid=(batch, heads // group),
        in_specs=[
            pl.BlockSpec(memory_space=pltpu.SMEM),
            pl.BlockSpec((seq, gw), lambda b, h: (b, q_col // group + h)),
            pl.BlockSpec((seq, gw), lambda b, h: (b, k_col // group + h)),
            pl.BlockSpec((seq, gw), lambda b, h: (b, v_col // group + h)),
            pl.BlockSpec((group, 2, tile, tile), lambda b, h: (h, 0, 0, 0)),
            _resident(lam_vecs.shape),
            pl.BlockSpec((1, gw), lambda b, h: (0, h)),
        ],
        out_specs=pl.BlockSpec((seq, gw), lambda b, h: (b, h)),
        out_shape=jax.ShapeDtypeStruct((n, heads * hw), BF16),
        scratch_shapes=[
            pltpu.VMEM((group, nq, hw + ATTN_ONES_ROWS, tile), BF16),
            pltpu.VMEM((group, hw + ATTN_ONES_ROWS, 2 * tile), F32),
            pltpu.VMEM((2 * group, tile, 2 * tile), F32),
        ],
        compiler_params=_params("parallel", "parallel"),
        name="diff_attn",
    )(far, proj, proj, proj, band, lam_vecs, head_gain.reshape(1, -1))


def _mem_kv_kernel(mem_ref, g_ref, w_ref, k_ref, v_ref):
    width = k_ref.shape[-1]
    hb = (_rms(mem_ref[0]) * g_ref[...]).astype(BF16)
    k_ref[0] = jnp.dot(hb, w_ref[:, :width], preferred_element_type=F32).astype(BF16)
    v_ref[0] = jnp.dot(hb, w_ref[:, width:], preferred_element_type=F32).astype(BF16)


def _mem_kv(mem, gain, w_kv):
    batch, mlen, d = mem.shape
    width = w_kv.shape[1] // 2
    out = jax.ShapeDtypeStruct((batch, mlen, width), BF16)
    return pl.pallas_call(
        _mem_kv_kernel,
        grid=(batch,),
        in_specs=[
            pl.BlockSpec((1, mlen, d), lambda b: (b, 0, 0)),
            _resident((1, d)),
            _resident(w_kv.shape),
        ],
        out_specs=[pl.BlockSpec((1, mlen, width), lambda b: (b, 0, 0))] * 2,
        out_shape=[out, out],
        compiler_params=_params("parallel"),
        name="mem_kv",
    )(mem, gain.reshape(1, d), w_kv)


def _merge_kernel(x_ref, qx_ref, g0_ref, g1_ref, g2_ref, ym_ref, ya_ref, kx_ref, vx_ref,
                  bg_ref, wm_ref, wa_ref, wx_ref, wo_ref, o_ref):
    d = x_ref.shape[1]
    xw = qx_ref.shape[1]
    xhd = xw // X_HEADS
    heads = []
    for h in range(X_HEADS):
        cols = slice(h * xhd, (h + 1) * xhd)
        s = lax.dot_general(qx_ref[:, cols], kx_ref[0, :, cols], (((1,), (1,)), ((), ())),
                            preferred_element_type=F32) * (xhd ** -0.5)
        p = jnp.exp(s - jnp.max(s, axis=1, keepdims=True))
        pv = jnp.dot(p.astype(BF16), vx_ref[0, :, cols], preferred_element_type=F32)
        heads.append((pv / jnp.sum(p, axis=1, keepdims=True)).astype(BF16))
    yx = jnp.concatenate(heads, axis=1)

    def gate(g_ref, k):
        return _sigmoid(g_ref[...].astype(F32) + bg_ref[:, k * d:(k + 1) * d])

    merged = (gate(g0_ref, 0) * jnp.dot(ym_ref[...], wm_ref[...], preferred_element_type=F32)
              + gate(g1_ref, 1) * jnp.dot(ya_ref[...], wa_ref[...], preferred_element_type=F32)
              + gate(g2_ref, 2) * jnp.dot(yx, wx_ref[...], preferred_element_type=F32))
    o_ref[...] = x_ref[...] + jnp.dot(merged.astype(BF16), wo_ref[...], preferred_element_type=F32)


def _merge(x, proj, qx_col, g_col, y_m, y_a, kx, vx, b_gate, w_m, w_a, w_x, w_o, *, batch, seq):
    n, d = x.shape
    tm = min(MERGE_ROWS, seq)
    nt = seq // tm

    def row(b, t):
        return b * nt + t

    def proj_spec(col):
        return pl.BlockSpec((tm, d), lambda b, t: (row(b, t), col))

    def mem_spec(a):
        return pl.BlockSpec((1,) + a.shape[1:], lambda b, t: (b, 0, 0))

    return pl.pallas_call(
        _merge_kernel,
        grid=(batch, nt),
        in_specs=[
            pl.BlockSpec((tm, d), lambda b, t: (row(b, t), 0)),
            proj_spec(qx_col), proj_spec(g_col), proj_spec(g_col + 1), proj_spec(g_col + 2),
            pl.BlockSpec((tm, y_m.shape[1]), lambda b, t: (row(b, t), 0)),
            pl.BlockSpec((tm, y_a.shape[1]), lambda b, t: (row(b, t), 0)),
            mem_spec(kx), mem_spec(vx),
            _resident((1, b_gate.shape[0])),
            _resident(w_m.shape), _resident(w_a.shape), _resident(w_x.shape), _resident(w_o.shape),
        ],
        out_specs=pl.BlockSpec((tm, d), lambda b, t: (row(b, t), 0)),
        out_shape=jax.ShapeDtypeStruct((n, d), F32),
        compiler_params=_params("parallel", "parallel"),
        name="merge",
    )(x, proj, proj, proj, proj, y_m, y_a, kx, vx, b_gate.reshape(1, -1), w_m, w_a, w_x, w_o)


def kernel(x, mem, positions, rel_bias, ffn1_norm, ffn1_w_gu, ffn1_w_down, mix_norm, mem_norm, w_in, b_if, b_gate, m_conv_w, m_conv_b, m_wq, m_wk, m_wv, m_head_norm, a_lam_q1, a_lam_k1, a_lam_q2, a_lam_k2, a_head_norm, w_mem_kv, w_branch_m, w_branch_a, w_branch_x, w_out, ffn2_norm, ffn2_w_gu, ffn2_w_down, final_norm):
    del positions
    batch, seq, d = x.shape
    n = batch * seq
    l = LAYER
    m_inner = m_conv_w.shape[-1]
    a_width = a_head_norm.shape[-1]
    x_width = w_mem_kv.shape[-1] // 2
    bf = lambda a: a.astype(BF16)

    gate_lo = 2 * m_inner
    gate_hi = gate_lo + 2 * M_HEADS
    w_all = bf(w_in[l])
    w_tail = w_all[:, gate_hi:]
    w_gate = jnp.pad(w_all[:, gate_lo:gate_hi], ((0, 0), (0, 128 - 2 * M_HEADS)))

    x0 = x.reshape(n, d)
    x1 = _ffn(x0, ffn1_norm[l], bf(ffn1_w_gu[l]), bf(ffn1_w_down[l]), final_norm, final_norm=False)

    proj, gates = _inproj(x1, mix_norm[l], w_all, gate_lo, w_tail, w_gate)

    m_hd = m_inner // M_HEADS
    q_m, k_m, v_m = _mlstm_qkv(proj, m_conv_w[l], m_conv_b[l], bf(m_wq[l]), bf(m_wk[l] * m_hd ** -0.5),
                               bf(m_wv[l]), batch=batch, seq=seq)
    gates_t = gates[:, :2 * M_HEADS].reshape(batch, seq, 2, M_HEADS).transpose(0, 3, 2, 1)
    y_m = _mlstm_scan(q_m, k_m, v_m, gates_t, b_if[l], proj, m_inner // m_hd, m_head_norm[l],
                      batch=batch, seq=seq)

    a_hw = a_width // A_HEADS
    qa_col = 2 * m_inner // a_hw
    lam_vecs = jnp.stack([a_lam_q1[l], a_lam_k1[l], a_lam_q2[l], a_lam_k2[l]])
    y_a = _diff_attn(proj, qa_col, qa_col + A_HEADS, qa_col + 2 * A_HEADS, rel_bias, lam_vecs,
                     a_head_norm[l], batch=batch, seq=seq)

    kx, vx = _mem_kv(mem, mem_norm[l], bf(w_mem_kv[l]))
    qx_col = (2 * m_inner + 3 * a_width) // d
    x2 = _merge(x1, proj, qx_col, qx_col + x_width // d, y_m, y_a, kx, vx, b_gate[l],
                bf(w_branch_m[l]), bf(w_branch_a[l]), bf(w_branch_x[l]), bf(w_out[l]),
                batch=batch, seq=seq)

    out = _ffn(x2, ffn2_norm[l], bf(ffn2_w_gu[l]), bf(ffn2_w_down[l]), final_norm, final_norm=True)
    return out.reshape(batch, seq, d)
```

```python
import functools
import math

import jax
import jax.numpy as jnp
from jax import lax
from jax.experimental import pallas as pl
from jax.experimental.pallas import tpu as pltpu

EPS = 1e-6
BF16 = jnp.bfloat16
F32 = jnp.float32
NEG_BIG = -1e30

M_HEADS = 4
M_CONV = 4
A_HEADS = 8
X_HEADS = 4
N_BRANCH = 3
REL_BUCKETS = 32
REL_MAX_DIST = 128
LAYER = 0

VMEM_LIMIT_BYTES = 56 * 1024 * 1024

FFN_ROWS = 512
INPROJ_ROWS = 2048
INPROJ_COLS = 1024
QKV_ROWS = 2048
QKV_SUBTILES = 8
MLSTM_CHUNK = 256
MLSTM_HEAD_GROUP = 2
ATTN_TILE = 256
ATTN_HEAD_GROUP = 4
LANES = 128
ATTN_STRIP = LANES
ATTN_PV_COLS = 256
ATTN_ONES_ROWS = 16
LOG2_E = math.log2(math.e)
MERGE_ROWS = 512
CONV_HALO = 16


def _params(*sem):
    return pltpu.CompilerParams(dimension_semantics=sem, vmem_limit_bytes=VMEM_LIMIT_BYTES)


def _resident(shape):
    zeros = (0,) * len(shape)
    return pl.BlockSpec(shape, lambda *_: zeros, pipeline_mode=pl.Buffered(1))


def _rms(x):
    return x * lax.rsqrt(jnp.mean(x * x, axis=-1, keepdims=True) + EPS)


def _sigmoid(x):
    return 1.0 / (1.0 + jnp.exp(-x))


def _ffn_kernel(x_ref, g_ref, wgu_ref, wd_ref, fg_ref, o_ref, *, d_ff, final_norm):
    x = x_ref[...]
    hb = (_rms(x) * g_ref[...]).astype(BF16)
    gate = jnp.dot(hb, wgu_ref[:, :d_ff], preferred_element_type=F32)
    up = jnp.dot(hb, wgu_ref[:, d_ff:], preferred_element_type=F32)
    act = (gate * _sigmoid(gate) * up).astype(BF16)
    y = x + 0.5 * jnp.dot(act, wd_ref[...], preferred_element_type=F32)
    if final_norm:
        y = _rms(y) * fg_ref[...]
    o_ref[...] = y


def _ffn(x, gain, w_gu, w_down, final_gain, *, final_norm):
    n, d = x.shape
    d_ff = w_down.shape[0]
    tm = min(FFN_ROWS, n)
    return pl.pallas_call(
        functools.partial(_ffn_kernel, d_ff=d_ff, final_norm=final_norm),
        grid=(n // tm,),
        in_specs=[
            pl.BlockSpec((tm, d), lambda i: (i, 0)),
            _resident((1, d)),
            _resident((d, 2 * d_ff)),
            _resident((d_ff, d)),
            _resident((1, d)),
        ],
        out_specs=pl.BlockSpec((tm, d), lambda i: (i, 0)),
        out_shape=jax.ShapeDtypeStruct((n, d), F32),
        compiler_params=_params("parallel"),
        name="ffn_final" if final_norm else "ffn",
    )(x, gain.reshape(1, d), w_gu, w_down, final_gain.reshape(1, d))


def _inproj_kernel(x_ref, g_ref, w_head_ref, w_tail_ref, wgate_ref, p_ref, gates_ref, h_scr, *, head_blocks):
    j = pl.program_id(1)

    @pl.when(j == 0)
    def _():
        hb = (_rms(x_ref[...]) * g_ref[...]).astype(BF16)
        h_scr[...] = hb
        gates_ref[...] = jnp.dot(hb, wgate_ref[...], preferred_element_type=F32)

    @pl.when(j < head_blocks)
    def _():
        p_ref[...] = jnp.dot(h_scr[...], w_head_ref[...], preferred_element_type=F32).astype(BF16)

    @pl.when(j >= head_blocks)
    def _():
        p_ref[...] = jnp.dot(h_scr[...], w_tail_ref[...], preferred_element_type=F32).astype(BF16)


def _inproj(x, gain, w_all, head_width, w_tail, w_gate):
    n, d = x.shape
    tm = min(INPROJ_ROWS, n)
    tn = INPROJ_COLS
    head_blocks = head_width // tn
    tail_blocks = w_tail.shape[1] // tn
    width = head_width + w_tail.shape[1]
    return pl.pallas_call(
        functools.partial(_inproj_kernel, head_blocks=head_blocks),
        grid=(n // tm, width // tn),
        in_specs=[
            pl.BlockSpec((tm, d), lambda i, j: (i, 0)),
            _resident((1, d)),
            pl.BlockSpec((d, tn), lambda i, j: (0, jnp.minimum(j, head_blocks - 1))),
            pl.BlockSpec((d, tn), lambda i, j: (0, jnp.where(j < head_blocks, tail_blocks - 1, j - head_blocks))),
            _resident(w_gate.shape),
        ],
        out_specs=[
            pl.BlockSpec((tm, tn), lambda i, j: (i, j)),
            pl.BlockSpec((tm, w_gate.shape[1]), lambda i, j: (i, 0)),
        ],
        out_shape=[
            jax.ShapeDtypeStruct((n, width), BF16),
            jax.ShapeDtypeStruct((n, w_gate.shape[1]), F32),
        ],
        scratch_shapes=[pltpu.VMEM((tm, d), BF16)],
        compiler_params=_params("parallel", "arbitrary"),
        name="inproj",
    )(x, gain.reshape(1, d), w_all, w_tail, w_gate)


def _mlstm_qkv_kernel(x_ref, halo_ref, cw_ref, cb_ref, wq_ref, wk_ref, wv_ref,
                      q_ref, k_ref, v_ref):
    ts = x_ref.shape[0]
    xb = x_ref[...]
    v_ref[...] = jnp.dot(xb, wv_ref[0], preferred_element_type=F32).astype(BF16)
    prev = jnp.where(pl.program_id(2) > 0, halo_ref[...].astype(F32), 0.0)
    full = jnp.concatenate([prev, xb.astype(F32)], axis=0)
    cw = cw_ref[...]
    sub = ts // QKV_SUBTILES
    for r0 in range(0, ts, sub):
        conv = cb_ref[...]
        for back in range(M_CONV):
            tap = M_CONV - 1 - back
            start = CONV_HALO + r0 - back
            conv = conv + cw[tap:tap + 1, :] * full[start:start + sub, :]
        xc = (conv * _sigmoid(conv)).astype(BF16)
        q_ref[r0:r0 + sub, :] = jnp.dot(xc, wq_ref[0], preferred_element_type=F32).astype(BF16)
        k_ref[r0:r0 + sub, :] = jnp.dot(xc, wk_ref[0], preferred_element_type=F32).astype(BF16)


def _mlstm_qkv(proj, conv_w, conv_b, wq, wk, wv, *, batch, seq):
    heads, hd, _ = wq.shape
    n = batch * seq
    ts = min(QKV_ROWS, seq)
    nt = seq // ts
    halo_per_tile = ts // CONV_HALO

    def x_map(h, b, t):
        return (b * nt + t, h)

    def halo_map(h, b, t):
        return (jnp.maximum((b * nt + t) * halo_per_tile - 1, 0), h)

    def w_map(h, b, t):
        return (h, 0, 0)

    out = jax.ShapeDtypeStruct((n, heads * hd), BF16)
    return pl.pallas_call(
        _mlstm_qkv_kernel,
        grid=(heads, batch, nt),
        in_specs=[
            pl.BlockSpec((ts, hd), x_map),
            pl.BlockSpec((CONV_HALO, hd), halo_map),
            pl.BlockSpec((M_CONV, hd), lambda h, b, t: (0, h)),
            pl.BlockSpec((1, hd), lambda h, b, t: (0, h)),
            pl.BlockSpec((1, hd, hd), w_map),
            pl.BlockSpec((1, hd, hd), w_map),
            pl.BlockSpec((1, hd, hd), w_map),
        ],
        out_specs=[pl.BlockSpec((ts, hd), x_map)] * 3,
        out_shape=[out, out, out],
        compiler_params=_params("parallel", "parallel", "parallel"),
        name="mlstm_qkv",
    )(proj, proj, conv_w, conv_b.reshape(1, -1), wq, wk, wv)


def _mlstm_scan_kernel(q_ref, k_ref, v_ref, g_ref, bias_ref, o_ref, gain_ref, y_ref,
                       c_scr, n_scr, m_scr):
    @pl.when(pl.program_id(2) == 0)
    def _():
        c_scr[...] = jnp.zeros_like(c_scr)
        n_scr[...] = jnp.zeros_like(n_scr)
        m_scr[...] = jnp.zeros_like(m_scr)

    group, hd, _ = c_scr.shape
    chunk = q_ref.shape[0]
    heads = range(group)
    cols = [slice(g * hd, (g + 1) * hd) for g in heads]
    q = [q_ref[:, cols[g]] for g in heads]
    k = [k_ref[:, cols[g]] for g in heads]
    v = [v_ref[:, cols[g]] for g in heads]
    c_prev = [c_scr[g] for g in heads]
    n_prev = [n_scr[g] for g in heads]
    m_prev = [m_scr[g] for g in heads]

    qk = [lax.dot_general(q[g], k[g], (((1,), (1,)), ((), ())), preferred_element_type=F32) for g in heads]
    qc = [jnp.dot(q[g], c_prev[g].astype(BF16), preferred_element_type=F32) for g in heads]

    t_idx = lax.broadcasted_iota(jnp.int32, (chunk, chunk), 0)
    s_idx = lax.broadcasted_iota(jnp.int32, (chunk, chunk), 1)
    causal = s_idx <= t_idx
    diag = s_idx == t_idx

    def to_col(row):
        return jnp.sum(jnp.where(diag, row, 0.0), axis=1, keepdims=True)

    def to_row(col):
        return jnp.sum(jnp.where(diag, col, 0.0), axis=0, keepdims=True)

    s, w_inter, m_t, w_s, decay, m_new = [], [], [], [], [], []
    for g in heads:
        gates = g_ref[0, g] + bias_ref[g]
        ig = gates[0:1, :]
        fpre = gates[1:2, :]
        lf = jnp.minimum(fpre, 0.0) - jnp.log1p(jnp.exp(-jnp.abs(fpre)))
        b_col = jnp.sum(jnp.where(causal, lf, 0.0), axis=1, keepdims=True)
        b_row = to_row(b_col)
        dlog = jnp.where(causal, b_col - b_row + ig, NEG_BIG)
        inter = b_col + m_prev[g]
        m_t.append(jnp.maximum(inter, jnp.max(dlog, axis=1, keepdims=True)))
        s.append(qk[g] * jnp.exp(dlog - m_t[g]))
        w_inter.append(jnp.exp(inter - m_t[g]))
        b_last = b_col[chunk - 1:chunk, :]
        tail = b_last - b_row + ig
        m_new.append(jnp.maximum(b_last + m_prev[g], jnp.max(tail, axis=1, keepdims=True)))
        decay.append(jnp.exp(b_last + m_prev[g] - m_new[g]))
        w_s.append(jnp.exp(to_col(tail) - m_new[g]))

    sv = [jnp.dot(s[g].astype(BF16), v[g], preferred_element_type=F32) for g in heads]
    kw = [k[g].astype(F32) * w_s[g] for g in heads]
    kv = [jnp.dot(kw[g].T.astype(BF16), v[g], preferred_element_type=F32) for g in heads]

    for g in heads:
        num = sv[g] + w_inter[g] * qc[g]
        qn = jnp.sum(q[g].astype(F32) * n_prev[g], axis=1, keepdims=True)
        den = jnp.sum(s[g], axis=1, keepdims=True) + w_inter[g] * qn
        h = num / jnp.maximum(jnp.abs(den), jnp.exp(-m_t[g]))
        c_scr[g] = decay[g] * c_prev[g] + kv[g]
        n_scr[g] = decay[g] * n_prev[g] + jnp.sum(kw[g], axis=0, keepdims=True)
        m_scr[g] = m_new[g]
        gate = _sigmoid(o_ref[:, cols[g]].astype(F32))
        y_ref[:, cols[g]] = (gate * (_rms(h) * gain_ref[:, cols[g]])).astype(BF16)


def _mlstm_scan(q, k, v, gates_t, b_if, proj, o_col_block, head_gain, *, batch, seq):
    n, width = q.shape
    heads = gates_t.shape[1]
    hd = width // heads
    group = MLSTM_HEAD_GROUP
    gw = group * hd
    assert heads % group == 0 and o_col_block % group == 0
    chunk = min(MLSTM_CHUNK, seq)
    nc = seq // chunk

    def qkv_map(b, h, c):
        return (b * nc + c, h)

    return pl.pallas_call(
        _mlstm_scan_kernel,
        grid=(batch, heads // group, nc),
        in_specs=[
            pl.BlockSpec((chunk, gw), qkv_map),
            pl.BlockSpec((chunk, gw), qkv_map),
            pl.BlockSpec((chunk, gw), qkv_map),
            pl.BlockSpec((1, group, 2, chunk), lambda b, h, c: (b, h, 0, c)),
            pl.BlockSpec((group, 2, 1), lambda b, h, c: (h, 0, 0)),
            pl.BlockSpec((chunk, gw), lambda b, h, c: (b * nc + c, o_col_block // group + h)),
            pl.BlockSpec((1, gw), lambda b, h, c: (0, h)),
        ],
        out_specs=pl.BlockSpec((chunk, gw), qkv_map),
        out_shape=jax.ShapeDtypeStruct((n, width), BF16),
        scratch_shapes=[pltpu.VMEM((group, hd, hd), F32), pltpu.VMEM((group, 1, hd), F32),
                        pltpu.VMEM((group, 1, 1), F32)],
        compiler_params=_params("parallel", "parallel", "arbitrary"),
        name="mlstm_scan",
    )(q, k, v, gates_t, b_if.reshape(2, heads, 1).transpose(1, 0, 2), proj, head_gain.reshape(1, -1))


def _diff_attn_kernel(far_ref, q_ref, k_ref, v_ref, band_ref, lam_ref, gain_ref, y_ref,
                      vt_scr, acc_scr, st_scr, *, out_scale, lam_init):
    group, n_tiles, rows, tile = vt_scr.shape
    width = q_ref.shape[1] // group
    head0 = pl.program_id(1) * group
    qi = pl.program_id(2)
    half = width // 2

    @pl.when(qi == 0)
    def _():
        ones = jnp.ones((rows - width, tile), BF16)
        for g in range(group):
            for j in range(n_tiles):
                v_tile = v_ref[j * tile:(j + 1) * tile, g * width:(g + 1) * width]
                vt_scr[g, j] = jnp.concatenate([v_tile.astype(F32).T.astype(BF16), ones], axis=0)

    q_all = q_ref[...]
    lane = lax.broadcasted_iota(jnp.int32, (tile, width), 1)
    qs = []
    for g in range(group):
        q = q_all[:, g * width:(g + 1) * width].astype(F32) * (half ** -0.5 * LOG2_E)
        stacked = jnp.concatenate([jnp.where(lane < half, q, 0.0), jnp.where(lane >= half, q, 0.0)], axis=0)
        qs.append(stacked.astype(BF16))

    acc_scr[...] = jnp.zeros_like(acc_scr)

    def scores(kj, g):
        start = pl.multiple_of(kj * tile, tile)
        return lax.dot_general(k_ref[pl.ds(start, tile), g * width:(g + 1) * width], qs[g],
                               (((1,), (1,)), ((), ())), preferred_element_type=F32)

    def block(tiles, maxes):
        units = [(kj, band_idx, g) for kj, band_idx in tiles for g in range(group)]
        all_scores, parked = [], 0
        for kj, band_idx, g in units:
            if band_idx is None:
                all_scores.append(scores(kj, g))
            else:
                st_scr[parked] = scores(kj, g)
                all_scores.append(parked)
                parked += 1
        maxes = list(maxes)
        for st, (kj, band_idx, g) in zip(all_scores, units):
            m_prev = maxes[g]
            if band_idx is None:
                shift = far_bias[g]
            else:
                shift = 0.0
                band = band_ref[g, band_idx]
                st = st_scr[st] + jnp.concatenate([band, band], axis=1)
            vt = vt_scr[g, kj]
            m_cols = []
            for c0 in range(0, 2 * tile, ATTN_PV_COLS):
                p_cols, a_cols = [], []
                for c in range(c0, c0 + ATTN_PV_COLS, ATTN_STRIP):
                    cols = slice(c, c + ATTN_STRIP)
                    s_c = st[:, cols]
                    m_c = jnp.maximum(m_prev[:, cols], jnp.max(s_c, axis=0, keepdims=True) + shift)
                    a_cols.append(jnp.exp2(m_prev[:, cols] - m_c))
                    p_cols.append(jnp.exp2(s_c - (m_c - shift)).astype(BF16))
                    m_cols.append(m_c)
                pv = jnp.dot(vt, jnp.concatenate(p_cols, axis=1), preferred_element_type=F32)
                wide = slice(c0, c0 + ATTN_PV_COLS)
                acc_scr[g, :, wide] = jnp.concatenate(a_cols, axis=1) * acc_scr[g, :, wide] + pv
            maxes[g] = jnp.concatenate(m_cols, axis=1)
        return tuple(maxes)

    def far(first, count):
        return tuple((first + i, None) for i in range(count))

    far_bias = tuple(far_ref[head0 + g] * LOG2_E for g in range(group))
    n_far = jnp.maximum(qi - 1, 0)
    n_quads = n_far // 4
    rest = n_far - 4 * n_quads
    maxes = tuple(jnp.full((1, 2 * tile), NEG_BIG, F32) for _ in range(group))
    maxes = lax.fori_loop(0, n_quads, lambda i, c: block(far(4 * i, 4), c), maxes)
    maxes = lax.cond(rest >= 2, lambda: block(far(4 * n_quads, 2), maxes), lambda: maxes)
    near = ((qi - 1, 0), (qi, 1))

    def last_tiles():
        return lax.cond(rest % 2 == 1, lambda: block(far(n_far - 1, 1) + near, maxes), lambda: block(near, maxes))

    lax.cond(qi > 0, last_tiles, lambda: block(((qi, 1),), maxes))

    lam_v = lam_ref[...]
    lam = (jnp.exp(jnp.sum(lam_v[0:1, :] * lam_v[1:2, :], axis=1, keepdims=True))
           - jnp.exp(jnp.sum(lam_v[2:3, :] * lam_v[3:4, :], axis=1, keepdims=True)) + lam_init)
    for g in range(group):
        o = acc_scr[g, :width, :] / acc_scr[g, width:width + 1, :]
        out = (o[:, :tile] - lam * o[:, tile:]).T
        gain = gain_ref[:, g * width:(g + 1) * width]
        y_ref[:, g * width:(g + 1) * width] = (_rms(out) * gain * out_scale).astype(BF16)


def _t5_bucket(n):
    max_exact = REL_BUCKETS // 2
    nf = jnp.maximum(n, 1).astype(F32)
    large = max_exact + (jnp.log(nf / max_exact) / math.log(REL_MAX_DIST / max_exact)
                         * (REL_BUCKETS - max_exact)).astype(jnp.int32)
    large = jnp.minimum(large, REL_BUCKETS - 1)
    return jnp.where(n < max_exact, n, large)


def _diff_attn(proj, q_col, k_col, v_col, rel_bias, lam_vecs, head_gain, *, batch, seq):
    n = proj.shape[0]
    heads = rel_bias.shape[1]
    hw = head_gain.shape[0] // heads
    tile = min(ATTN_TILE, seq)
    nq = seq // tile
    assert tile >= REL_MAX_DIST, "distances beyond one tile must all fall in the last bucket"
    group = ATTN_HEAD_GROUP
    gw = group * hw
    assert heads % group == 0 and q_col % group == 0 and k_col % group == 0 and v_col % group == 0

    bucket = _t5_bucket(jnp.arange(2 * tile, dtype=jnp.int32))
    by_dist = jnp.zeros((heads, 2 * tile), F32)
    for b in range(REL_BUCKETS):
        by_dist = jnp.where(bucket[None] == b, rel_bias[b].astype(F32)[:, None], by_dist)
    masked = jnp.full((heads, tile - 1), NEG_BIG, F32)
    line = jnp.concatenate([masked, by_dist * LOG2_E, masked[:, :2]], axis=1)
    length = line.shape[1] - 1
    skew = jnp.tile(line, (1, tile))[:, :tile * length].reshape(heads, tile, length)
    band = jnp.stack([skew[:, :, 2 * tile - 1:3 * tile - 1], skew[:, :, tile - 1:2 * tile - 1]], axis=1)
    far = rel_bias[REL_BUCKETS - 1].astype(F32)

    lam_init = 0.8 - 0.6 * math.exp(-0.3 * LAYER)
    return pl.pallas_call(
        functools.partial(_diff_attn_kernel, out_scale=1.0 - lam_init, lam_init=lam_init),
        grid=(batch, heads // group, nq),
        in_specs=[
            pl.BlockSpec(memory_space=pltpu.SMEM),
            pl.BlockSpec((tile, gw), lambda b, h, i: (b * nq + i, q_col // group + h)),
            pl.BlockSpec((seq, gw), lambda b, h, i: (b, k_col // group + h)),
            pl.BlockSpec((seq, gw), lambda b, h, i: (b, v_col // group + h)),
            pl.BlockSpec((group, 2, tile, tile), lambda b, h, i: (h, 0, 0, 0)),
            _resident(lam_vecs.shape),
            pl.BlockSpec((1, gw), lambda b, h, i: (0, h)),
        ],
        out_specs=pl.BlockSpec((tile, gw), lambda b, h, i: (b * nq + i, h)),
        out_shape=jax.ShapeDtypeStruct((n, heads * hw), BF16),
        scratch_shapes=[
            pltpu.VMEM((group, nq, hw + ATTN_ONES_ROWS, tile), BF16),
            pltpu.VMEM((group, hw + ATTN_ONES_ROWS, 2 * tile), F32),
            pltpu.VMEM((2 * group, tile, 2 * tile), F32),
        ],
        compiler_params=_params("parallel", "parallel", "arbitrary"),
        name="diff_attn",
    )(far, proj, proj, proj, band, lam_vecs, head_gain.reshape(1, -1))


def _mem_kv_kernel(mem_ref, g_ref, w_ref, k_ref, v_ref):
    width = k_ref.shape[-1]
    hb = (_rms(mem_ref[0]) * g_ref[...]).astype(BF16)
    k_ref[0] = jnp.dot(hb, w_ref[:, :width], preferred_element_type=F32).astype(BF16)
    v_ref[0] = jnp.dot(hb, w_ref[:, width:], preferred_element_type=F32).astype(BF16)


def _mem_kv(mem, gain, w_kv):
    batch, mlen, d = mem.shape
    width = w_kv.shape[1] // 2
    out = jax.ShapeDtypeStruct((batch, mlen, width), BF16)
    return pl.pallas_call(
        _mem_kv_kernel,
        grid=(batch,),
        in_specs=[
            pl.BlockSpec((1, mlen, d), lambda b: (b, 0, 0)),
            _resident((1, d)),
            _resident(w_kv.shape),
        ],
        out_specs=[pl.BlockSpec((1, mlen, width), lambda b: (b, 0, 0))] * 2,
        out_shape=[out, out],
        compiler_params=_params("parallel"),
        name="mem_kv",
    )(mem, gain.reshape(1, d), w_kv)


def _merge_kernel(x_ref, qx_ref, g0_ref, g1_ref, g2_ref, ym_ref, ya_ref, kx_ref, vx_ref,
                  bg_ref, wm_ref, wa_ref, wx_ref, wo_ref, o_ref):
    d = x_ref.shape[1]
    xw = qx_ref.shape[1]
    xhd = xw // X_HEADS
    heads = []
    for h in range(X_HEADS):
        cols = slice(h * xhd, (h + 1) * xhd)
        s = lax.dot_general(qx_ref[:, cols], kx_ref[0, :, cols], (((1,), (1,)), ((), ())),
                            preferred_element_type=F32) * (xhd ** -0.5)
        p = jnp.exp(s - jnp.max(s, axis=1, keepdims=True))
        pv = jnp.dot(p.astype(BF16), vx_ref[0, :, cols], preferred_element_type=F32)
        heads.append((pv / jnp.sum(p, axis=1, keepdims=True)).astype(BF16))
    yx = jnp.concatenate(heads, axis=1)

    def gate(g_ref, k):
        return _sigmoid(g_ref[...].astype(F32) + bg_ref[:, k * d:(k + 1) * d])

    merged = (gate(g0_ref, 0) * jnp.dot(ym_ref[...], wm_ref[...], preferred_element_type=F32)
              + gate(g1_ref, 1) * jnp.dot(ya_ref[...], wa_ref[...], preferred_element_type=F32)
              + gate(g2_ref, 2) * jnp.dot(yx, wx_ref[...], preferred_element_type=F32))
    o_ref[...] = x_ref[...] + jnp.dot(merged.astype(BF16), wo_ref[...], preferred_element_type=F32)


def _merge(x, proj, qx_col, g_col, y_m, y_a, kx, vx, b_gate, w_m, w_a, w_x, w_o, *, batch, seq):
    n, d = x.shape
    tm = min(MERGE_ROWS, seq)
    nt = seq // tm

    def row(b, t):
        return b * nt + t

    def proj_spec(col):
        return pl.BlockSpec((tm, d), lambda b, t: (row(b, t), col))

    def mem_spec(a):
        return pl.BlockSpec((1,) + a.shape[1:], lambda b, t: (b, 0, 0))

    return pl.pallas_call(
        _merge_kernel,
        grid=(batch, nt),
        in_specs=[
            pl.BlockSpec((tm, d), lambda b, t: (row(b, t), 0)),
            proj_spec(qx_col), proj_spec(g_col), proj_spec(g_col + 1), proj_spec(g_col + 2),
            pl.BlockSpec((tm, y_m.shape[1]), lambda b, t: (row(b, t), 0)),
            pl.BlockSpec((tm, y_a.shape[1]), lambda b, t: (row(b, t), 0)),
            mem_spec(kx), mem_spec(vx),
            _resident((1, b_gate.shape[0])),
            _resident(w_m.shape), _resident(w_a.shape), _resident(w_x.shape), _resident(w_o.shape),
        ],
        out_specs=pl.BlockSpec((tm, d), lambda b, t: (row(b, t), 0)),
        out_shape=jax.ShapeDtypeStruct((n, d), F32),
        compiler_params=_params("parallel", "parallel"),
        name="merge",
    )(x, proj, proj, proj, proj, y_m, y_a, kx, vx, b_gate.reshape(1, -1), w_m, w_a, w_x, w_o)


def kernel(x, mem, positions, rel_bias, ffn1_norm, ffn1_w_gu, ffn1_w_down, mix_norm, mem_norm, w_in, b_if, b_gate, m_conv_w, m_conv_b, m_wq, m_wk, m_wv, m_head_norm, a_lam_q1, a_lam_k1, a_lam_q2, a_lam_k2, a_head_norm, w_mem_kv, w_branch_m, w_branch_a, w_branch_x, w_out, ffn2_norm, ffn2_w_gu, ffn2_w_down, final_norm):
    del positions
    batch, seq, d = x.shape
    n = batch * seq
    l = LAYER
    m_inner = m_conv_w.shape[-1]
    a_width = a_head_norm.shape[-1]
    x_width = w_mem_kv.shape[-1] // 2
    bf = lambda a: a.astype(BF16)

    gate_lo = 2 * m_inner
    gate_hi = gate_lo + 2 * M_HEADS
    w_all = bf(w_in[l])
    w_tail = w_all[:, gate_hi:]
    w_gate = jnp.pad(w_all[:, gate_lo:gate_hi], ((0, 0), (0, LANES - 2 * M_HEADS)))

    x0 = x.reshape(n, d)
    x1 = _ffn(x0, ffn1_norm[l], bf(ffn1_w_gu[l]), bf(ffn1_w_down[l]), final_norm, final_norm=False)

    proj, gates = _inproj(x1, mix_norm[l], w_all, gate_lo, w_tail, w_gate)

    m_hd = m_inner // M_HEADS
    q_m, k_m, v_m = _mlstm_qkv(proj, m_conv_w[l], m_conv_b[l], bf(m_wq[l]), bf(m_wk[l] * m_hd ** -0.5),
                               bf(m_wv[l]), batch=batch, seq=seq)
    gates_t = gates[:, :2 * M_HEADS].reshape(batch, seq, 2, M_HEADS).transpose(0, 3, 2, 1)
    y_m = _mlstm_scan(q_m, k_m, v_m, gates_t, b_if[l], proj, m_inner // m_hd, m_head_norm[l],
                      batch=batch, seq=seq)

    a_hw = a_width // A_HEADS
    qa_col = 2 * m_inner // a_hw
    lam_vecs = jnp.stack([a_lam_q1[l], a_lam_k1[l], a_lam_q2[l], a_lam_k2[l]])
    y_a = _diff_attn(proj, qa_col, qa_col + A_HEADS, qa_col + 2 * A_HEADS, rel_bias, lam_vecs,
                     a_head_norm[l], batch=batch, seq=seq)

    kx, vx = _mem_kv(mem, mem_norm[l], bf(w_mem_kv[l]))
    qx_col = (2 * m_inner + 3 * a_width) // d
    x2 = _merge(x1, proj, qx_col, qx_col + x_width // d, y_m, y_a, kx, vx, b_gate[l],
                bf(w_branch_m[l]), bf(w_branch_a[l]), bf(w_branch_x[l]), bf(w_out[l]),
                batch=batch, seq=seq)

    out = _ffn(x2, ffn2_norm[l], bf(ffn2_w_gu[l]), bf(ffn2_w_down[l]), final_norm, final_norm=True)
    return out.reshape(batch, seq, d)
```

```python
import functools
import math

import jax
import jax.numpy as jnp
from jax import lax
from jax.experimental import pallas as pl
from jax.experimental.pallas import tpu as pltpu

EPS = 1e-6
BF16 = jnp.bfloat16
F32 = jnp.float32
NEG_BIG = -1e30

M_HEADS = 4
M_CONV = 4
A_HEADS = 8
X_HEADS = 4
N_BRANCH = 3
REL_BUCKETS = 32
REL_MAX_DIST = 128
LAYER = 0

VMEM_LIMIT_BYTES = 56 * 1024 * 1024

FFN_ROWS = 512
INPROJ_ROWS = 2048
INPROJ_COLS = 1024
QKV_ROWS = 4096
QKV_SUBTILES = 16
MLSTM_CHUNK = 512
MLSTM_HEAD_GROUP = 2
ATTN_TILE = 256
ATTN_HEAD_GROUP = 4
LANES = 128
ATTN_STRIP = LANES
ATTN_PV_COLS = 256
ATTN_ONES_ROWS = 16
LOG2_E = math.log2(math.e)
MERGE_ROWS = 512
CONV_HALO = 16


def _params(*sem):
    return pltpu.CompilerParams(dimension_semantics=sem, vmem_limit_bytes=VMEM_LIMIT_BYTES)


def _resident(shape):
    zeros = (0,) * len(shape)
    return pl.BlockSpec(shape, lambda *_: zeros, pipeline_mode=pl.Buffered(1))


def _rms(x):
    return x * lax.rsqrt(jnp.mean(x * x, axis=-1, keepdims=True) + EPS)


def _sigmoid(x):
    return 1.0 / (1.0 + jnp.exp(-x))


def _ffn_kernel(x_ref, g_ref, wgu_ref, wd_ref, fg_ref, o_ref, *, d_ff, final_norm):
    x = x_ref[...]
    hb = (_rms(x) * g_ref[...]).astype(BF16)
    gate = jnp.dot(hb, wgu_ref[:, :d_ff], preferred_element_type=F32)
    up = jnp.dot(hb, wgu_ref[:, d_ff:], preferred_element_type=F32)
    act = (gate * _sigmoid(gate) * up).astype(BF16)
    y = x + 0.5 * jnp.dot(act, wd_ref[...], preferred_element_type=F32)
    if final_norm:
        y = _rms(y) * fg_ref[...]
    o_ref[...] = y


def _ffn(x, gain, w_gu, w_down, final_gain, *, final_norm):
    n, d = x.shape
    d_ff = w_down.shape[0]
    tm = min(FFN_ROWS, n)
    return pl.pallas_call(
        functools.partial(_ffn_kernel, d_ff=d_ff, final_norm=final_norm),
        grid=(n // tm,),
        in_specs=[
            pl.BlockSpec((tm, d), lambda i: (i, 0)),
            _resident((1, d)),
            _resident((d, 2 * d_ff)),
            _resident((d_ff, d)),
            _resident((1, d)),
        ],
        out_specs=pl.BlockSpec((tm, d), lambda i: (i, 0)),
        out_shape=jax.ShapeDtypeStruct((n, d), F32),
        compiler_params=_params("parallel"),
        name="ffn_final" if final_norm else "ffn",
    )(x, gain.reshape(1, d), w_gu, w_down, final_gain.reshape(1, d))


def _inproj_kernel(x_ref, g_ref, w_head_ref, w_tail_ref, wgate_ref, p_ref, gates_ref, h_scr, *, head_blocks):
    j = pl.program_id(1)

    @pl.when(j == 0)
    def _():
        hb = (_rms(x_ref[...]) * g_ref[...]).astype(BF16)
        h_scr[...] = hb
        gates_ref[...] = jnp.dot(hb, wgate_ref[...], preferred_element_type=F32)

    @pl.when(j < head_blocks)
    def _():
        p_ref[...] = jnp.dot(h_scr[...], w_head_ref[...], preferred_element_type=F32).astype(BF16)

    @pl.when(j >= head_blocks)
    def _():
        p_ref[...] = jnp.dot(h_scr[...], w_tail_ref[...], preferred_element_type=F32).astype(BF16)


def _inproj(x, gain, w_all, head_width, w_tail, w_gate):
    n, d = x.shape
    tm = min(INPROJ_ROWS, n)
    tn = INPROJ_COLS
    head_blocks = head_width // tn
    tail_blocks = w_tail.shape[1] // tn
    width = head_width + w_tail.shape[1]
    return pl.pallas_call(
        functools.partial(_inproj_kernel, head_blocks=head_blocks),
        grid=(n // tm, width // tn),
        in_specs=[
            pl.BlockSpec((tm, d), lambda i, j: (i, 0)),
            _resident((1, d)),
            pl.BlockSpec((d, tn), lambda i, j: (0, jnp.minimum(j, head_blocks - 1))),
            pl.BlockSpec((d, tn), lambda i, j: (0, jnp.where(j < head_blocks, tail_blocks - 1, j - head_blocks))),
            _resident(w_gate.shape),
        ],
        out_specs=[
            pl.BlockSpec((tm, tn), lambda i, j: (i, j)),
            pl.BlockSpec((tm, w_gate.shape[1]), lambda i, j: (i, 0)),
        ],
        out_shape=[
            jax.ShapeDtypeStruct((n, width), BF16),
            jax.ShapeDtypeStruct((n, w_gate.shape[1]), F32),
        ],
        scratch_shapes=[pltpu.VMEM((tm, d), BF16)],
        compiler_params=_params("parallel", "arbitrary"),
        name="inproj",
    )(x, gain.reshape(1, d), w_all, w_tail, w_gate)


def _mlstm_qkv_kernel(x_ref, halo_ref, cw_ref, cb_ref, wq_ref, wk_ref, wv_ref,
                      q_ref, k_ref, v_ref):
    ts = x_ref.shape[0]
    xb = x_ref[...]
    v_ref[...] = jnp.dot(xb, wv_ref[0], preferred_element_type=F32).astype(BF16)
    prev = jnp.where(pl.program_id(2) > 0, halo_ref[...].astype(F32), 0.0)
    full = jnp.concatenate([prev, xb.astype(F32)], axis=0)
    cw = cw_ref[...]
    sub = ts // QKV_SUBTILES
    for r0 in range(0, ts, sub):
        conv = cb_ref[...]
        for back in range(M_CONV):
            tap = M_CONV - 1 - back
            start = CONV_HALO + r0 - back
            conv = conv + cw[tap:tap + 1, :] * full[start:start + sub, :]
        xc = (conv * _sigmoid(conv)).astype(BF16)
        q_ref[r0:r0 + sub, :] = jnp.dot(xc, wq_ref[0], preferred_element_type=F32).astype(BF16)
        k_ref[r0:r0 + sub, :] = jnp.dot(xc, wk_ref[0], preferred_element_type=F32).astype(BF16)


def _mlstm_qkv(proj, conv_w, conv_b, wq, wk, wv, *, batch, seq):
    heads, hd, _ = wq.shape
    n = batch * seq
    ts = min(QKV_ROWS, seq)
    nt = seq // ts
    halo_per_tile = ts // CONV_HALO

    def x_map(h, b, t):
        return (b * nt + t, h)

    def halo_map(h, b, t):
        return (jnp.maximum((b * nt + t) * halo_per_tile - 1, 0), h)

    def w_map(h, b, t):
        return (h, 0, 0)

    out = jax.ShapeDtypeStruct((n, heads * hd), BF16)
    return pl.pallas_call(
        _mlstm_qkv_kernel,
        grid=(heads, batch, nt),
        in_specs=[
            pl.BlockSpec((ts, hd), x_map),
            pl.BlockSpec((CONV_HALO, hd), halo_map),
            pl.BlockSpec((M_CONV, hd), lambda h, b, t: (0, h)),
            pl.BlockSpec((1, hd), lambda h, b, t: (0, h)),
            pl.BlockSpec((1, hd, hd), w_map),
            pl.BlockSpec((1, hd, hd), w_map),
            pl.BlockSpec((1, hd, hd), w_map),
        ],
        out_specs=[pl.BlockSpec((ts, hd), x_map)] * 3,
        out_shape=[out, out, out],
        compiler_params=_params("parallel", "parallel", "parallel"),
        name="mlstm_qkv",
    )(proj, proj, conv_w, conv_b.reshape(1, -1), wq, wk, wv)


def _mlstm_scan_kernel(q_ref, k_ref, v_ref, g_ref, bias_ref, o_ref, gain_ref, y_ref,
                       c_scr, n_scr, m_scr):
    @pl.when(pl.program_id(2) == 0)
    def _():
        c_scr[...] = jnp.zeros_like(c_scr)
        n_scr[...] = jnp.zeros_like(n_scr)
        m_scr[...] = jnp.zeros_like(m_scr)

    group, hd, _ = c_scr.shape
    chunk = q_ref.shape[0]
    heads = range(group)
    cols = [slice(g * hd, (g + 1) * hd) for g in heads]
    q = [q_ref[:, cols[g]] for g in heads]
    k = [k_ref[:, cols[g]] for g in heads]
    v = [v_ref[:, cols[g]] for g in heads]
    c_prev = [c_scr[g] for g in heads]
    n_prev = [n_scr[g] for g in heads]
    m_prev = [m_scr[g] for g in heads]

    qk = [lax.dot_general(q[g], k[g], (((1,), (1,)), ((), ())), preferred_element_type=F32) for g in heads]
    qc = [jnp.dot(q[g], c_prev[g].astype(BF16), preferred_element_type=F32) for g in heads]

    t_idx = lax.broadcasted_iota(jnp.int32, (chunk, chunk), 0)
    s_idx = lax.broadcasted_iota(jnp.int32, (chunk, chunk), 1)
    causal = s_idx <= t_idx
    diag = s_idx == t_idx

    def to_col(row):
        return jnp.sum(jnp.where(diag, row, 0.0), axis=1, keepdims=True)

    def to_row(col):
        return jnp.sum(jnp.where(diag, col, 0.0), axis=0, keepdims=True)

    s, w_inter, m_t, w_s, decay, m_new = [], [], [], [], [], []
    for g in heads:
        gates = g_ref[0, g] + bias_ref[g]
        ig = gates[0:1, :]
        fpre = gates[1:2, :]
        lf = jnp.minimum(fpre, 0.0) - jnp.log1p(jnp.exp(-jnp.abs(fpre)))
        b_col = jnp.sum(jnp.where(causal, lf, 0.0), axis=1, keepdims=True)
        b_row = to_row(b_col)
        dlog = jnp.where(causal, b_col - b_row + ig, NEG_BIG)
        inter = b_col + m_prev[g]
        m_t.append(jnp.maximum(inter, jnp.max(dlog, axis=1, keepdims=True)))
        s.append(qk[g] * jnp.exp(dlog - m_t[g]))
        w_inter.append(jnp.exp(inter - m_t[g]))
        b_last = b_col[chunk - 1:chunk, :]
        tail = b_last - b_row + ig
        m_new.append(jnp.maximum(b_last + m_prev[g], jnp.max(tail, axis=1, keepdims=True)))
        decay.append(jnp.exp(b_last + m_prev[g] - m_new[g]))
        w_s.append(jnp.exp(to_col(tail) - m_new[g]))

    sv = [jnp.dot(s[g].astype(BF16), v[g], preferred_element_type=F32) for g in heads]
    kw = [k[g].astype(F32) * w_s[g] for g in heads]
    kv = [jnp.dot(kw[g].T.astype(BF16), v[g], preferred_element_type=F32) for g in heads]

    for g in heads:
        num = sv[g] + w_inter[g] * qc[g]
        qn = jnp.sum(q[g].astype(F32) * n_prev[g], axis=1, keepdims=True)
        den = jnp.sum(s[g], axis=1, keepdims=True) + w_inter[g] * qn
        h = num / jnp.maximum(jnp.abs(den), jnp.exp(-m_t[g]))
        c_scr[g] = decay[g] * c_prev[g] + kv[g]
        n_scr[g] = decay[g] * n_prev[g] + jnp.sum(kw[g], axis=0, keepdims=True)
        m_scr[g] = m_new[g]
        gate = _sigmoid(o_ref[:, cols[g]].astype(F32))
        y_ref[:, cols[g]] = (gate * (_rms(h) * gain_ref[:, cols[g]])).astype(BF16)


def _mlstm_scan(q, k, v, gates_t, b_if, proj, o_col_block, head_gain, *, batch, seq):
    n, width = q.shape
    heads = gates_t.shape[1]
    hd = width // heads
    group = MLSTM_HEAD_GROUP
    gw = group * hd
    assert heads % group == 0 and o_col_block % group == 0
    chunk = min(MLSTM_CHUNK, seq)
    nc = seq // chunk

    def qkv_map(b, h, c):
        return (b * nc + c, h)

    return pl.pallas_call(
        _mlstm_scan_kernel,
        grid=(batch, heads // group, nc),
        in_specs=[
            pl.BlockSpec((chunk, gw), qkv_map),
            pl.BlockSpec((chunk, gw), qkv_map),
            pl.BlockSpec((chunk, gw), qkv_map),
            pl.BlockSpec((1, group, 2, chunk), lambda b, h, c: (b, h, 0, c)),
            pl.BlockSpec((group, 2, 1), lambda b, h, c: (h, 0, 0)),
            pl.BlockSpec((chunk, gw), lambda b, h, c: (b * nc + c, o_col_block // group + h)),
            pl.BlockSpec((1, gw), lambda b, h, c: (0, h)),
        ],
        out_specs=pl.BlockSpec((chunk, gw), qkv_map),
        out_shape=jax.ShapeDtypeStruct((n, width), BF16),
        scratch_shapes=[pltpu.VMEM((group, hd, hd), F32), pltpu.VMEM((group, 1, hd), F32),
                        pltpu.VMEM((group, 1, 1), F32)],
        compiler_params=_params("parallel", "parallel", "arbitrary"),
        name="mlstm_scan",
    )(q, k, v, gates_t, b_if.reshape(2, heads, 1).transpose(1, 0, 2), proj, head_gain.reshape(1, -1))


def _diff_attn_kernel(far_ref, q_ref, k_ref, v_ref, band_ref, lam_ref, gain_ref, y_ref,
                      vt_scr, acc_scr, st_scr, *, out_scale, lam_init):
    group, n_tiles, rows, tile = vt_scr.shape
    width = q_ref.shape[1] // group
    head0 = pl.program_id(1) * group
    qi = pl.program_id(2)
    half = width // 2

    @pl.when(qi == 0)
    def _():
        ones = jnp.ones((rows - width, tile), BF16)
        for g in range(group):
            for j in range(n_tiles):
                v_tile = v_ref[j * tile:(j + 1) * tile, g * width:(g + 1) * width]
                vt_scr[g, j] = jnp.concatenate([v_tile.astype(F32).T.astype(BF16), ones], axis=0)

    q_all = q_ref[...]
    lane = lax.broadcasted_iota(jnp.int32, (tile, width), 1)
    qs = []
    for g in range(group):
        q = q_all[:, g * width:(g + 1) * width].astype(F32) * (half ** -0.5 * LOG2_E)
        stacked = jnp.concatenate([jnp.where(lane < half, q, 0.0), jnp.where(lane >= half, q, 0.0)], axis=0)
        qs.append(stacked.astype(BF16))

    acc_scr[...] = jnp.zeros_like(acc_scr)
    lam_v = lam_ref[...]
    lam = (jnp.exp(jnp.sum(lam_v[0:1, :] * lam_v[1:2, :], axis=1, keepdims=True))
           - jnp.exp(jnp.sum(lam_v[2:3, :] * lam_v[3:4, :], axis=1, keepdims=True)) + lam_init)

    def scores(kj, g):
        start = pl.multiple_of(kj * tile, tile)
        return lax.dot_general(k_ref[pl.ds(start, tile), g * width:(g + 1) * width], qs[g],
                               (((1,), (1,)), ((), ())), preferred_element_type=F32)

    def block(tiles, maxes):
        units = [(kj, band_idx, g) for kj, band_idx in tiles for g in range(group)]
        all_scores, parked = [], 0
        for kj, band_idx, g in units:
            if band_idx is None:
                all_scores.append(scores(kj, g))
            else:
                st_scr[parked] = scores(kj, g)
                all_scores.append(parked)
                parked += 1
        maxes = list(maxes)
        for st, (kj, band_idx, g) in zip(all_scores, units):
            m_prev = maxes[g]
            if band_idx is None:
                shift = far_bias[g]
            else:
                shift = 0.0
                band = band_ref[g, band_idx]
                st = st_scr[st] + jnp.concatenate([band, band], axis=1)
            vt = vt_scr[g, kj]
            m_cols = []
            for c0 in range(0, 2 * tile, ATTN_PV_COLS):
                p_cols, a_cols = [], []
                for c in range(c0, c0 + ATTN_PV_COLS, ATTN_STRIP):
                    cols = slice(c, c + ATTN_STRIP)
                    s_c = st[:, cols]
                    m_c = jnp.maximum(m_prev[:, cols], jnp.max(s_c, axis=0, keepdims=True) + shift)
                    a_cols.append(jnp.exp2(m_prev[:, cols] - m_c))
                    p_cols.append(jnp.exp2(s_c - (m_c - shift)).astype(BF16))
                    m_cols.append(m_c)
                pv = jnp.dot(vt, jnp.concatenate(p_cols, axis=1), preferred_element_type=F32)
                wide = slice(c0, c0 + ATTN_PV_COLS)
                acc_scr[g, :, wide] = jnp.concatenate(a_cols, axis=1) * acc_scr[g, :, wide] + pv
            maxes[g] = jnp.concatenate(m_cols, axis=1)
        return tuple(maxes)

    def far(first, count):
        return tuple((first + i, None) for i in range(count))

    far_bias = tuple(far_ref[head0 + g] * LOG2_E for g in range(group))
    n_far = jnp.maximum(qi - 1, 0)
    n_quads = n_far // 4
    rest = n_far - 4 * n_quads
    maxes = tuple(jnp.full((1, 2 * tile), NEG_BIG, F32) for _ in range(group))
    maxes = lax.fori_loop(0, n_quads, lambda i, c: block(far(4 * i, 4), c), maxes)
    maxes = lax.cond(rest >= 2, lambda: block(far(4 * n_quads, 2), maxes), lambda: maxes)
    near = ((qi - 1, 0), (qi, 1))

    def last_tiles():
        return lax.cond(rest % 2 == 1, lambda: block(far(n_far - 1, 1) + near, maxes), lambda: block(near, maxes))

    lax.cond(qi > 0, last_tiles, lambda: block(((qi, 1),), maxes))

    for g in range(group):
        o = acc_scr[g, :width, :] / acc_scr[g, width:width + 1, :]
        out = (o[:, :tile] - lam * o[:, tile:]).T
        gain = gain_ref[:, g * width:(g + 1) * width]
        y_ref[:, g * width:(g + 1) * width] = (_rms(out) * gain * out_scale).astype(BF16)


def _t5_bucket(n):
    max_exact = REL_BUCKETS // 2
    nf = jnp.maximum(n, 1).astype(F32)
    large = max_exact + (jnp.log(nf / max_exact) / math.log(REL_MAX_DIST / max_exact)
                         * (REL_BUCKETS - max_exact)).astype(jnp.int32)
    large = jnp.minimum(large, REL_BUCKETS - 1)
    return jnp.where(n < max_exact, n, large)


def _diff_attn(proj, q_col, k_col, v_col, rel_bias, lam_vecs, head_gain, *, batch, seq):
    n = proj.shape[0]
    heads = rel_bias.shape[1]
    hw = head_gain.shape[0] // heads
    tile = min(ATTN_TILE, seq)
    nq = seq // tile
    assert tile >= REL_MAX_DIST, "distances beyond one tile must all fall in the last bucket"
    group = ATTN_HEAD_GROUP
    gw = group * hw
    assert heads % group == 0 and q_col % group == 0 and k_col % group == 0 and v_col % group == 0

    bucket = _t5_bucket(jnp.arange(2 * tile, dtype=jnp.int32))
    by_dist = jnp.zeros((heads, 2 * tile), F32)
    for b in range(REL_BUCKETS):
        by_dist = jnp.where(bucket[None] == b, rel_bias[b].astype(F32)[:, None], by_dist)
    masked = jnp.full((heads, tile - 1), NEG_BIG, F32)
    line = jnp.concatenate([masked, by_dist * LOG2_E, masked[:, :2]], axis=1)
    length = line.shape[1] - 1
    skew = jnp.tile(line, (1, tile))[:, :tile * length].reshape(heads, tile, length)
    band = jnp.stack([skew[:, :, 2 * tile - 1:3 * tile - 1], skew[:, :, tile - 1:2 * tile - 1]], axis=1)
    far = rel_bias[REL_BUCKETS - 1].astype(F32)

    lam_init = 0.8 - 0.6 * math.exp(-0.3 * LAYER)
    return pl.pallas_call(
        functools.partial(_diff_attn_kernel, out_scale=1.0 - lam_init, lam_init=lam_init),
        grid=(batch, heads // group, nq),
        in_specs=[
            pl.BlockSpec(memory_space=pltpu.SMEM),
            pl.BlockSpec((tile, gw), lambda b, h, i: (b * nq + i, q_col // group + h)),
            pl.BlockSpec((seq, gw), lambda b, h, i: (b, k_col // group + h)),
            pl.BlockSpec((seq, gw), lambda b, h, i: (b, v_col // group + h)),
            pl.BlockSpec((group, 2, tile, tile), lambda b, h, i: (h, 0, 0, 0)),
            _resident(lam_vecs.shape),
            pl.BlockSpec((1, gw), lambda b, h, i: (0, h)),
        ],
        out_specs=pl.BlockSpec((tile, gw), lambda b, h, i: (b * nq + i, h)),
        out_shape=jax.ShapeDtypeStruct((n, heads * hw), BF16),
        scratch_shapes=[
            pltpu.VMEM((group, nq, hw + ATTN_ONES_ROWS, tile), BF16),
            pltpu.VMEM((group, hw + ATTN_ONES_ROWS, 2 * tile), F32),
            pltpu.VMEM((2 * group, tile, 2 * tile), F32),
        ],
        compiler_params=_params("parallel", "parallel", "arbitrary"),
        name="diff_attn",
    )(far, proj, proj, proj, band, lam_vecs, head_gain.reshape(1, -1))


def _mem_kv_kernel(mem_ref, g_ref, w_ref, k_ref, v_ref):
    width = k_ref.shape[-1]
    hb = (_rms(mem_ref[0]) * g_ref[...]).astype(BF16)
    k_ref[0] = jnp.dot(hb, w_ref[:, :width], preferred_element_type=F32).astype(BF16)
    v_ref[0] = jnp.dot(hb, w_ref[:, width:], preferred_element_type=F32).astype(BF16)


def _mem_kv(mem, gain, w_kv):
    batch, mlen, d = mem.shape
    width = w_kv.shape[1] // 2
    out = jax.ShapeDtypeStruct((batch, mlen, width), BF16)
    return pl.pallas_call(
        _mem_kv_kernel,
        grid=(batch,),
        in_specs=[
            pl.BlockSpec((1, mlen, d), lambda b: (b, 0, 0)),
            _resident((1, d)),
            _resident(w_kv.shape),
        ],
        out_specs=[pl.BlockSpec((1, mlen, width), lambda b: (b, 0, 0))] * 2,
        out_shape=[out, out],
        compiler_params=_params("parallel"),
        name="mem_kv",
    )(mem, gain.reshape(1, d), w_kv)


def _merge_kernel(x_ref, qx_ref, g0_ref, g1_ref, g2_ref, ym_ref, ya_ref, kx_ref, vx_ref,
                  bg_ref, wm_ref, wa_ref, wx_ref, wo_ref, o_ref):
    d = x_ref.shape[1]
    xw = qx_ref.shape[1]
    xhd = xw // X_HEADS
    heads = []
    for h in range(X_HEADS):
        cols = slice(h * xhd, (h + 1) * xhd)
        s = lax.dot_general(qx_ref[:, cols], kx_ref[0, :, cols], (((1,), (1,)), ((), ())),
                            preferred_element_type=F32) * (xhd ** -0.5)
        p = jnp.exp(s - jnp.max(s, axis=1, keepdims=True))
        pv = jnp.dot(p.astype(BF16), vx_ref[0, :, cols], preferred_element_type=F32)
        heads.append((pv / jnp.sum(p, axis=1, keepdims=True)).astype(BF16))
    yx = jnp.concatenate(heads, axis=1)

    def gate(g_ref, k):
        return _sigmoid(g_ref[...].astype(F32) + bg_ref[:, k * d:(k + 1) * d])

    merged = (gate(g0_ref, 0) * jnp.dot(ym_ref[...], wm_ref[...], preferred_element_type=F32)
              + gate(g1_ref, 1) * jnp.dot(ya_ref[...], wa_ref[...], preferred_element_type=F32)
              + gate(g2_ref, 2) * jnp.dot(yx, wx_ref[...], preferred_element_type=F32))
    o_ref[...] = x_ref[...] + jnp.dot(merged.astype(BF16), wo_ref[...], preferred_element_type=F32)


def _merge(x, proj, qx_col, g_col, y_m, y_a, kx, vx, b_gate, w_m, w_a, w_x, w_o, *, batch, seq):
    n, d = x.shape
    tm = min(MERGE_ROWS, seq)
    nt = seq // tm

    def row(b, t):
        return b * nt + t

    def proj_spec(col):
        return pl.BlockSpec((tm, d), lambda b, t: (row(b, t), col))

    def mem_spec(a):
        return pl.BlockSpec((1,) + a.shape[1:], lambda b, t: (b, 0, 0))

    return pl.pallas_call(
        _merge_kernel,
        grid=(batch, nt),
        in_specs=[
            pl.BlockSpec((tm, d), lambda b, t: (row(b, t), 0)),
            proj_spec(qx_col), proj_spec(g_col), proj_spec(g_col + 1), proj_spec(g_col + 2),
            pl.BlockSpec((tm, y_m.shape[1]), lambda b, t: (row(b, t), 0)),
            pl.BlockSpec((tm, y_a.shape[1]), lambda b, t: (row(b, t), 0)),
            mem_spec(kx), mem_spec(vx),
            _resident((1, b_gate.shape[0])),
            _resident(w_m.shape), _resident(w_a.shape), _resident(w_x.shape), _resident(w_o.shape),
        ],
        out_specs=pl.BlockSpec((tm, d), lambda b, t: (row(b, t), 0)),
        out_shape=jax.ShapeDtypeStruct((n, d), F32),
        compiler_params=_params("parallel", "parallel"),
        name="merge",
    )(x, proj, proj, proj, proj, y_m, y_a, kx, vx, b_gate.reshape(1, -1), w_m, w_a, w_x, w_o)


def kernel(x, mem, positions, rel_bias, ffn1_norm, ffn1_w_gu, ffn1_w_down, mix_norm, mem_norm, w_in, b_if, b_gate, m_conv_w, m_conv_b, m_wq, m_wk, m_wv, m_head_norm, a_lam_q1, a_lam_k1, a_lam_q2, a_lam_k2, a_head_norm, w_mem_kv, w_branch_m, w_branch_a, w_branch_x, w_out, ffn2_norm, ffn2_w_gu, ffn2_w_down, final_norm):
    del positions
    batch, seq, d = x.shape
    n = batch * seq
    l = LAYER
    m_inner = m_conv_w.shape[-1]
    a_width = a_head_norm.shape[-1]
    x_width = w_mem_kv.shape[-1] // 2
    bf = lambda a: a.astype(BF16)

    gate_lo = 2 * m_inner
    gate_hi = gate_lo + 2 * M_HEADS
    w_all = bf(w_in[l])
    w_tail = w_all[:, gate_hi:]
    w_gate = jnp.pad(w_all[:, gate_lo:gate_hi], ((0, 0), (0, LANES - 2 * M_HEADS)))

    x0 = x.reshape(n, d)
    x1 = _ffn(x0, ffn1_norm[l], bf(ffn1_w_gu[l]), bf(ffn1_w_down[l]), final_norm, final_norm=False)

    proj, gates = _inproj(x1, mix_norm[l], w_all, gate_lo, w_tail, w_gate)

    m_hd = m_inner // M_HEADS
    q_m, k_m, v_m = _mlstm_qkv(proj, m_conv_w[l], m_conv_b[l], bf(m_wq[l]), bf(m_wk[l] * m_hd ** -0.5),
                               bf(m_wv[l]), batch=batch, seq=seq)
    gates_t = gates[:, :2 * M_HEADS].reshape(batch, seq, 2, M_HEADS).transpose(0, 3, 2, 1)
    y_m = _mlstm_scan(q_m, k_m, v_m, gates_t, b_if[l], proj, m_inner // m_hd, m_head_norm[l],
                      batch=batch, seq=seq)

    a_hw = a_width // A_HEADS
    qa_col = 2 * m_inner // a_hw
    lam_vecs = jnp.stack([a_lam_q1[l], a_lam_k1[l], a_lam_q2[l], a_lam_k2[l]])
    y_a = _diff_attn(proj, qa_col, qa_col + A_HEADS, qa_col + 2 * A_HEADS, rel_bias, lam_vecs,
                     a_head_norm[l], batch=batch, seq=seq)

    kx, vx = _mem_kv(mem, mem_norm[l], bf(w_mem_kv[l]))
    qx_col = (2 * m_inner + 3 * a_width) // d
    x2 = _merge(x1, proj, qx_col, qx_col + x_width // d, y_m, y_a, kx, vx, b_gate[l],
                bf(w_branch_m[l]), bf(w_branch_a[l]), bf(w_branch_x[l]), bf(w_out[l]),
                batch=batch, seq=seq)

    out = _ffn(x2, ffn2_norm[l], bf(ffn2_w_gu[l]), bf(ffn2_w_down[l]), final_norm, final_norm=True)
    return out.reshape(batch, seq, d)
```

```python
import functools
import math

import jax
import jax.numpy as jnp
from jax import lax
from jax.experimental import pallas as pl
from jax.experimental.pallas import tpu as pltpu

EPS = 1e-6
BF16 = jnp.bfloat16
F32 = jnp.float32
NEG_BIG = -1e30

M_HEADS = 4
M_CONV = 4
A_HEADS = 8
X_HEADS = 4
N_BRANCH = 3
REL_BUCKETS = 32
REL_MAX_DIST = 128
LAYER = 0

VMEM_LIMIT_BYTES = 56 * 1024 * 1024

FFN_ROWS = 512
INPROJ_ROWS = 2048
INPROJ_COLS = 1024
QKV_ROWS = 4096
QKV_SUBTILES = 16
MLSTM_CHUNK = 256
MLSTM_HEAD_GROUP = 2
ATTN_TILE = 256
ATTN_HEAD_GROUP = 4
LANES = 128
ATTN_STRIP = LANES
ATTN_PV_COLS = 256
ATTN_ONES_ROWS = 16
LOG2_E = math.log2(math.e)
MERGE_ROWS = 512
CONV_HALO = 16


def _params(*sem):
    return pltpu.CompilerParams(dimension_semantics=sem, vmem_limit_bytes=VMEM_LIMIT_BYTES)


def _resident(shape):
    zeros = (0,) * len(shape)
    return pl.BlockSpec(shape, lambda *_: zeros, pipeline_mode=pl.Buffered(1))


def _rms(x):
    return x * lax.rsqrt(jnp.mean(x * x, axis=-1, keepdims=True) + EPS)


def _sigmoid(x):
    return 1.0 / (1.0 + jnp.exp(-x))


def _ffn_kernel(x_ref, g_ref, wgu_ref, wd_ref, fg_ref, o_ref, *, d_ff, final_norm):
    x = x_ref[...]
    hb = (_rms(x) * g_ref[...]).astype(BF16)
    gate = jnp.dot(hb, wgu_ref[:, :d_ff], preferred_element_type=F32)
    up = jnp.dot(hb, wgu_ref[:, d_ff:], preferred_element_type=F32)
    act = (gate * _sigmoid(gate) * up).astype(BF16)
    y = x + 0.5 * jnp.dot(act, wd_ref[...], preferred_element_type=F32)
    if final_norm:
        y = _rms(y) * fg_ref[...]
    o_ref[...] = y


def _ffn(x, gain, w_gu, w_down, final_gain, *, final_norm):
    n, d = x.shape
    d_ff = w_down.shape[0]
    tm = min(FFN_ROWS, n)
    return pl.pallas_call(
        functools.partial(_ffn_kernel, d_ff=d_ff, final_norm=final_norm),
        grid=(n // tm,),
        in_specs=[
            pl.BlockSpec((tm, d), lambda i: (i, 0)),
            _resident((1, d)),
            _resident((d, 2 * d_ff)),
            _resident((d_ff, d)),
            _resident((1, d)),
        ],
        out_specs=pl.BlockSpec((tm, d), lambda i: (i, 0)),
        out_shape=jax.ShapeDtypeStruct((n, d), F32),
        compiler_params=_params("parallel"),
        name="ffn_final" if final_norm else "ffn",
    )(x, gain.reshape(1, d), w_gu, w_down, final_gain.reshape(1, d))


def _inproj_kernel(x_ref, g_ref, w_head_ref, w_tail_ref, wgate_ref, p_ref, gates_ref, h_scr, *, head_blocks):
    j = pl.program_id(1)

    @pl.when(j == 0)
    def _():
        hb = (_rms(x_ref[...]) * g_ref[...]).astype(BF16)
        h_scr[...] = hb
        gates_ref[...] = jnp.dot(hb, wgate_ref[...], preferred_element_type=F32)

    @pl.when(j < head_blocks)
    def _():
        p_ref[...] = jnp.dot(h_scr[...], w_head_ref[...], preferred_element_type=F32).astype(BF16)

    @pl.when(j >= head_blocks)
    def _():
        p_ref[...] = jnp.dot(h_scr[...], w_tail_ref[...], preferred_element_type=F32).astype(BF16)


def _inproj(x, gain, w_all, head_width, w_tail, w_gate):
    n, d = x.shape
    tm = min(INPROJ_ROWS, n)
    tn = INPROJ_COLS
    head_blocks = head_width // tn
    tail_blocks = w_tail.shape[1] // tn
    width = head_width + w_tail.shape[1]
    return pl.pallas_call(
        functools.partial(_inproj_kernel, head_blocks=head_blocks),
        grid=(n // tm, width // tn),
        in_specs=[
            pl.BlockSpec((tm, d), lambda i, j: (i, 0)),
            _resident((1, d)),
            pl.BlockSpec((d, tn), lambda i, j: (0, jnp.minimum(j, head_blocks - 1))),
            pl.BlockSpec((d, tn), lambda i, j: (0, jnp.where(j < head_blocks, tail_blocks - 1, j - head_blocks))),
            _resident(w_gate.shape),
        ],
        out_specs=[
            pl.BlockSpec((tm, tn), lambda i, j: (i, j)),
            pl.BlockSpec((tm, w_gate.shape[1]), lambda i, j: (i, 0)),
        ],
        out_shape=[
            jax.ShapeDtypeStruct((n, width), BF16),
            jax.ShapeDtypeStruct((n, w_gate.shape[1]), F32),
        ],
        scratch_shapes=[pltpu.VMEM((tm, d), BF16)],
        compiler_params=_params("parallel", "arbitrary"),
        name="inproj",
    )(x, gain.reshape(1, d), w_all, w_tail, w_gate)


def _mlstm_qkv_kernel(x_ref, halo_ref, cw_ref, cb_ref, wq_ref, wk_ref, wv_ref,
                      q_ref, k_ref, v_ref):
    ts = x_ref.shape[0]
    xb = x_ref[...]
    v_ref[...] = jnp.dot(xb, wv_ref[0], preferred_element_type=F32).astype(BF16)
    prev = jnp.where(pl.program_id(2) > 0, halo_ref[...].astype(F32), 0.0)
    full = jnp.concatenate([prev, xb.astype(F32)], axis=0)
    cw = cw_ref[...]
    sub = ts // QKV_SUBTILES
    for r0 in range(0, ts, sub):
        conv = cb_ref[...]
        for back in range(M_CONV):
            tap = M_CONV - 1 - back
            start = CONV_HALO + r0 - back
            conv = conv + cw[tap:tap + 1, :] * full[start:start + sub, :]
        xc = (conv * _sigmoid(conv)).astype(BF16)
        q_ref[r0:r0 + sub, :] = jnp.dot(xc, wq_ref[0], preferred_element_type=F32).astype(BF16)
        k_ref[r0:r0 + sub, :] = jnp.dot(xc, wk_ref[0], preferred_element_type=F32).astype(BF16)


def _mlstm_qkv(proj, conv_w, conv_b, wq, wk, wv, *, batch, seq):
    heads, hd, _ = wq.shape
    n = batch * seq
    ts = min(QKV_ROWS, seq)
    nt = seq // ts
    halo_per_tile = ts // CONV_HALO

    def x_map(h, b, t):
        return (b * nt + t, h)

    def halo_map(h, b, t):
        return (jnp.maximum((b * nt + t) * halo_per_tile - 1, 0), h)

    def w_map(h, b, t):
        return (h, 0, 0)

    out = jax.ShapeDtypeStruct((n, heads * hd), BF16)
    return pl.pallas_call(
        _mlstm_qkv_kernel,
        grid=(heads, batch, nt),
        in_specs=[
            pl.BlockSpec((ts, hd), x_map),
            pl.BlockSpec((CONV_HALO, hd), halo_map),
            pl.BlockSpec((M_CONV, hd), lambda h, b, t: (0, h)),
            pl.BlockSpec((1, hd), lambda h, b, t: (0, h)),
            pl.BlockSpec((1, hd, hd), w_map),
            pl.BlockSpec((1, hd, hd), w_map),
            pl.BlockSpec((1, hd, hd), w_map),
        ],
        out_specs=[pl.BlockSpec((ts, hd), x_map)] * 3,
        out_shape=[out, out, out],
        compiler_params=_params("parallel", "parallel", "parallel"),
        name="mlstm_qkv",
    )(proj, proj, conv_w, conv_b.reshape(1, -1), wq, wk, wv)


def _mlstm_scan_kernel(q_ref, k_ref, v_ref, g_ref, bias_ref, o_ref, gain_ref, y_ref,
                       c_scr, n_scr, m_scr):
    @pl.when(pl.program_id(2) == 0)
    def _():
        c_scr[...] = jnp.zeros_like(c_scr)
        n_scr[...] = jnp.zeros_like(n_scr)
        m_scr[...] = jnp.zeros_like(m_scr)

    group, hd, _ = c_scr.shape
    chunk = q_ref.shape[0]
    heads = range(group)
    cols = [slice(g * hd, (g + 1) * hd) for g in heads]
    q = [q_ref[:, cols[g]] for g in heads]
    k = [k_ref[:, cols[g]] for g in heads]
    v = [v_ref[:, cols[g]] for g in heads]
    c_prev = [c_scr[g] for g in heads]
    n_prev = [n_scr[g] for g in heads]
    m_prev = [m_scr[g] for g in heads]

    qk = [lax.dot_general(q[g], k[g], (((1,), (1,)), ((), ())), preferred_element_type=F32) for g in heads]
    qc = [jnp.dot(q[g], c_prev[g].astype(BF16), preferred_element_type=F32) for g in heads]

    t_idx = lax.broadcasted_iota(jnp.int32, (chunk, chunk), 0)
    s_idx = lax.broadcasted_iota(jnp.int32, (chunk, chunk), 1)
    causal = s_idx <= t_idx
    diag = s_idx == t_idx

    def to_col(row):
        return jnp.sum(jnp.where(diag, row, 0.0), axis=1, keepdims=True)

    def to_row(col):
        return jnp.sum(jnp.where(diag, col, 0.0), axis=0, keepdims=True)

    s, w_inter, m_t, w_s, decay, m_new = [], [], [], [], [], []
    for g in heads:
        gates = g_ref[0, g] + bias_ref[g]
        ig = gates[0:1, :]
        fpre = gates[1:2, :]
        lf = jnp.minimum(fpre, 0.0) - jnp.log1p(jnp.exp(-jnp.abs(fpre)))
        b_col = jnp.sum(jnp.where(causal, lf, 0.0), axis=1, keepdims=True)
        b_row = to_row(b_col)
        dlog = jnp.where(causal, b_col - b_row + ig, NEG_BIG)
        inter = b_col + m_prev[g]
        m_t.append(jnp.maximum(inter, jnp.max(dlog, axis=1, keepdims=True)))
        s.append(qk[g] * jnp.exp(dlog - m_t[g]))
        w_inter.append(jnp.exp(inter - m_t[g]))
        b_last = b_col[chunk - 1:chunk, :]
        tail = b_last - b_row + ig
        m_new.append(jnp.maximum(b_last + m_prev[g], jnp.max(tail, axis=1, keepdims=True)))
        decay.append(jnp.exp(b_last + m_prev[g] - m_new[g]))
        w_s.append(jnp.exp(to_col(tail) - m_new[g]))

    sv = [jnp.dot(s[g].astype(BF16), v[g], preferred_element_type=F32) for g in heads]
    kw = [k[g].astype(F32) * w_s[g] for g in heads]
    kv = [jnp.dot(kw[g].T.astype(BF16), v[g], preferred_element_type=F32) for g in heads]

    for g in heads:
        num = sv[g] + w_inter[g] * qc[g]
        qn = jnp.sum(q[g].astype(F32) * n_prev[g], axis=1, keepdims=True)
        den = jnp.sum(s[g], axis=1, keepdims=True) + w_inter[g] * qn
        h = num / jnp.maximum(jnp.abs(den), jnp.exp(-m_t[g]))
        c_scr[g] = decay[g] * c_prev[g] + kv[g]
        n_scr[g] = decay[g] * n_prev[g] + jnp.sum(kw[g], axis=0, keepdims=True)
        m_scr[g] = m_new[g]
        gate = _sigmoid(o_ref[:, cols[g]].astype(F32))
        y_ref[:, cols[g]] = (gate * (_rms(h) * gain_ref[:, cols[g]])).astype(BF16)


def _mlstm_scan(q, k, v, gates_t, b_if, proj, o_col_block, head_gain, *, batch, seq):
    n, width = q.shape
    heads = gates_t.shape[1]
    hd = width // heads
    group = MLSTM_HEAD_GROUP
    gw = group * hd
    assert heads % group == 0 and o_col_block % group == 0
    chunk = min(MLSTM_CHUNK, seq)
    nc = seq // chunk

    def qkv_map(b, h, c):
        return (b * nc + c, h)

    return pl.pallas_call(
        _mlstm_scan_kernel,
        grid=(batch, heads // group, nc),
        in_specs=[
            pl.BlockSpec((chunk, gw), qkv_map),
            pl.BlockSpec((chunk, gw), qkv_map),
            pl.BlockSpec((chunk, gw), qkv_map),
            pl.BlockSpec((1, group, 2, chunk), lambda b, h, c: (b, h, 0, c)),
            pl.BlockSpec((group, 2, 1), lambda b, h, c: (h, 0, 0)),
            pl.BlockSpec((chunk, gw), lambda b, h, c: (b * nc + c, o_col_block // group + h)),
            pl.BlockSpec((1, gw), lambda b, h, c: (0, h)),
        ],
        out_specs=pl.BlockSpec((chunk, gw), qkv_map),
        out_shape=jax.ShapeDtypeStruct((n, width), BF16),
        scratch_shapes=[pltpu.VMEM((group, hd, hd), F32), pltpu.VMEM((group, 1, hd), F32),
                        pltpu.VMEM((group, 1, 1), F32)],
        compiler_params=_params("parallel", "parallel", "arbitrary"),
        name="mlstm_scan",
    )(q, k, v, gates_t, b_if.reshape(2, heads, 1).transpose(1, 0, 2), proj, head_gain.reshape(1, -1))


def _diff_attn_kernel(far_ref, q_ref, k_ref, v_ref, band_ref, lam_ref, gain_ref, y_ref,
                      vt_scr, acc_scr, st_scr, *, out_scale, lam_init):
    group, n_tiles, rows, tile = vt_scr.shape
    width = q_ref.shape[1] // group
    head0 = pl.program_id(1) * group
    qi = pl.program_id(2)
    half = width // 2

    @pl.when(qi == 0)
    def _():
        ones = jnp.ones((rows - width, tile), BF16)
        for g in range(group):
            for j in range(n_tiles):
                v_tile = v_ref[j * tile:(j + 1) * tile, g * width:(g + 1) * width]
                vt_scr[g, j] = jnp.concatenate([v_tile.astype(F32).T.astype(BF16), ones], axis=0)

    q_all = q_ref[...]
    lane = lax.broadcasted_iota(jnp.int32, (tile, width), 1)
    qs = []
    for g in range(group):
        q = q_all[:, g * width:(g + 1) * width].astype(F32) * (half ** -0.5 * LOG2_E)
        stacked = jnp.concatenate([jnp.where(lane < half, q, 0.0), jnp.where(lane >= half, q, 0.0)], axis=0)
        qs.append(stacked.astype(BF16))

    acc_scr[...] = jnp.zeros_like(acc_scr)
    lam_v = lam_ref[...]
    lam = (jnp.exp(jnp.sum(lam_v[0:1, :] * lam_v[1:2, :], axis=1, keepdims=True))
           - jnp.exp(jnp.sum(lam_v[2:3, :] * lam_v[3:4, :], axis=1, keepdims=True)) + lam_init)

    def scores(kj, g):
        start = pl.multiple_of(kj * tile, tile)
        return lax.dot_general(k_ref[pl.ds(start, tile), g * width:(g + 1) * width], qs[g],
                               (((1,), (1,)), ((), ())), preferred_element_type=F32)

    def block(tiles, maxes):
        units = [(kj, band_idx, g) for kj, band_idx in tiles for g in range(group)]
        all_scores, parked = [], 0
        for kj, band_idx, g in units:
            if band_idx is None:
                all_scores.append(scores(kj, g))
            else:
                st_scr[parked] = scores(kj, g)
                all_scores.append(parked)
                parked += 1
        maxes = list(maxes)
        for st, (kj, band_idx, g) in zip(all_scores, units):
            m_prev = maxes[g]
            if band_idx is None:
                shift = far_bias[g]
            else:
                shift = 0.0
                band = band_ref[g, band_idx]
                st = st_scr[st] + jnp.concatenate([band, band], axis=1)
            vt = vt_scr[g, kj]
            m_cols = []
            for c0 in range(0, 2 * tile, ATTN_PV_COLS):
                p_cols, a_cols = [], []
                for c in range(c0, c0 + ATTN_PV_COLS, ATTN_STRIP):
                    cols = slice(c, c + ATTN_STRIP)
                    s_c = st[:, cols]
                    m_c = jnp.maximum(m_prev[:, cols], jnp.max(s_c, axis=0, keepdims=True) + shift)
                    a_cols.append(jnp.exp2(m_prev[:, cols] - m_c))
                    p_cols.append(jnp.exp2(s_c - (m_c - shift)).astype(BF16))
                    m_cols.append(m_c)
                pv = jnp.dot(vt, jnp.concatenate(p_cols, axis=1), preferred_element_type=F32)
                wide = slice(c0, c0 + ATTN_PV_COLS)
                acc_scr[g, :, wide] = jnp.concatenate(a_cols, axis=1) * acc_scr[g, :, wide] + pv
            maxes[g] = jnp.concatenate(m_cols, axis=1)
        return tuple(maxes)

    def far(first, count):
        return tuple((first + i, None) for i in range(count))

    far_bias = tuple(far_ref[head0 + g] * LOG2_E for g in range(group))
    n_far = jnp.maximum(qi - 1, 0)
    n_quads = n_far // 4
    rest = n_far - 4 * n_quads
    maxes = tuple(jnp.full((1, 2 * tile), NEG_BIG, F32) for _ in range(group))
    maxes = lax.fori_loop(0, n_quads, lambda i, c: block(far(4 * i, 4), c), maxes)
    maxes = lax.cond(rest >= 2, lambda: block(far(4 * n_quads, 2), maxes), lambda: maxes)
    near = ((qi - 1, 0), (qi, 1))

    def last_tiles():
        return lax.cond(rest % 2 == 1, lambda: block(far(n_far - 1, 1) + near, maxes), lambda: block(near, maxes))

    lax.cond(qi > 0, last_tiles, lambda: block(((qi, 1),), maxes))

    for g in range(group):
        o = acc_scr[g, :width, :] / acc_scr[g, width:width + 1, :]
        out = (o[:, :tile] - lam * o[:, tile:]).T
        gain = gain_ref[:, g * width:(g + 1) * width]
        y_ref[:, g * width:(g + 1) * width] = (_rms(out) * gain * out_scale).astype(BF16)


def _t5_bucket(n):
    max_exact = REL_BUCKETS // 2
    nf = jnp.maximum(n, 1).astype(F32)
    large = max_exact + (jnp.log(nf / max_exact) / math.log(REL_MAX_DIST / max_exact)
                         * (REL_BUCKETS - max_exact)).astype(jnp.int32)
    large = jnp.minimum(large, REL_BUCKETS - 1)
    return jnp.where(n < max_exact, n, large)


def _diff_attn(proj, q_col, k_col, v_col, rel_bias, lam_vecs, head_gain, *, batch, seq):
    n = proj.shape[0]
    heads = rel_bias.shape[1]
    hw = head_gain.shape[0] // heads
    tile = min(ATTN_TILE, seq)
    nq = seq // tile
    assert tile >= REL_MAX_DIST, "distances beyond one tile must all fall in the last bucket"
    group = ATTN_HEAD_GROUP
    gw = group * hw
    assert heads % group == 0 and q_col % group == 0 and k_col % group == 0 and v_col % group == 0

    bucket = _t5_bucket(jnp.arange(2 * tile, dtype=jnp.int32))
    by_dist = jnp.zeros((heads, 2 * tile), F32)
    for b in range(REL_BUCKETS):
        by_dist = jnp.where(bucket[None] == b, rel_bias[b].astype(F32)[:, None], by_dist)
    masked = jnp.full((heads, tile - 1), NEG_BIG, F32)
    line = jnp.concatenate([masked, by_dist * LOG2_E, masked[:, :2]], axis=1)
    length = line.shape[1] - 1
    skew = jnp.tile(line, (1, tile))[:, :tile * length].reshape(heads, tile, length)
    band = jnp.stack([skew[:, :, 2 * tile - 1:3 * tile - 1], skew[:, :, tile - 1:2 * tile - 1]], axis=1)
    far = rel_bias[REL_BUCKETS - 1].astype(F32)

    lam_init = 0.8 - 0.6 * math.exp(-0.3 * LAYER)
    return pl.pallas_call(
        functools.partial(_diff_attn_kernel, out_scale=1.0 - lam_init, lam_init=lam_init),
        grid=(batch, heads // group, nq),
        in_specs=[
            pl.BlockSpec(memory_space=pltpu.SMEM),
            pl.BlockSpec((tile, gw), lambda b, h, i: (b * nq + i, q_col // group + h)),
            pl.BlockSpec((seq, gw), lambda b, h, i: (b, k_col // group + h)),
            pl.BlockSpec((seq, gw), lambda b, h, i: (b, v_col // group + h)),
            pl.BlockSpec((group, 2, tile, tile), lambda b, h, i: (h, 0, 0, 0)),
            _resident(lam_vecs.shape),
            pl.BlockSpec((1, gw), lambda b, h, i: (0, h)),
        ],
        out_specs=pl.BlockSpec((tile, gw), lambda b, h, i: (b * nq + i, h)),
        out_shape=jax.ShapeDtypeStruct((n, heads * hw), BF16),
        scratch_shapes=[
            pltpu.VMEM((group, nq, hw + ATTN_ONES_ROWS, tile), BF16),
            pltpu.VMEM((group, hw + ATTN_ONES_ROWS, 2 * tile), F32),
            pltpu.VMEM((2 * group, tile, 2 * tile), F32),
        ],
        compiler_params=_params("parallel", "parallel", "arbitrary"),
        name="diff_attn",
    )(far, proj, proj, proj, band, lam_vecs, head_gain.reshape(1, -1))


def _mem_kv_kernel(mem_ref, g_ref, w_ref, k_ref, v_ref):
    width = k_ref.shape[-1]
    hb = (_rms(mem_ref[0]) * g_ref[...]).astype(BF16)
    k_ref[0] = jnp.dot(hb, w_ref[:, :width], preferred_element_type=F32).astype(BF16)
    v_ref[0] = jnp.dot(hb, w_ref[:, width:], preferred_element_type=F32).astype(BF16)


def _mem_kv(mem, gain, w_kv):
    batch, mlen, d = mem.shape
    width = w_kv.shape[1] // 2
    out = jax.ShapeDtypeStruct((batch, mlen, width), BF16)
    return pl.pallas_call(
        _mem_kv_kernel,
        grid=(batch,),
        in_specs=[
            pl.BlockSpec((1, mlen, d), lambda b: (b, 0, 0)),
            _resident((1, d)),
            _resident(w_kv.shape),
        ],
        out_specs=[pl.BlockSpec((1, mlen, width), lambda b: (b, 0, 0))] * 2,
        out_shape=[out, out],
        compiler_params=_params("parallel"),
        name="mem_kv",
    )(mem, gain.reshape(1, d), w_kv)


def _merge_kernel(x_ref, qx_ref, g0_ref, g1_ref, g2_ref, ym_ref, ya_ref, kx_ref, vx_ref,
                  bg_ref, wm_ref, wa_ref, wx_ref, wo_ref, o_ref):
    d = x_ref.shape[1]
    xw = qx_ref.shape[1]
    xhd = xw // X_HEADS
    heads = []
    for h in range(X_HEADS):
        cols = slice(h * xhd, (h + 1) * xhd)
        s = lax.dot_general(qx_ref[:, cols], kx_ref[0, :, cols], (((1,), (1,)), ((), ())),
                            preferred_element_type=F32) * (xhd ** -0.5)
        p = jnp.exp(s - jnp.max(s, axis=1, keepdims=True))
        pv = jnp.dot(p.astype(BF16), vx_ref[0, :, cols], preferred_element_type=F32)
        heads.append((pv / jnp.sum(p, axis=1, keepdims=True)).astype(BF16))
    yx = jnp.concatenate(heads, axis=1)

    def gate(g_ref, k):
        return _sigmoid(g_ref[...].astype(F32) + bg_ref[:, k * d:(k + 1) * d])

    merged = (gate(g0_ref, 0) * jnp.dot(ym_ref[...], wm_ref[...], preferred_element_type=F32)
              + gate(g1_ref, 1) * jnp.dot(ya_ref[...], wa_ref[...], preferred_element_type=F32)
              + gate(g2_ref, 2) * jnp.dot(yx, wx_ref[...], preferred_element_type=F32))
    o_ref[...] = x_ref[...] + jnp.dot(merged.astype(BF16), wo_ref[...], preferred_element_type=F32)


def _merge(x, proj, qx_col, g_col, y_m, y_a, kx, vx, b_gate, w_m, w_a, w_x, w_o, *, batch, seq):
    n, d = x.shape
    tm = min(MERGE_ROWS, seq)
    nt = seq // tm

    def row(b, t):
        return b * nt + t

    def proj_spec(col):
        return pl.BlockSpec((tm, d), lambda b, t: (row(b, t), col))

    def mem_spec(a):
        return pl.BlockSpec((1,) + a.shape[1:], lambda b, t: (b, 0, 0))

    return pl.pallas_call(
        _merge_kernel,
        grid=(batch, nt),
        in_specs=[
            pl.BlockSpec((tm, d), lambda b, t: (row(b, t), 0)),
            proj_spec(qx_col), proj_spec(g_col), proj_spec(g_col + 1), proj_spec(g_col + 2),
            pl.BlockSpec((tm, y_m.shape[1]), lambda b, t: (row(b, t), 0)),
            pl.BlockSpec((tm, y_a.shape[1]), lambda b, t: (row(b, t), 0)),
            mem_spec(kx), mem_spec(vx),
            _resident((1, b_gate.shape[0])),
            _resident(w_m.shape), _resident(w_a.shape), _resident(w_x.shape), _resident(w_o.shape),
        ],
        out_specs=pl.BlockSpec((tm, d), lambda b, t: (row(b, t), 0)),
        out_shape=jax.ShapeDtypeStruct((n, d), F32),
        compiler_params=_params("parallel", "parallel"),
        name="merge",
    )(x, proj, proj, proj, proj, y_m, y_a, kx, vx, b_gate.reshape(1, -1), w_m, w_a, w_x, w_o)


def kernel(x, mem, positions, rel_bias, ffn1_norm, ffn1_w_gu, ffn1_w_down, mix_norm, mem_norm, w_in, b_if, b_gate, m_conv_w, m_conv_b, m_wq, m_wk, m_wv, m_head_norm, a_lam_q1, a_lam_k1, a_lam_q2, a_lam_k2, a_head_norm, w_mem_kv, w_branch_m, w_branch_a, w_branch_x, w_out, ffn2_norm, ffn2_w_gu, ffn2_w_down, final_norm):
    del positions
    batch, seq, d = x.shape
    n = batch * seq
    l = LAYER
    m_inner = m_conv_w.shape[-1]
    a_width = a_head_norm.shape[-1]
    x_width = w_mem_kv.shape[-1] // 2
    bf = lambda a: a.astype(BF16)

    gate_lo = 2 * m_inner
    gate_hi = gate_lo + 2 * M_HEADS
    w_all = bf(w_in[l])
    w_tail = w_all[:, gate_hi:]
    w_gate = jnp.pad(w_all[:, gate_lo:gate_hi], ((0, 0), (0, LANES - 2 * M_HEADS)))

    x0 = x.reshape(n, d)
    x1 = _ffn(x0, ffn1_norm[l], bf(ffn1_w_gu[l]), bf(ffn1_w_down[l]), final_norm, final_norm=False)

    proj, gates = _inproj(x1, mix_norm[l], w_all, gate_lo, w_tail, w_gate)

    m_hd = m_inner // M_HEADS
    q_m, k_m, v_m = _mlstm_qkv(proj, m_conv_w[l], m_conv_b[l], bf(m_wq[l]), bf(m_wk[l] * m_hd ** -0.5),
                               bf(m_wv[l]), batch=batch, seq=seq)
    gates_t = gates[:, :2 * M_HEADS].reshape(batch, seq, 2, M_HEADS).transpose(0, 3, 2, 1)
    y_m = _mlstm_scan(q_m, k_m, v_m, gates_t, b_if[l], proj, m_inner // m_hd, m_head_norm[l],
                      batch=batch, seq=seq)

    a_hw = a_width // A_HEADS
    qa_col = 2 * m_inner // a_hw
    lam_vecs = jnp.stack([a_lam_q1[l], a_lam_k1[l], a_lam_q2[l], a_lam_k2[l]])
    y_a = _diff_attn(proj, qa_col, qa_col + A_HEADS, qa_col + 2 * A_HEADS, rel_bias, lam_vecs,
                     a_head_norm[l], batch=batch, seq=seq)

    kx, vx = _mem_kv(mem, mem_norm[l], bf(w_mem_kv[l]))
    qx_col = (2 * m_inner + 3 * a_width) // d
    x2 = _merge(x1, proj, qx_col, qx_col + x_width // d, y_m, y_a, kx, vx, b_gate[l],
                bf(w_branch_m[l]), bf(w_branch_a[l]), bf(w_branch_x[l]), bf(w_out[l]),
                batch=batch, seq=seq)

    out = _ffn(x2, ffn2_norm[l], bf(ffn2_w_gu[l]), bf(ffn2_w_down[l]), final_norm, final_norm=True)
    return out.reshape(batch, seq, d)
```

```python
import functools
import math

import jax
import jax.numpy as jnp
from jax import lax
from jax.experimental import pallas as pl
from jax.experimental.pallas import tpu as pltpu

EPS = 1e-6
BF16 = jnp.bfloat16
F32 = jnp.float32
NEG_BIG = -1e30

M_HEADS = 4
M_CONV = 4
A_HEADS = 8
X_HEADS = 4
N_BRANCH = 3
REL_BUCKETS = 32
REL_MAX_DIST = 128
LAYER = 0

VMEM_LIMIT_BYTES = 56 * 1024 * 1024

FFN_ROWS = 1024
FFN_SUBTILES = 4
INPROJ_ROWS = 2048
INPROJ_COLS = 1024
QKV_ROWS = 4096
QKV_SUBTILES = 16
MLSTM_CHUNK = 256
MLSTM_HEAD_GROUP = 2
ATTN_TILE = 256
ATTN_HEAD_GROUP = 4
LANES = 128
ATTN_STRIP = LANES
ATTN_PV_COLS = 256
ATTN_ONES_ROWS = 16
LOG2_E = math.log2(math.e)
MERGE_ROWS = 512
CONV_HALO = 16


def _params(*sem):
    return pltpu.CompilerParams(dimension_semantics=sem, vmem_limit_bytes=VMEM_LIMIT_BYTES)


def _resident(shape):
    zeros = (0,) * len(shape)
    return pl.BlockSpec(shape, lambda *_: zeros, pipeline_mode=pl.Buffered(1))


def _rms(x):
    return x * lax.rsqrt(jnp.mean(x * x, axis=-1, keepdims=True) + EPS)


def _sigmoid(x):
    return 1.0 / (1.0 + jnp.exp(-x))


def _ffn_kernel(x_ref, g_ref, wgu_ref, wd_ref, fg_ref, o_ref, *, d_ff, final_norm):
    sub = x_ref.shape[0] // FFN_SUBTILES
    for r0 in range(0, x_ref.shape[0], sub):
        x = x_ref[r0:r0 + sub, :]
        hb = (_rms(x) * g_ref[...]).astype(BF16)
        gate = jnp.dot(hb, wgu_ref[:, :d_ff], preferred_element_type=F32)
        up = jnp.dot(hb, wgu_ref[:, d_ff:], preferred_element_type=F32)
        act = (gate * _sigmoid(gate) * up).astype(BF16)
        y = x + 0.5 * jnp.dot(act, wd_ref[...], preferred_element_type=F32)
        if final_norm:
            y = _rms(y) * fg_ref[...]
        o_ref[r0:r0 + sub, :] = y


def _ffn(x, gain, w_gu, w_down, final_gain, *, final_norm):
    n, d = x.shape
    d_ff = w_down.shape[0]
    tm = min(FFN_ROWS, n)
    return pl.pallas_call(
        functools.partial(_ffn_kernel, d_ff=d_ff, final_norm=final_norm),
        grid=(n // tm,),
        in_specs=[
            pl.BlockSpec((tm, d), lambda i: (i, 0)),
            _resident((1, d)),
            _resident((d, 2 * d_ff)),
            _resident((d_ff, d)),
            _resident((1, d)),
        ],
        out_specs=pl.BlockSpec((tm, d), lambda i: (i, 0)),
        out_shape=jax.ShapeDtypeStruct((n, d), F32),
        compiler_params=_params("parallel"),
        name="ffn_final" if final_norm else "ffn",
    )(x, gain.reshape(1, d), w_gu, w_down, final_gain.reshape(1, d))


def _inproj_kernel(x_ref, g_ref, w_head_ref, w_tail_ref, wgate_ref, p_ref, gates_ref, h_scr, *, head_blocks):
    j = pl.program_id(1)

    @pl.when(j == 0)
    def _():
        hb = (_rms(x_ref[...]) * g_ref[...]).astype(BF16)
        h_scr[...] = hb
        gates_ref[...] = jnp.dot(hb, wgate_ref[...], preferred_element_type=F32)

    @pl.when(j < head_blocks)
    def _():
        p_ref[...] = jnp.dot(h_scr[...], w_head_ref[...], preferred_element_type=F32).astype(BF16)

    @pl.when(j >= head_blocks)
    def _():
        p_ref[...] = jnp.dot(h_scr[...], w_tail_ref[...], preferred_element_type=F32).astype(BF16)


def _inproj(x, gain, w_all, head_width, w_tail, w_gate):
    n, d = x.shape
    tm = min(INPROJ_ROWS, n)
    tn = INPROJ_COLS
    head_blocks = head_width // tn
    tail_blocks = w_tail.shape[1] // tn
    width = head_width + w_tail.shape[1]
    return pl.pallas_call(
        functools.partial(_inproj_kernel, head_blocks=head_blocks),
        grid=(n // tm, width // tn),
        in_specs=[
            pl.BlockSpec((tm, d), lambda i, j: (i, 0)),
            _resident((1, d)),
            pl.BlockSpec((d, tn), lambda i, j: (0, jnp.minimum(j, head_blocks - 1))),
            pl.BlockSpec((d, tn), lambda i, j: (0, jnp.where(j < head_blocks, tail_blocks - 1, j - head_blocks))),
            _resident(w_gate.shape),
        ],
        out_specs=[
            pl.BlockSpec((tm, tn), lambda i, j: (i, j)),
            pl.BlockSpec((tm, w_gate.shape[1]), lambda i, j: (i, 0)),
        ],
        out_shape=[
            jax.ShapeDtypeStruct((n, width), BF16),
            jax.ShapeDtypeStruct((n, w_gate.shape[1]), F32),
        ],
        scratch_shapes=[pltpu.VMEM((tm, d), BF16)],
        compiler_params=_params("parallel", "arbitrary"),
        name="inproj",
    )(x, gain.reshape(1, d), w_all, w_tail, w_gate)


def _mlstm_qkv_kernel(x_ref, halo_ref, cw_ref, cb_ref, wq_ref, wk_ref, wv_ref,
                      q_ref, k_ref, v_ref):
    ts = x_ref.shape[0]
    xb = x_ref[...]
    v_ref[...] = jnp.dot(xb, wv_ref[0], preferred_element_type=F32).astype(BF16)
    prev = jnp.where(pl.program_id(2) > 0, halo_ref[...].astype(F32), 0.0)
    full = jnp.concatenate([prev, xb.astype(F32)], axis=0)
    cw = cw_ref[...]
    sub = ts // QKV_SUBTILES
    for r0 in range(0, ts, sub):
        conv = cb_ref[...]
        for back in range(M_CONV):
            tap = M_CONV - 1 - back
            start = CONV_HALO + r0 - back
            conv = conv + cw[tap:tap + 1, :] * full[start:start + sub, :]
        xc = (conv * _sigmoid(conv)).astype(BF16)
        q_ref[r0:r0 + sub, :] = jnp.dot(xc, wq_ref[0], preferred_element_type=F32).astype(BF16)
        k_ref[r0:r0 + sub, :] = jnp.dot(xc, wk_ref[0], preferred_element_type=F32).astype(BF16)


def _mlstm_qkv(proj, conv_w, conv_b, wq, wk, wv, *, batch, seq):
    heads, hd, _ = wq.shape
    n = batch * seq
    ts = min(QKV_ROWS, seq)
    nt = seq // ts
    halo_per_tile = ts // CONV_HALO

    def x_map(h, b, t):
        return (b * nt + t, h)

    def halo_map(h, b, t):
        return (jnp.maximum((b * nt + t) * halo_per_tile - 1, 0), h)

    def w_map(h, b, t):
        return (h, 0, 0)

    out = jax.ShapeDtypeStruct((n, heads * hd), BF16)
    return pl.pallas_call(
        _mlstm_qkv_kernel,
        grid=(heads, batch, nt),
        in_specs=[
            pl.BlockSpec((ts, hd), x_map),
            pl.BlockSpec((CONV_HALO, hd), halo_map),
            pl.BlockSpec((M_CONV, hd), lambda h, b, t: (0, h)),
            pl.BlockSpec((1, hd), lambda h, b, t: (0, h)),
            pl.BlockSpec((1, hd, hd), w_map),
            pl.BlockSpec((1, hd, hd), w_map),
            pl.BlockSpec((1, hd, hd), w_map),
        ],
        out_specs=[pl.BlockSpec((ts, hd), x_map)] * 3,
        out_shape=[out, out, out],
        compiler_params=_params("parallel", "parallel", "parallel"),
        name="mlstm_qkv",
    )(proj, proj, conv_w, conv_b.reshape(1, -1), wq, wk, wv)


def _mlstm_scan_kernel(q_ref, k_ref, v_ref, g_ref, bias_ref, o_ref, gain_ref, y_ref,
                       c_scr, n_scr, m_scr):
    @pl.when(pl.program_id(2) == 0)
    def _():
        c_scr[...] = jnp.zeros_like(c_scr)
        n_scr[...] = jnp.zeros_like(n_scr)
        m_scr[...] = jnp.zeros_like(m_scr)

    group, hd, _ = c_scr.shape
    chunk = q_ref.shape[0]
    heads = range(group)
    cols = [slice(g * hd, (g + 1) * hd) for g in heads]
    q = [q_ref[:, cols[g]] for g in heads]
    k = [k_ref[:, cols[g]] for g in heads]
    v = [v_ref[:, cols[g]] for g in heads]
    c_prev = [c_scr[g] for g in heads]
    n_prev = [n_scr[g] for g in heads]
    m_prev = [m_scr[g] for g in heads]

    qk = [lax.dot_general(q[g], k[g], (((1,), (1,)), ((), ())), preferred_element_type=F32) for g in heads]
    qc = [jnp.dot(q[g], c_prev[g].astype(BF16), preferred_element_type=F32) for g in heads]

    t_idx = lax.broadcasted_iota(jnp.int32, (chunk, chunk), 0)
    s_idx = lax.broadcasted_iota(jnp.int32, (chunk, chunk), 1)
    causal = s_idx <= t_idx
    diag = s_idx == t_idx

    def to_col(row):
        return jnp.sum(jnp.where(diag, row, 0.0), axis=1, keepdims=True)

    def to_row(col):
        return jnp.sum(jnp.where(diag, col, 0.0), axis=0, keepdims=True)

    s, w_inter, m_t, w_s, decay, m_new = [], [], [], [], [], []
    for g in heads:
        gates = g_ref[0, g] + bias_ref[g]
        ig = gates[0:1, :]
        fpre = gates[1:2, :]
        lf = jnp.minimum(fpre, 0.0) - jnp.log1p(jnp.exp(-jnp.abs(fpre)))
        b_col = jnp.sum(jnp.where(causal, lf, 0.0), axis=1, keepdims=True)
        b_row = to_row(b_col)
        dlog = jnp.where(causal, b_col - b_row + ig, NEG_BIG)
        inter = b_col + m_prev[g]
        m_t.append(jnp.maximum(inter, jnp.max(dlog, axis=1, keepdims=True)))
        s.append(qk[g] * jnp.exp(dlog - m_t[g]))
        w_inter.append(jnp.exp(inter - m_t[g]))
        b_last = b_col[chunk - 1:chunk, :]
        tail = b_last - b_row + ig
        m_new.append(jnp.maximum(b_last + m_prev[g], jnp.max(tail, axis=1, keepdims=True)))
        decay.append(jnp.exp(b_last + m_prev[g] - m_new[g]))
        w_s.append(jnp.exp(to_col(tail) - m_new[g]))

    sv = [jnp.dot(s[g].astype(BF16), v[g], preferred_element_type=F32) for g in heads]
    kw = [k[g].astype(F32) * w_s[g] for g in heads]
    kv = [jnp.dot(kw[g].T.astype(BF16), v[g], preferred_element_type=F32) for g in heads]

    for g in heads:
        num = sv[g] + w_inter[g] * qc[g]
        qn = jnp.sum(q[g].astype(F32) * n_prev[g], axis=1, keepdims=True)
        den = jnp.sum(s[g], axis=1, keepdims=True) + w_inter[g] * qn
        h = num / jnp.maximum(jnp.abs(den), jnp.exp(-m_t[g]))
        c_scr[g] = decay[g] * c_prev[g] + kv[g]
        n_scr[g] = decay[g] * n_prev[g] + jnp.sum(kw[g], axis=0, keepdims=True)
        m_scr[g] = m_new[g]
        gate = _sigmoid(o_ref[:, cols[g]].astype(F32))
        y_ref[:, cols[g]] = (gate * (_rms(h) * gain_ref[:, cols[g]])).astype(BF16)


def _mlstm_scan(q, k, v, gates_t, b_if, proj, o_col_block, head_gain, *, batch, seq):
    n, width = q.shape
    heads = gates_t.shape[1]
    hd = width // heads
    group = MLSTM_HEAD_GROUP
    gw = group * hd
    assert heads % group == 0 and o_col_block % group == 0
    chunk = min(MLSTM_CHUNK, seq)
    nc = seq // chunk

    def qkv_map(b, h, c):
        return (b * nc + c, h)

    return pl.pallas_call(
        _mlstm_scan_kernel,
        grid=(batch, heads // group, nc),
        in_specs=[
            pl.BlockSpec((chunk, gw), qkv_map),
            pl.BlockSpec((chunk, gw), qkv_map),
            pl.BlockSpec((chunk, gw), qkv_map),
            pl.BlockSpec((1, group, 2, chunk), lambda b, h, c: (b, h, 0, c)),
            pl.BlockSpec((group, 2, 1), lambda b, h, c: (h, 0, 0)),
            pl.BlockSpec((chunk, gw), lambda b, h, c: (b * nc + c, o_col_block // group + h)),
            pl.BlockSpec((1, gw), lambda b, h, c: (0, h)),
        ],
        out_specs=pl.BlockSpec((chunk, gw), qkv_map),
        out_shape=jax.ShapeDtypeStruct((n, width), BF16),
        scratch_shapes=[pltpu.VMEM((group, hd, hd), F32), pltpu.VMEM((group, 1, hd), F32),
                        pltpu.VMEM((group, 1, 1), F32)],
        compiler_params=_params("parallel", "parallel", "arbitrary"),
        name="mlstm_scan",
    )(q, k, v, gates_t, b_if.reshape(2, heads, 1).transpose(1, 0, 2), proj, head_gain.reshape(1, -1))


def _diff_attn_kernel(far_ref, q_ref, k_ref, v_ref, band_ref, lam_ref, gain_ref, y_ref,
                      vt_scr, acc_scr, st_scr, *, out_scale, lam_init):
    group, n_tiles, rows, tile = vt_scr.shape
    width = q_ref.shape[1] // group
    head0 = pl.program_id(1) * group
    qi = pl.program_id(2)
    half = width // 2

    @pl.when(qi == 0)
    def _():
        ones = jnp.ones((rows - width, tile), BF16)
        for g in range(group):
            for j in range(n_tiles):
                v_tile = v_ref[j * tile:(j + 1) * tile, g * width:(g + 1) * width]
                vt_scr[g, j] = jnp.concatenate([v_tile.astype(F32).T.astype(BF16), ones], axis=0)

    q_all = q_ref[...]
    lane = lax.broadcasted_iota(jnp.int32, (tile, width), 1)
    qs = []
    for g in range(group):
        q = q_all[:, g * width:(g + 1) * width].astype(F32) * (half ** -0.5 * LOG2_E)
        stacked = jnp.concatenate([jnp.where(lane < half, q, 0.0), jnp.where(lane >= half, q, 0.0)], axis=0)
        qs.append(stacked.astype(BF16))

    acc_scr[...] = jnp.zeros_like(acc_scr)
    lam_v = lam_ref[...]
    lam = (jnp.exp(jnp.sum(lam_v[0:1, :] * lam_v[1:2, :], axis=1, keepdims=True))
           - jnp.exp(jnp.sum(lam_v[2:3, :] * lam_v[3:4, :], axis=1, keepdims=True)) + lam_init)

    def scores(kj, g):
        start = pl.multiple_of(kj * tile, tile)
        return lax.dot_general(k_ref[pl.ds(start, tile), g * width:(g + 1) * width], qs[g],
                               (((1,), (1,)), ((), ())), preferred_element_type=F32)

    def block(tiles, maxes):
        units = [(kj, band_idx, g) for kj, band_idx in tiles for g in range(group)]
        all_scores, parked = [], 0
        for kj, band_idx, g in units:
            if band_idx is None:
                all_scores.append(scores(kj, g))
            else:
                st_scr[parked] = scores(kj, g)
                all_scores.append(parked)
                parked += 1
        maxes = list(maxes)
        for st, (kj, band_idx, g) in zip(all_scores, units):
            m_prev = maxes[g]
            if band_idx is None:
                shift = far_bias[g]
            else:
                shift = 0.0
                band = band_ref[g, band_idx]
                st = st_scr[st] + jnp.concatenate([band, band], axis=1)
            vt = vt_scr[g, kj]
            m_cols = []
            for c0 in range(0, 2 * tile, ATTN_PV_COLS):
                p_cols, a_cols = [], []
                for c in range(c0, c0 + ATTN_PV_COLS, ATTN_STRIP):
                    cols = slice(c, c + ATTN_STRIP)
                    s_c = st[:, cols]
                    m_c = jnp.maximum(m_prev[:, cols], jnp.max(s_c, axis=0, keepdims=True) + shift)
                    a_cols.append(jnp.exp2(m_prev[:, cols] - m_c))
                    p_cols.append(jnp.exp2(s_c - (m_c - shift)).astype(BF16))
                    m_cols.append(m_c)
                pv = jnp.dot(vt, jnp.concatenate(p_cols, axis=1), preferred_element_type=F32)
                wide = slice(c0, c0 + ATTN_PV_COLS)
                acc_scr[g, :, wide] = jnp.concatenate(a_cols, axis=1) * acc_scr[g, :, wide] + pv
            maxes[g] = jnp.concatenate(m_cols, axis=1)
        return tuple(maxes)

    def far(first, count):
        return tuple((first + i, None) for i in range(count))

    far_bias = tuple(far_ref[head0 + g] * LOG2_E for g in range(group))
    n_far = jnp.maximum(qi - 1, 0)
    n_quads = n_far // 4
    rest = n_far - 4 * n_quads
    maxes = tuple(jnp.full((1, 2 * tile), NEG_BIG, F32) for _ in range(group))
    maxes = lax.fori_loop(0, n_quads, lambda i, c: block(far(4 * i, 4), c), maxes)
    maxes = lax.cond(rest >= 2, lambda: block(far(4 * n_quads, 2), maxes), lambda: maxes)
    near = ((qi - 1, 0), (qi, 1))

    def last_tiles():
        return lax.cond(rest % 2 == 1, lambda: block(far(n_far - 1, 1) + near, maxes), lambda: block(near, maxes))

    lax.cond(qi > 0, last_tiles, lambda: block(((qi, 1),), maxes))

    for g in range(group):
        o = acc_scr[g, :width, :] / acc_scr[g, width:width + 1, :]
        out = (o[:, :tile] - lam * o[:, tile:]).T
        gain = gain_ref[:, g * width:(g + 1) * width]
        y_ref[:, g * width:(g + 1) * width] = (_rms(out) * gain * out_scale).astype(BF16)


def _t5_bucket(n):
    max_exact = REL_BUCKETS // 2
    nf = jnp.maximum(n, 1).astype(F32)
    large = max_exact + (jnp.log(nf / max_exact) / math.log(REL_MAX_DIST / max_exact)
                         * (REL_BUCKETS - max_exact)).astype(jnp.int32)
    large = jnp.minimum(large, REL_BUCKETS - 1)
    return jnp.where(n < max_exact, n, large)


def _diff_attn(proj, q_col, k_col, v_col, rel_bias, lam_vecs, head_gain, *, batch, seq):
    n = proj.shape[0]
    heads = rel_bias.shape[1]
    hw = head_gain.shape[0] // heads
    tile = min(ATTN_TILE, seq)
    nq = seq // tile
    assert tile >= REL_MAX_DIST, "distances beyond one tile must all fall in the last bucket"
    group = ATTN_HEAD_GROUP
    gw = group * hw
    assert heads % group == 0 and q_col % group == 0 and k_col % group == 0 and v_col % group == 0

    bucket = _t5_bucket(jnp.arange(2 * tile, dtype=jnp.int32))
    by_dist = jnp.zeros((heads, 2 * tile), F32)
    for b in range(REL_BUCKETS):
        by_dist = jnp.where(bucket[None] == b, rel_bias[b].astype(F32)[:, None], by_dist)
    masked = jnp.full((heads, tile - 1), NEG_BIG, F32)
    line = jnp.concatenate([masked, by_dist * LOG2_E, masked[:, :2]], axis=1)
    length = line.shape[1] - 1
    skew = jnp.tile(line, (1, tile))[:, :tile * length].reshape(heads, tile, length)
    band = jnp.stack([skew[:, :, 2 * tile - 1:3 * tile - 1], skew[:, :, tile - 1:2 * tile - 1]], axis=1)
    far = rel_bias[REL_BUCKETS - 1].astype(F32)

    lam_init = 0.8 - 0.6 * math.exp(-0.3 * LAYER)
    return pl.pallas_call(
        functools.partial(_diff_attn_kernel, out_scale=1.0 - lam_init, lam_init=lam_init),
        grid=(batch, heads // group, nq),
        in_specs=[
            pl.BlockSpec(memory_space=pltpu.SMEM),
            pl.BlockSpec((tile, gw), lambda b, h, i: (b * nq + i, q_col // group + h)),
            pl.BlockSpec((seq, gw), lambda b, h, i: (b, k_col // group + h)),
            pl.BlockSpec((seq, gw), lambda b, h, i: (b, v_col // group + h)),
            pl.BlockSpec((group, 2, tile, tile), lambda b, h, i: (h, 0, 0, 0)),
            _resident(lam_vecs.shape),
            pl.BlockSpec((1, gw), lambda b, h, i: (0, h)),
        ],
        out_specs=pl.BlockSpec((tile, gw), lambda b, h, i: (b * nq + i, h)),
        out_shape=jax.ShapeDtypeStruct((n, heads * hw), BF16),
        scratch_shapes=[
            pltpu.VMEM((group, nq, hw + ATTN_ONES_ROWS, tile), BF16),
            pltpu.VMEM((group, hw + ATTN_ONES_ROWS, 2 * tile), F32),
            pltpu.VMEM((2 * group, tile, 2 * tile), F32),
        ],
        compiler_params=_params("parallel", "parallel", "arbitrary"),
        name="diff_attn",
    )(far, proj, proj, proj, band, lam_vecs, head_gain.reshape(1, -1))


def _mem_kv_kernel(mem_ref, g_ref, w_ref, k_ref, v_ref):
    width = k_ref.shape[-1]
    hb = (_rms(mem_ref[0]) * g_ref[...]).astype(BF16)
    k_ref[0] = jnp.dot(hb, w_ref[:, :width], preferred_element_type=F32).astype(BF16)
    v_ref[0] = jnp.dot(hb, w_ref[:, width:], preferred_element_type=F32).astype(BF16)


def _mem_kv(mem, gain, w_kv):
    batch, mlen, d = mem.shape
    width = w_kv.shape[1] // 2
    out = jax.ShapeDtypeStruct((batch, mlen, width), BF16)
    return pl.pallas_call(
        _mem_kv_kernel,
        grid=(batch,),
        in_specs=[
            pl.BlockSpec((1, mlen, d), lambda b: (b, 0, 0)),
            _resident((1, d)),
            _resident(w_kv.shape),
        ],
        out_specs=[pl.BlockSpec((1, mlen, width), lambda b: (b, 0, 0))] * 2,
        out_shape=[out, out],
        compiler_params=_params("parallel"),
        name="mem_kv",
    )(mem, gain.reshape(1, d), w_kv)


def _merge_kernel(x_ref, qx_ref, g0_ref, g1_ref, g2_ref, ym_ref, ya_ref, kx_ref, vx_ref,
                  bg_ref, wm_ref, wa_ref, wx_ref, wo_ref, o_ref):
    d = x_ref.shape[1]
    xw = qx_ref.shape[1]
    xhd = xw // X_HEADS
    heads = []
    for h in range(X_HEADS):
        cols = slice(h * xhd, (h + 1) * xhd)
        s = lax.dot_general(qx_ref[:, cols], kx_ref[0, :, cols], (((1,), (1,)), ((), ())),
                            preferred_element_type=F32) * (xhd ** -0.5)
        p = jnp.exp(s - jnp.max(s, axis=1, keepdims=True))
        pv = jnp.dot(p.astype(BF16), vx_ref[0, :, cols], preferred_element_type=F32)
        heads.append((pv / jnp.sum(p, axis=1, keepdims=True)).astype(BF16))
    yx = jnp.concatenate(heads, axis=1)

    def gate(g_ref, k):
        return _sigmoid(g_ref[...].astype(F32) + bg_ref[:, k * d:(k + 1) * d])

    merged = (gate(g0_ref, 0) * jnp.dot(ym_ref[...], wm_ref[...], preferred_element_type=F32)
              + gate(g1_ref, 1) * jnp.dot(ya_ref[...], wa_ref[...], preferred_element_type=F32)
              + gate(g2_ref, 2) * jnp.dot(yx, wx_ref[...], preferred_element_type=F32))
    o_ref[...] = x_ref[...] + jnp.dot(merged.astype(BF16), wo_ref[...], preferred_element_type=F32)


def _merge(x, proj, qx_col, g_col, y_m, y_a, kx, vx, b_gate, w_m, w_a, w_x, w_o, *, batch, seq):
    n, d = x.shape
    tm = min(MERGE_ROWS, seq)
    nt = seq // tm

    def row(b, t):
        return b * nt + t

    def proj_spec(col):
        return pl.BlockSpec((tm, d), lambda b, t: (row(b, t), col))

    def mem_spec(a):
        return pl.BlockSpec((1,) + a.shape[1:], lambda b, t: (b, 0, 0))

    return pl.pallas_call(
        _merge_kernel,
        grid=(batch, nt),
        in_specs=[
            pl.BlockSpec((tm, d), lambda b, t: (row(b, t), 0)),
            proj_spec(qx_col), proj_spec(g_col), proj_spec(g_col + 1), proj_spec(g_col + 2),
            pl.BlockSpec((tm, y_m.shape[1]), lambda b, t: (row(b, t), 0)),
            pl.BlockSpec((tm, y_a.shape[1]), lambda b, t: (row(b, t), 0)),
            mem_spec(kx), mem_spec(vx),
            _resident((1, b_gate.shape[0])),
            _resident(w_m.shape), _resident(w_a.shape), _resident(w_x.shape), _resident(w_o.shape),
        ],
        out_specs=pl.BlockSpec((tm, d), lambda b, t: (row(b, t), 0)),
        out_shape=jax.ShapeDtypeStruct((n, d), F32),
        compiler_params=_params("parallel", "parallel"),
        name="merge",
    )(x, proj, proj, proj, proj, y_m, y_a, kx, vx, b_gate.reshape(1, -1), w_m, w_a, w_x, w_o)


def kernel(x, mem, positions, rel_bias, ffn1_norm, ffn1_w_gu, ffn1_w_down, mix_norm, mem_norm, w_in, b_if, b_gate, m_conv_w, m_conv_b, m_wq, m_wk, m_wv, m_head_norm, a_lam_q1, a_lam_k1, a_lam_q2, a_lam_k2, a_head_norm, w_mem_kv, w_branch_m, w_branch_a, w_branch_x, w_out, ffn2_norm, ffn2_w_gu, ffn2_w_down, final_norm):
    del positions
    batch, seq, d = x.shape
    n = batch * seq
    l = LAYER
    m_inner = m_conv_w.shape[-1]
    a_width = a_head_norm.shape[-1]
    x_width = w_mem_kv.shape[-1] // 2
    bf = lambda a: a.astype(BF16)

    gate_lo = 2 * m_inner
    gate_hi = gate_lo + 2 * M_HEADS
    w_all = bf(w_in[l])
    w_tail = w_all[:, gate_hi:]
    w_gate = jnp.pad(w_all[:, gate_lo:gate_hi], ((0, 0), (0, LANES - 2 * M_HEADS)))

    x0 = x.reshape(n, d)
    x1 = _ffn(x0, ffn1_norm[l], bf(ffn1_w_gu[l]), bf(ffn1_w_down[l]), final_norm, final_norm=False)

    proj, gates = _inproj(x1, mix_norm[l], w_all, gate_lo, w_tail, w_gate)

    m_hd = m_inner // M_HEADS
    q_m, k_m, v_m = _mlstm_qkv(proj, m_conv_w[l], m_conv_b[l], bf(m_wq[l]), bf(m_wk[l] * m_hd ** -0.5),
                               bf(m_wv[l]), batch=batch, seq=seq)
    gates_t = gates[:, :2 * M_HEADS].reshape(batch, seq, 2, M_HEADS).transpose(0, 3, 2, 1)
    y_m = _mlstm_scan(q_m, k_m, v_m, gates_t, b_if[l], proj, m_inner // m_hd, m_head_norm[l],
                      batch=batch, seq=seq)

    a_hw = a_width // A_HEADS
    qa_col = 2 * m_inner // a_hw
    lam_vecs = jnp.stack([a_lam_q1[l], a_lam_k1[l], a_lam_q2[l], a_lam_k2[l]])
    y_a = _diff_attn(proj, qa_col, qa_col + A_HEADS, qa_col + 2 * A_HEADS, rel_bias, lam_vecs,
                     a_head_norm[l], batch=batch, seq=seq)

    kx, vx = _mem_kv(mem, mem_norm[l], bf(w_mem_kv[l]))
    qx_col = (2 * m_inner + 3 * a_width) // d
    x2 = _merge(x1, proj, qx_col, qx_col + x_width // d, y_m, y_a, kx, vx, b_gate[l],
                bf(w_branch_m[l]), bf(w_branch_a[l]), bf(w_branch_x[l]), bf(w_out[l]),
                batch=batch, seq=seq)

    out = _ffn(x2, ffn2_norm[l], bf(ffn2_w_gu[l]), bf(ffn2_w_down[l]), final_norm, final_norm=True)
    return out.reshape(batch, seq, d)
```

```python
import functools
import math

import jax
import jax.numpy as jnp
from jax import lax
from jax.experimental import pallas as pl
from jax.experimental.pallas import tpu as pltpu

EPS = 1e-6
BF16 = jnp.bfloat16
F32 = jnp.float32
NEG_BIG = -1e30

M_HEADS = 4
M_CONV = 4
A_HEADS = 8
X_HEADS = 4
N_BRANCH = 3
REL_BUCKETS = 32
REL_MAX_DIST = 128
LAYER = 0

VMEM_LIMIT_BYTES = 56 * 1024 * 1024

FFN_ROWS = 1024
FFN_SUBTILES = 4
INPROJ_ROWS = 2048
INPROJ_COLS = 1024
QKV_ROWS = 4096
QKV_SUBTILES = 16
MLSTM_CHUNK = 256
MLSTM_HEAD_GROUP = 2
ATTN_TILE = 256
ATTN_HEAD_GROUP = 4
LANES = 128
ATTN_STRIP = LANES
ATTN_PV_COLS = 256
ATTN_ONES_ROWS = 16
LOG2_E = math.log2(math.e)
MERGE_ROWS = 512
CONV_HALO = 16


def _params(*sem):
    return pltpu.CompilerParams(dimension_semantics=sem, vmem_limit_bytes=VMEM_LIMIT_BYTES)


def _resident(shape):
    zeros = (0,) * len(shape)
    return pl.BlockSpec(shape, lambda *_: zeros, pipeline_mode=pl.Buffered(1))


def _rms(x):
    return x * lax.rsqrt(jnp.mean(x * x, axis=-1, keepdims=True) + EPS)


def _sigmoid(x):
    return 1.0 / (1.0 + jnp.exp(-x))


def _ffn_kernel(x_ref, g_ref, wgu_ref, wd_ref, fg_ref, o_ref, *, d_ff, final_norm):
    sub = x_ref.shape[0] // FFN_SUBTILES
    for r0 in range(0, x_ref.shape[0], sub):
        x = x_ref[r0:r0 + sub, :]
        hb = (_rms(x) * g_ref[...]).astype(BF16)
        gate = jnp.dot(hb, wgu_ref[:, :d_ff], preferred_element_type=F32)
        up = jnp.dot(hb, wgu_ref[:, d_ff:], preferred_element_type=F32)
        act = (gate * _sigmoid(gate) * up).astype(BF16)
        y = x + 0.5 * jnp.dot(act, wd_ref[...], preferred_element_type=F32)
        if final_norm:
            y = _rms(y) * fg_ref[...]
        o_ref[r0:r0 + sub, :] = y


def _ffn(x, gain, w_gu, w_down, final_gain, *, final_norm):
    n, d = x.shape
    d_ff = w_down.shape[0]
    tm = min(FFN_ROWS, n)
    return pl.pallas_call(
        functools.partial(_ffn_kernel, d_ff=d_ff, final_norm=final_norm),
        grid=(n // tm,),
        in_specs=[
            pl.BlockSpec((tm, d), lambda i: (i, 0)),
            _resident((1, d)),
            _resident((d, 2 * d_ff)),
            _resident((d_ff, d)),
            _resident((1, d)),
        ],
        out_specs=pl.BlockSpec((tm, d), lambda i: (i, 0)),
        out_shape=jax.ShapeDtypeStruct((n, d), F32),
        compiler_params=_params("parallel"),
        name="ffn_final" if final_norm else "ffn",
    )(x, gain.reshape(1, d), w_gu, w_down, final_gain.reshape(1, d))


def _inproj_kernel(x_ref, g_ref, w_head_ref, w_tail_ref, wgate_ref, p_ref, gates_ref, h_scr, *, head_blocks):
    j = pl.program_id(1)

    @pl.when(j == 0)
    def _():
        hb = (_rms(x_ref[...]) * g_ref[...]).astype(BF16)
        h_scr[...] = hb
        gates_ref[...] = jnp.dot(hb, wgate_ref[...], preferred_element_type=F32)

    @pl.when(j < head_blocks)
    def _():
        p_ref[...] = jnp.dot(h_scr[...], w_head_ref[...], preferred_element_type=F32).astype(BF16)

    @pl.when(j >= head_blocks)
    def _():
        p_ref[...] = jnp.dot(h_scr[...], w_tail_ref[...], preferred_element_type=F32).astype(BF16)


def _inproj(x, gain, w_all, head_width, w_tail, w_gate):
    n, d = x.shape
    tm = min(INPROJ_ROWS, n)
    tn = INPROJ_COLS
    head_blocks = head_width // tn
    tail_blocks = w_tail.shape[1] // tn
    width = head_width + w_tail.shape[1]
    return pl.pallas_call(
        functools.partial(_inproj_kernel, head_blocks=head_blocks),
        grid=(n // tm, width // tn),
        in_specs=[
            pl.BlockSpec((tm, d), lambda i, j: (i, 0)),
            _resident((1, d)),
            pl.BlockSpec((d, tn), lambda i, j: (0, jnp.minimum(j, head_blocks - 1))),
            pl.BlockSpec((d, tn), lambda i, j: (0, jnp.where(j < head_blocks, tail_blocks - 1, j - head_blocks))),
            _resident(w_gate.shape),
        ],
        out_specs=[
            pl.BlockSpec((tm, tn), lambda i, j: (i, j)),
            pl.BlockSpec((tm, w_gate.shape[1]), lambda i, j: (i, 0)),
        ],
        out_shape=[
            jax.ShapeDtypeStruct((n, width), BF16),
            jax.ShapeDtypeStruct((n, w_gate.shape[1]), F32),
        ],
        scratch_shapes=[pltpu.VMEM((tm, d), BF16)],
        compiler_params=_params("parallel", "arbitrary"),
        name="inproj",
    )(x, gain.reshape(1, d), w_all, w_tail, w_gate)


def _mlstm_qkv_kernel(x_ref, halo_ref, cw_ref, cb_ref, wq_ref, wk_ref, wv_ref,
                      q_ref, k_ref, v_ref):
    ts = x_ref.shape[0]
    xb = x_ref[...]
    v_ref[...] = jnp.dot(xb, wv_ref[0], preferred_element_type=F32).astype(BF16)
    prev = jnp.where(pl.program_id(2) > 0, halo_ref[...].astype(F32), 0.0)
    full = jnp.concatenate([prev, xb.astype(F32)], axis=0)
    cw = cw_ref[...]
    sub = ts // QKV_SUBTILES
    for r0 in range(0, ts, sub):
        conv = cb_ref[...]
        for back in range(M_CONV):
            tap = M_CONV - 1 - back
            start = CONV_HALO + r0 - back
            conv = conv + cw[tap:tap + 1, :] * full[start:start + sub, :]
        xc = (conv * _sigmoid(conv)).astype(BF16)
        q_ref[r0:r0 + sub, :] = jnp.dot(xc, wq_ref[0], preferred_element_type=F32).astype(BF16)
        k_ref[r0:r0 + sub, :] = jnp.dot(xc, wk_ref[0], preferred_element_type=F32).astype(BF16)


def _mlstm_qkv(proj, conv_w, conv_b, wq, wk, wv, *, batch, seq):
    heads, hd, _ = wq.shape
    n = batch * seq
    ts = min(QKV_ROWS, seq)
    nt = seq // ts
    halo_per_tile = ts // CONV_HALO

    def x_map(h, b, t):
        return (b * nt + t, h)

    def halo_map(h, b, t):
        return (jnp.maximum((b * nt + t) * halo_per_tile - 1, 0), h)

    def w_map(h, b, t):
        return (h, 0, 0)

    out = jax.ShapeDtypeStruct((n, heads * hd), BF16)
    return pl.pallas_call(
        _mlstm_qkv_kernel,
        grid=(heads, batch, nt),
        in_specs=[
            pl.BlockSpec((ts, hd), x_map),
            pl.BlockSpec((CONV_HALO, hd), halo_map),
            pl.BlockSpec((M_CONV, hd), lambda h, b, t: (0, h)),
            pl.BlockSpec((1, hd), lambda h, b, t: (0, h)),
            pl.BlockSpec((1, hd, hd), w_map),
            pl.BlockSpec((1, hd, hd), w_map),
            pl.BlockSpec((1, hd, hd), w_map),
        ],
        out_specs=[pl.BlockSpec((ts, hd), x_map)] * 3,
        out_shape=[out, out, out],
        compiler_params=_params("parallel", "parallel", "parallel"),
        name="mlstm_qkv",
    )(proj, proj, conv_w, conv_b.reshape(1, -1), wq, wk, wv)


def _mlstm_scan_kernel(q_ref, k_ref, v_ref, g_ref, bias_ref, o_ref, gain_ref, y_ref,
                       c_scr, n_scr, m_scr):
    @pl.when(pl.program_id(2) == 0)
    def _():
        c_scr[...] = jnp.zeros_like(c_scr)
        n_scr[...] = jnp.zeros_like(n_scr)
        m_scr[...] = jnp.zeros_like(m_scr)

    group, hd, _ = c_scr.shape
    chunk = q_ref.shape[0]
    heads = range(group)
    cols = [slice(g * hd, (g + 1) * hd) for g in heads]
    q = [q_ref[:, cols[g]] for g in heads]
    k = [k_ref[:, cols[g]] for g in heads]
    v = [v_ref[:, cols[g]] for g in heads]
    c_prev = [c_scr[g] for g in heads]
    n_prev = [n_scr[g] for g in heads]
    m_prev = [m_scr[g] for g in heads]

    qk = [lax.dot_general(q[g], k[g], (((1,), (1,)), ((), ())), preferred_element_type=F32) for g in heads]
    qc = [jnp.dot(q[g], c_prev[g].astype(BF16), preferred_element_type=F32) for g in heads]

    t_idx = lax.broadcasted_iota(jnp.int32, (chunk, chunk), 0)
    s_idx = lax.broadcasted_iota(jnp.int32, (chunk, chunk), 1)
    causal = s_idx <= t_idx
    diag = s_idx == t_idx

    def to_col(row):
        return jnp.sum(jnp.where(diag, row, 0.0), axis=1, keepdims=True)

    def to_row(col):
        return jnp.sum(jnp.where(diag, col, 0.0), axis=0, keepdims=True)

    s, w_inter, m_t, w_s, decay, m_new = [], [], [], [], [], []
    for g in heads:
        gates = g_ref[0, g] + bias_ref[g]
        ig = gates[0:1, :]
        fpre = gates[1:2, :]
        lf = jnp.minimum(fpre, 0.0) - jnp.log1p(jnp.exp(-jnp.abs(fpre)))
        b_col = jnp.sum(jnp.where(causal, lf, 0.0), axis=1, keepdims=True)
        b_row = to_row(b_col)
        dlog = jnp.where(causal, b_col - b_row + ig, NEG_BIG)
        inter = b_col + m_prev[g]
        m_t.append(jnp.maximum(inter, jnp.max(dlog, axis=1, keepdims=True)))
        s.append(qk[g] * jnp.exp(dlog - m_t[g]))
        w_inter.append(jnp.exp(inter - m_t[g]))
        b_last = b_col[chunk - 1:chunk, :]
        tail = b_last - b_row + ig
        m_new.append(jnp.maximum(b_last + m_prev[g], jnp.max(tail, axis=1, keepdims=True)))
        decay.append(jnp.exp(b_last + m_prev[g] - m_new[g]))
        w_s.append(jnp.exp(to_col(tail) - m_new[g]))

    sv = [jnp.dot(s[g].astype(BF16), v[g], preferred_element_type=F32) for g in heads]
    kw = [k[g].astype(F32) * w_s[g] for g in heads]
    kv = [jnp.dot(kw[g].T.astype(BF16), v[g], preferred_element_type=F32) for g in heads]

    for g in heads:
        num = sv[g] + w_inter[g] * qc[g]
        qn = jnp.sum(q[g].astype(F32) * n_prev[g], axis=1, keepdims=True)
        den = jnp.sum(s[g], axis=1, keepdims=True) + w_inter[g] * qn
        h = num / jnp.maximum(jnp.abs(den), jnp.exp(-m_t[g]))
        c_scr[g] = decay[g] * c_prev[g] + kv[g]
        n_scr[g] = decay[g] * n_prev[g] + jnp.sum(kw[g], axis=0, keepdims=True)
        m_scr[g] = m_new[g]
        gate = _sigmoid(o_ref[:, cols[g]].astype(F32))
        y_ref[:, cols[g]] = (gate * (_rms(h) * gain_ref[:, cols[g]])).astype(BF16)


def _mlstm_scan(q, k, v, gates_t, b_if, proj, o_col_block, head_gain, *, batch, seq):
    n, width = q.shape
    heads = gates_t.shape[1]
    hd = width // heads
    group = MLSTM_HEAD_GROUP
    gw = group * hd
    assert heads % group == 0 and o_col_block % group == 0
    chunk = min(MLSTM_CHUNK, seq)
    nc = seq // chunk

    def qkv_map(b, h, c):
        return (b * nc + c, h)

    return pl.pallas_call(
        _mlstm_scan_kernel,
        grid=(batch, heads // group, nc),
        in_specs=[
            pl.BlockSpec((chunk, gw), qkv_map),
            pl.BlockSpec((chunk, gw), qkv_map),
            pl.BlockSpec((chunk, gw), qkv_map),
            pl.BlockSpec((1, group, 2, chunk), lambda b, h, c: (b, h, 0, c)),
            pl.BlockSpec((group, 2, 1), lambda b, h, c: (h, 0, 0)),
            pl.BlockSpec((chunk, gw), lambda b, h, c: (b * nc + c, o_col_block // group + h)),
            pl.BlockSpec((1, gw), lambda b, h, c: (0, h)),
        ],
        out_specs=pl.BlockSpec((chunk, gw), qkv_map),
        out_shape=jax.ShapeDtypeStruct((n, width), BF16),
        scratch_shapes=[pltpu.VMEM((group, hd, hd), F32), pltpu.VMEM((group, 1, hd), F32),
                        pltpu.VMEM((group, 1, 1), F32)],
        compiler_params=_params("parallel", "parallel", "arbitrary"),
        name="mlstm_scan",
    )(q, k, v, gates_t, b_if.reshape(2, heads, 1).transpose(1, 0, 2), proj, head_gain.reshape(1, -1))


def _diff_attn_kernel(far_ref, q_ref, k_ref, v_ref, band_ref, lam_ref, gain_ref, y_ref,
                      vt_scr, acc_scr, st_scr, *, out_scale, lam_init):
    group, n_tiles, rows, tile = vt_scr.shape
    width = q_ref.shape[1] // group
    head0 = pl.program_id(1) * group
    qi = pl.program_id(2)
    half = width // 2

    @pl.when(qi == 0)
    def _():
        ones = jnp.ones((rows - width, tile), BF16)
        for g in range(group):
            for j in range(n_tiles):
                v_tile = v_ref[j * tile:(j + 1) * tile, g * width:(g + 1) * width]
                vt_scr[g, j] = jnp.concatenate([v_tile.astype(F32).T.astype(BF16), ones], axis=0)

    q_all = q_ref[...]
    lane = lax.broadcasted_iota(jnp.int32, (tile, width), 1)
    qs = []
    for g in range(group):
        q = q_all[:, g * width:(g + 1) * width].astype(F32) * (half ** -0.5 * LOG2_E)
        stacked = jnp.concatenate([jnp.where(lane < half, q, 0.0), jnp.where(lane >= half, q, 0.0)], axis=0)
        qs.append(stacked.astype(BF16))

    acc_scr[...] = jnp.zeros_like(acc_scr)
    lam_v = lam_ref[...]
    lam = (jnp.exp(jnp.sum(lam_v[0:1, :] * lam_v[1:2, :], axis=1, keepdims=True))
           - jnp.exp(jnp.sum(lam_v[2:3, :] * lam_v[3:4, :], axis=1, keepdims=True)) + lam_init)

    def scores(kj, g):
        start = pl.multiple_of(kj * tile, tile)
        return lax.dot_general(k_ref[pl.ds(start, tile), g * width:(g + 1) * width], qs[g],
                               (((1,), (1,)), ((), ())), preferred_element_type=F32)

    def block(tiles, maxes):
        units = [(kj, band_idx, g) for kj, band_idx in tiles for g in range(group)]
        all_scores, parked = [], 0
        for kj, band_idx, g in units:
            if band_idx is None:
                all_scores.append(scores(kj, g))
            else:
                st_scr[parked] = scores(kj, g)
                all_scores.append(parked)
                parked += 1
        maxes = list(maxes)
        for st, (kj, band_idx, g) in zip(all_scores, units):
            m_prev = maxes[g]
            if band_idx is None:
                shift = far_bias[g]
            else:
                shift = 0.0
                band = band_ref[g, band_idx]
                st = st_scr[st] + jnp.concatenate([band, band], axis=1)
            vt = vt_scr[g, kj]
            m_cols = []
            for c0 in range(0, 2 * tile, ATTN_PV_COLS):
                p_cols, a_cols = [], []
                for c in range(c0, c0 + ATTN_PV_COLS, ATTN_STRIP):
                    cols = slice(c, c + ATTN_STRIP)
                    s_c = st[:, cols]
                    m_c = jnp.maximum(m_prev[:, cols], jnp.max(s_c, axis=0, keepdims=True) + shift)
                    a_cols.append(jnp.exp2(m_prev[:, cols] - m_c))
                    p_cols.append(jnp.exp2(s_c - (m_c - shift)).astype(BF16))
                    m_cols.append(m_c)
                pv = jnp.dot(vt, jnp.concatenate(p_cols, axis=1), preferred_element_type=F32)
                wide = slice(c0, c0 + ATTN_PV_COLS)
                acc_scr[g, :, wide] = jnp.concatenate(a_cols, axis=1) * acc_scr[g, :, wide] + pv
            maxes[g] = jnp.concatenate(m_cols, axis=1)
        return tuple(maxes)

    def far(first, count):
        return tuple((first + i, None) for i in range(count))

    far_bias = tuple(far_ref[head0 + g] * LOG2_E for g in range(group))
    n_far = jnp.maximum(qi - 1, 0)
    n_octs = n_far // 8
    n_quads = n_far // 4
    rest = n_far - 4 * n_quads
    maxes = tuple(jnp.full((1, 2 * tile), NEG_BIG, F32) for _ in range(group))
    maxes = lax.fori_loop(0, n_octs, lambda i, c: block(far(8 * i, 8), c), maxes)
    maxes = lax.cond(n_quads > 2 * n_octs, lambda: block(far(8 * n_octs, 4), maxes), lambda: maxes)
    maxes = lax.cond(rest >= 2, lambda: block(far(4 * n_quads, 2), maxes), lambda: maxes)
    near = ((qi - 1, 0), (qi, 1))

    def last_tiles():
        return lax.cond(rest % 2 == 1, lambda: block(far(n_far - 1, 1) + near, maxes), lambda: block(near, maxes))

    lax.cond(qi > 0, last_tiles, lambda: block(((qi, 1),), maxes))

    for g in range(group):
        o = acc_scr[g, :width, :] / acc_scr[g, width:width + 1, :]
        out = (o[:, :tile] - lam * o[:, tile:]).T
        gain = gain_ref[:, g * width:(g + 1) * width]
        y_ref[:, g * width:(g + 1) * width] = (_rms(out) * gain * out_scale).astype(BF16)


def _t5_bucket(n):
    max_exact = REL_BUCKETS // 2
    nf = jnp.maximum(n, 1).astype(F32)
    large = max_exact + (jnp.log(nf / max_exact) / math.log(REL_MAX_DIST / max_exact)
                         * (REL_BUCKETS - max_exact)).astype(jnp.int32)
    large = jnp.minimum(large, REL_BUCKETS - 1)
    return jnp.where(n < max_exact, n, large)


def _diff_attn(proj, q_col, k_col, v_col, rel_bias, lam_vecs, head_gain, *, batch, seq):
    n = proj.shape[0]
    heads = rel_bias.shape[1]
    hw = head_gain.shape[0] // heads
    tile = min(ATTN_TILE, seq)
    nq = seq // tile
    assert tile >= REL_MAX_DIST, "distances beyond one tile must all fall in the last bucket"
    group = ATTN_HEAD_GROUP
    gw = group * hw
    assert heads % group == 0 and q_col % group == 0 and k_col % group == 0 and v_col % group == 0

    bucket = _t5_bucket(jnp.arange(2 * tile, dtype=jnp.int32))
    by_dist = jnp.zeros((heads, 2 * tile), F32)
    for b in range(REL_BUCKETS):
        by_dist = jnp.where(bucket[None] == b, rel_bias[b].astype(F32)[:, None], by_dist)
    masked = jnp.full((heads, tile - 1), NEG_BIG, F32)
    line = jnp.concatenate([masked, by_dist * LOG2_E, masked[:, :2]], axis=1)
    length = line.shape[1] - 1
    skew = jnp.tile(line, (1, tile))[:, :tile * length].reshape(heads, tile, length)
    band = jnp.stack([skew[:, :, 2 * tile - 1:3 * tile - 1], skew[:, :, tile - 1:2 * tile - 1]], axis=1)
    far = rel_bias[REL_BUCKETS - 1].astype(F32)

    lam_init = 0.8 - 0.6 * math.exp(-0.3 * LAYER)
    return pl.pallas_call(
        functools.partial(_diff_attn_kernel, out_scale=1.0 - lam_init, lam_init=lam_init),
        grid=(batch, heads // group, nq),
        in_specs=[
            pl.BlockSpec(memory_space=pltpu.SMEM),
            pl.BlockSpec((tile, gw), lambda b, h, i: (b * nq + i, q_col // group + h)),
            pl.BlockSpec((seq, gw), lambda b, h, i: (b, k_col // group + h)),
            pl.BlockSpec((seq, gw), lambda b, h, i: (b, v_col // group + h)),
            pl.BlockSpec((group, 2, tile, tile), lambda b, h, i: (h, 0, 0, 0)),
            _resident(lam_vecs.shape),
            pl.BlockSpec((1, gw), lambda b, h, i: (0, h)),
        ],
        out_specs=pl.BlockSpec((tile, gw), lambda b, h, i: (b * nq + i, h)),
        out_shape=jax.ShapeDtypeStruct((n, heads * hw), BF16),
        scratch_shapes=[
            pltpu.VMEM((group, nq, hw + ATTN_ONES_ROWS, tile), BF16),
            pltpu.VMEM((group, hw + ATTN_ONES_ROWS, 2 * tile), F32),
            pltpu.VMEM((2 * group, tile, 2 * tile), F32),
        ],
        compiler_params=_params("parallel", "parallel", "arbitrary"),
        name="diff_attn",
    )(far, proj, proj, proj, band, lam_vecs, head_gain.reshape(1, -1))


def _mem_kv_kernel(mem_ref, g_ref, w_ref, k_ref, v_ref):
    width = k_ref.shape[-1]
    hb = (_rms(mem_ref[0]) * g_ref[...]).astype(BF16)
    k_ref[0] = jnp.dot(hb, w_ref[:, :width], preferred_element_type=F32).astype(BF16)
    v_ref[0] = jnp.dot(hb, w_ref[:, width:], preferred_element_type=F32).astype(BF16)


def _mem_kv(mem, gain, w_kv):
    batch, mlen, d = mem.shape
    width = w_kv.shape[1] // 2
    out = jax.ShapeDtypeStruct((batch, mlen, width), BF16)
    return pl.pallas_call(
        _mem_kv_kernel,
        grid=(batch,),
        in_specs=[
            pl.BlockSpec((1, mlen, d), lambda b: (b, 0, 0)),
            _resident((1, d)),
            _resident(w_kv.shape),
        ],
        out_specs=[pl.BlockSpec((1, mlen, width), lambda b: (b, 0, 0))] * 2,
        out_shape=[out, out],
        compiler_params=_params("parallel"),
        name="mem_kv",
    )(mem, gain.reshape(1, d), w_kv)


def _merge_kernel(x_ref, qx_ref, g0_ref, g1_ref, g2_ref, ym_ref, ya_ref, kx_ref, vx_ref,
                  bg_ref, wm_ref, wa_ref, wx_ref, wo_ref, o_ref):
    d = x_ref.shape[1]
    xw = qx_ref.shape[1]
    xhd = xw // X_HEADS
    heads = []
    for h in range(X_HEADS):
        cols = slice(h * xhd, (h + 1) * xhd)
        s = lax.dot_general(qx_ref[:, cols], kx_ref[0, :, cols], (((1,), (1,)), ((), ())),
                            preferred_element_type=F32) * (xhd ** -0.5)
        p = jnp.exp(s - jnp.max(s, axis=1, keepdims=True))
        pv = jnp.dot(p.astype(BF16), vx_ref[0, :, cols], preferred_element_type=F32)
        heads.append((pv / jnp.sum(p, axis=1, keepdims=True)).astype(BF16))
    yx = jnp.concatenate(heads, axis=1)

    def gate(g_ref, k):
        return _sigmoid(g_ref[...].astype(F32) + bg_ref[:, k * d:(k + 1) * d])

    merged = (gate(g0_ref, 0) * jnp.dot(ym_ref[...], wm_ref[...], preferred_element_type=F32)
              + gate(g1_ref, 1) * jnp.dot(ya_ref[...], wa_ref[...], preferred_element_type=F32)
              + gate(g2_ref, 2) * jnp.dot(yx, wx_ref[...], preferred_element_type=F32))
    o_ref[...] = x_ref[...] + jnp.dot(merged.astype(BF16), wo_ref[...], preferred_element_type=F32)


def _merge(x, proj, qx_col, g_col, y_m, y_a, kx, vx, b_gate, w_m, w_a, w_x, w_o, *, batch, seq):
    n, d = x.shape
    tm = min(MERGE_ROWS, seq)
    nt = seq // tm

    def row(b, t):
        return b * nt + t

    def proj_spec(col):
        return pl.BlockSpec((tm, d), lambda b, t: (row(b, t), col))

    def mem_spec(a):
        return pl.BlockSpec((1,) + a.shape[1:], lambda b, t: (b, 0, 0))

    return pl.pallas_call(
        _merge_kernel,
        grid=(batch, nt),
        in_specs=[
            pl.BlockSpec((tm, d), lambda b, t: (row(b, t), 0)),
            proj_spec(qx_col), proj_spec(g_col), proj_spec(g_col + 1), proj_spec(g_col + 2),
            pl.BlockSpec((tm, y_m.shape[1]), lambda b, t: (row(b, t), 0)),
            pl.BlockSpec((tm, y_a.shape[1]), lambda b, t: (row(b, t), 0)),
            mem_spec(kx), mem_spec(vx),
            _resident((1, b_gate.shape[0])),
            _resident(w_m.shape), _resident(w_a.shape), _resident(w_x.shape), _resident(w_o.shape),
        ],
        out_specs=pl.BlockSpec((tm, d), lambda b, t: (row(b, t), 0)),
        out_shape=jax.ShapeDtypeStruct((n, d), F32),
        compiler_params=_params("parallel", "parallel"),
        name="merge",
    )(x, proj, proj, proj, proj, y_m, y_a, kx, vx, b_gate.reshape(1, -1), w_m, w_a, w_x, w_o)


def kernel(x, mem, positions, rel_bias, ffn1_norm, ffn1_w_gu, ffn1_w_down, mix_norm, mem_norm, w_in, b_if, b_gate, m_conv_w, m_conv_b, m_wq, m_wk, m_wv, m_head_norm, a_lam_q1, a_lam_k1, a_lam_q2, a_lam_k2, a_head_norm, w_mem_kv, w_branch_m, w_branch_a, w_branch_x, w_out, ffn2_norm, ffn2_w_gu, ffn2_w_down, final_norm):
    del positions
    batch, seq, d = x.shape
    n = batch * seq
    l = LAYER
    m_inner = m_conv_w.shape[-1]
    a_width = a_head_norm.shape[-1]
    x_width = w_mem_kv.shape[-1] // 2
    bf = lambda a: a.astype(BF16)

    gate_lo = 2 * m_inner
    gate_hi = gate_lo + 2 * M_HEADS
    w_all = bf(w_in[l])
    w_tail = w_all[:, gate_hi:]
    w_gate = jnp.pad(w_all[:, gate_lo:gate_hi], ((0, 0), (0, LANES - 2 * M_HEADS)))

    x0 = x.reshape(n, d)
    x1 = _ffn(x0, ffn1_norm[l], bf(ffn1_w_gu[l]), bf(ffn1_w_down[l]), final_norm, final_norm=False)

    proj, gates = _inproj(x1, mix_norm[l], w_all, gate_lo, w_tail, w_gate)

    m_hd = m_inner // M_HEADS
    q_m, k_m, v_m = _mlstm_qkv(proj, m_conv_w[l], m_conv_b[l], bf(m_wq[l]), bf(m_wk[l] * m_hd ** -0.5),
                               bf(m_wv[l]), batch=batch, seq=seq)
    gates_t = gates[:, :2 * M_HEADS].reshape(batch, seq, 2, M_HEADS).transpose(0, 3, 2, 1)
    y_m = _mlstm_scan(q_m, k_m, v_m, gates_t, b_if[l], proj, m_inner // m_hd, m_head_norm[l],
                      batch=batch, seq=seq)

    a_hw = a_width // A_HEADS
    qa_col = 2 * m_inner // a_hw
    lam_vecs = jnp.stack([a_lam_q1[l], a_lam_k1[l], a_lam_q2[l], a_lam_k2[l]])
    y_a = _diff_attn(proj, qa_col, qa_col + A_HEADS, qa_col + 2 * A_HEADS, rel_bias, lam_vecs,
                     a_head_norm[l], batch=batch, seq=seq)

    kx, vx = _mem_kv(mem, mem_norm[l], bf(w_mem_kv[l]))
    qx_col = (2 * m_inner + 3 * a_width) // d
    x2 = _merge(x1, proj, qx_col, qx_col + x_width // d, y_m, y_a, kx, vx, b_gate[l],
                bf(w_branch_m[l]), bf(w_branch_a[l]), bf(w_branch_x[l]), bf(w_out[l]),
                batch=batch, seq=seq)

    out = _ffn(x2, ffn2_norm[l], bf(ffn2_w_gu[l]), bf(ffn2_w_down[l]), final_norm, final_norm=True)
    return out.reshape(batch, seq, d)
```

```python
import functools
import math

import jax
import jax.numpy as jnp
from jax import lax
from jax.experimental import pallas as pl
from jax.experimental.pallas import tpu as pltpu

EPS = 1e-6
BF16 = jnp.bfloat16
F32 = jnp.float32
NEG_BIG = -1e30

M_HEADS = 4
M_CONV = 4
A_HEADS = 8
X_HEADS = 4
N_BRANCH = 3
REL_BUCKETS = 32
REL_MAX_DIST = 128
LAYER = 0

VMEM_LIMIT_BYTES = 56 * 1024 * 1024

FFN_ROWS = 1024
FFN_SUBTILES = 4
INPROJ_ROWS = 2048
INPROJ_COLS = 1024
QKV_ROWS = 4096
QKV_SUBTILES = 16
MLSTM_CHUNK = 256
MLSTM_HEAD_GROUP = 2
ATTN_TILE = 256
ATTN_HEAD_GROUP = 4
LANES = 128
ATTN_STRIP = LANES
ATTN_PV_COLS = 256
ATTN_ONES_ROWS = 16
LOG2_E = math.log2(math.e)
MERGE_ROWS = 512
CONV_HALO = 16


def _params(*sem):
    return pltpu.CompilerParams(dimension_semantics=sem, vmem_limit_bytes=VMEM_LIMIT_BYTES)


def _resident(shape):
    zeros = (0,) * len(shape)
    return pl.BlockSpec(shape, lambda *_: zeros, pipeline_mode=pl.Buffered(1))


def _rms(x):
    return x * lax.rsqrt(jnp.mean(x * x, axis=-1, keepdims=True) + EPS)


def _sigmoid(x):
    return 1.0 / (1.0 + jnp.exp(-x))


def _ffn_kernel(x_ref, g_ref, wgu_ref, wd_ref, fg_ref, o_ref, *, d_ff, final_norm):
    sub = x_ref.shape[0] // FFN_SUBTILES
    for r0 in range(0, x_ref.shape[0], sub):
        x = x_ref[r0:r0 + sub, :]
        hb = (_rms(x) * g_ref[...]).astype(BF16)
        gate = jnp.dot(hb, wgu_ref[:, :d_ff], preferred_element_type=F32)
        up = jnp.dot(hb, wgu_ref[:, d_ff:], preferred_element_type=F32)
        act = (gate * _sigmoid(gate) * up).astype(BF16)
        y = x + 0.5 * jnp.dot(act, wd_ref[...], preferred_element_type=F32)
        if final_norm:
            y = _rms(y) * fg_ref[...]
        o_ref[r0:r0 + sub, :] = y


def _ffn(x, gain, w_gu, w_down, final_gain, *, final_norm):
    n, d = x.shape
    d_ff = w_down.shape[0]
    tm = min(FFN_ROWS, n)
    return pl.pallas_call(
        functools.partial(_ffn_kernel, d_ff=d_ff, final_norm=final_norm),
        grid=(n // tm,),
        in_specs=[
            pl.BlockSpec((tm, d), lambda i: (i, 0)),
            _resident((1, d)),
            _resident((d, 2 * d_ff)),
            _resident((d_ff, d)),
            _resident((1, d)),
        ],
        out_specs=pl.BlockSpec((tm, d), lambda i: (i, 0)),
        out_shape=jax.ShapeDtypeStruct((n, d), F32),
        compiler_params=_params("parallel"),
        name="ffn_final" if final_norm else "ffn",
    )(x, gain.reshape(1, d), w_gu, w_down, final_gain.reshape(1, d))


def _inproj_kernel(x_ref, g_ref, w_head_ref, w_tail_ref, wgate_ref, p_ref, gates_ref, h_scr, *, head_blocks):
    j = pl.program_id(1)

    @pl.when(j == 0)
    def _():
        hb = (_rms(x_ref[...]) * g_ref[...]).astype(BF16)
        h_scr[...] = hb
        gates_ref[...] = jnp.dot(hb, wgate_ref[...], preferred_element_type=F32)

    @pl.when(j < head_blocks)
    def _():
        p_ref[...] = jnp.dot(h_scr[...], w_head_ref[...], preferred_element_type=F32).astype(BF16)

    @pl.when(j >= head_blocks)
    def _():
        p_ref[...] = jnp.dot(h_scr[...], w_tail_ref[...], preferred_element_type=F32).astype(BF16)


def _inproj(x, gain, w_all, head_width, w_tail, w_gate):
    n, d = x.shape
    tm = min(INPROJ_ROWS, n)
    tn = INPROJ_COLS
    head_blocks = head_width // tn
    tail_blocks = w_tail.shape[1] // tn
    width = head_width + w_tail.shape[1]
    return pl.pallas_call(
        functools.partial(_inproj_kernel, head_blocks=head_blocks),
        grid=(n // tm, width // tn),
        in_specs=[
            pl.BlockSpec((tm, d), lambda i, j: (i, 0)),
            _resident((1, d)),
            pl.BlockSpec((d, tn), lambda i, j: (0, jnp.minimum(j, head_blocks - 1))),
            pl.BlockSpec((d, tn), lambda i, j: (0, jnp.where(j < head_blocks, tail_blocks - 1, j - head_blocks))),
            _resident(w_gate.shape),
        ],
        out_specs=[
            pl.BlockSpec((tm, tn), lambda i, j: (i, j)),
            pl.BlockSpec((tm, w_gate.shape[1]), lambda i, j: (i, 0)),
        ],
        out_shape=[
            jax.ShapeDtypeStruct((n, width), BF16),
            jax.ShapeDtypeStruct((n, w_gate.shape[1]), F32),
        ],
        scratch_shapes=[pltpu.VMEM((tm, d), BF16)],
        compiler_params=_params("parallel", "arbitrary"),
        name="inproj",
    )(x, gain.reshape(1, d), w_all, w_tail, w_gate)


def _mlstm_qkv_kernel(x_ref, halo_ref, cw_ref, cb_ref, wq_ref, wk_ref, wv_ref,
                      q_ref, k_ref, v_ref):
    ts = x_ref.shape[0]
    xb = x_ref[...]
    v_ref[...] = jnp.dot(xb, wv_ref[0], preferred_element_type=F32).astype(BF16)
    prev = jnp.where(pl.program_id(2) > 0, halo_ref[...].astype(F32), 0.0)
    full = jnp.concatenate([prev, xb.astype(F32)], axis=0)
    cw = cw_ref[...]
    sub = ts // QKV_SUBTILES
    for r0 in range(0, ts, sub):
        conv = cb_ref[...]
        for back in range(M_CONV):
            tap = M_CONV - 1 - back
            start = CONV_HALO + r0 - back
            conv = conv + cw[tap:tap + 1, :] * full[start:start + sub, :]
        xc = (conv * _sigmoid(conv)).astype(BF16)
        q_ref[r0:r0 + sub, :] = jnp.dot(xc, wq_ref[0], preferred_element_type=F32).astype(BF16)
        k_ref[r0:r0 + sub, :] = jnp.dot(xc, wk_ref[0], preferred_element_type=F32).astype(BF16)


def _mlstm_qkv(proj, conv_w, conv_b, wq, wk, wv, *, batch, seq):
    heads, hd, _ = wq.shape
    n = batch * seq
    ts = min(QKV_ROWS, seq)
    nt = seq // ts
    halo_per_tile = ts // CONV_HALO

    def x_map(h, b, t):
        return (b * nt + t, h)

    def halo_map(h, b, t):
        return (jnp.maximum((b * nt + t) * halo_per_tile - 1, 0), h)

    def w_map(h, b, t):
        return (h, 0, 0)

    out = jax.ShapeDtypeStruct((n, heads * hd), BF16)
    return pl.pallas_call(
        _mlstm_qkv_kernel,
        grid=(heads, batch, nt),
        in_specs=[
            pl.BlockSpec((ts, hd), x_map),
            pl.BlockSpec((CONV_HALO, hd), halo_map),
            pl.BlockSpec((M_CONV, hd), lambda h, b, t: (0, h)),
            pl.BlockSpec((1, hd), lambda h, b, t: (0, h)),
            pl.BlockSpec((1, hd, hd), w_map),
            pl.BlockSpec((1, hd, hd), w_map),
            pl.BlockSpec((1, hd, hd), w_map),
        ],
        out_specs=[pl.BlockSpec((ts, hd), x_map)] * 3,
        out_shape=[out, out, out],
        compiler_params=_params("parallel", "parallel", "parallel"),
        name="mlstm_qkv",
    )(proj, proj, conv_w, conv_b.reshape(1, -1), wq, wk, wv)


def _mlstm_scan_kernel(q_ref, k_ref, v_ref, g_ref, bias_ref, o_ref, gain_ref, y_ref,
                       c_scr, n_scr, m_scr):
    @pl.when(pl.program_id(2) == 0)
    def _():
        c_scr[...] = jnp.zeros_like(c_scr)
        n_scr[...] = jnp.zeros_like(n_scr)
        m_scr[...] = jnp.zeros_like(m_scr)

    group, hd, _ = c_scr.shape
    chunk = q_ref.shape[0]
    heads = range(group)
    cols = [slice(g * hd, (g + 1) * hd) for g in heads]
    q = [q_ref[:, cols[g]] for g in heads]
    k = [k_ref[:, cols[g]] for g in heads]
    v = [v_ref[:, cols[g]] for g in heads]
    c_prev = [c_scr[g] for g in heads]
    n_prev = [n_scr[g] for g in heads]
    m_prev = [m_scr[g] for g in heads]

    qk = [lax.dot_general(q[g], k[g], (((1,), (1,)), ((), ())), preferred_element_type=F32) for g in heads]
    qc = [jnp.dot(q[g], c_prev[g].astype(BF16), preferred_element_type=F32) for g in heads]

    t_idx = lax.broadcasted_iota(jnp.int32, (chunk, chunk), 0)
    s_idx = lax.broadcasted_iota(jnp.int32, (chunk, chunk), 1)
    causal = s_idx <= t_idx
    diag = s_idx == t_idx

    def to_col(row):
        return jnp.sum(jnp.where(diag, row, 0.0), axis=1, keepdims=True)

    def to_row(col):
        return jnp.sum(jnp.where(diag, col, 0.0), axis=0, keepdims=True)

    s, w_inter, m_t, w_s, decay, m_new = [], [], [], [], [], []
    for g in heads:
        gates = g_ref[0, g] + bias_ref[g]
        ig = gates[0:1, :]
        fpre = gates[1:2, :]
        lf = jnp.minimum(fpre, 0.0) - jnp.log1p(jnp.exp(-jnp.abs(fpre)))
        b_col = jnp.sum(jnp.where(causal, lf, 0.0), axis=1, keepdims=True)
        b_row = to_row(b_col)
        dlog = jnp.where(causal, b_col - b_row + ig, NEG_BIG)
        inter = b_col + m_prev[g]
        m_t.append(jnp.maximum(inter, jnp.max(dlog, axis=1, keepdims=True)))
        s.append(qk[g] * jnp.exp(dlog - m_t[g]))
        w_inter.append(jnp.exp(inter - m_t[g]))
        b_last = b_col[chunk - 1:chunk, :]
        tail = b_last - b_row + ig
        m_new.append(jnp.maximum(b_last + m_prev[g], jnp.max(tail, axis=1, keepdims=True)))
        decay.append(jnp.exp(b_last + m_prev[g] - m_new[g]))
        w_s.append(jnp.exp(to_col(tail) - m_new[g]))

    sv = [jnp.dot(s[g].astype(BF16), v[g], preferred_element_type=F32) for g in heads]
    kw = [k[g].astype(F32) * w_s[g] for g in heads]
    kv = [jnp.dot(kw[g].T.astype(BF16), v[g], preferred_element_type=F32) for g in heads]

    for g in heads:
        num = sv[g] + w_inter[g] * qc[g]
        qn = jnp.sum(q[g].astype(F32) * n_prev[g], axis=1, keepdims=True)
        den = jnp.sum(s[g], axis=1, keepdims=True) + w_inter[g] * qn
        h = num / jnp.maximum(jnp.abs(den), jnp.exp(-m_t[g]))
        c_scr[g] = decay[g] * c_prev[g] + kv[g]
        n_scr[g] = decay[g] * n_prev[g] + jnp.sum(kw[g], axis=0, keepdims=True)
        m_scr[g] = m_new[g]
        gate = _sigmoid(o_ref[:, cols[g]].astype(F32))
        y_ref[:, cols[g]] = (gate * (_rms(h) * gain_ref[:, cols[g]])).astype(BF16)


def _mlstm_scan(q, k, v, gates_t, b_if, proj, o_col_block, head_gain, *, batch, seq):
    n, width = q.shape
    heads = gates_t.shape[1]
    hd = width // heads
    group = MLSTM_HEAD_GROUP
    gw = group * hd
    assert heads % group == 0 and o_col_block % group == 0
    chunk = min(MLSTM_CHUNK, seq)
    nc = seq // chunk

    def qkv_map(b, h, c):
        return (b * nc + c, h)

    return pl.pallas_call(
        _mlstm_scan_kernel,
        grid=(batch, heads // group, nc),
        in_specs=[
            pl.BlockSpec((chunk, gw), qkv_map),
            pl.BlockSpec((chunk, gw), qkv_map),
            pl.BlockSpec((chunk, gw), qkv_map),
            pl.BlockSpec((1, group, 2, chunk), lambda b, h, c: (b, h, 0, c)),
            pl.BlockSpec((group, 2, 1), lambda b, h, c: (h, 0, 0)),
            pl.BlockSpec((chunk, gw), lambda b, h, c: (b * nc + c, o_col_block // group + h)),
            pl.BlockSpec((1, gw), lambda b, h, c: (0, h)),
        ],
        out_specs=pl.BlockSpec((chunk, gw), qkv_map),
        out_shape=jax.ShapeDtypeStruct((n, width), BF16),
        scratch_shapes=[pltpu.VMEM((group, hd, hd), F32), pltpu.VMEM((group, 1, hd), F32),
                        pltpu.VMEM((group, 1, 1), F32)],
        compiler_params=_params("parallel", "parallel", "arbitrary"),
        name="mlstm_scan",
    )(q, k, v, gates_t, b_if.reshape(2, heads, 1).transpose(1, 0, 2), proj, head_gain.reshape(1, -1))


def _diff_attn_kernel(far_ref, q_ref, k_ref, v_ref, band_ref, lam_ref, gain_ref, y_ref,
                      vt_scr, acc_scr, st_scr, *, out_scale, lam_init):
    group, n_tiles, rows, tile = vt_scr.shape
    width = q_ref.shape[1] // group
    head0 = pl.program_id(1) * group
    qi = pl.program_id(2)
    half = width // 2

    @pl.when(qi == 0)
    def _():
        ones = jnp.ones((rows - width, tile), BF16)
        for g in range(group):
            for j in range(n_tiles):
                v_tile = v_ref[j * tile:(j + 1) * tile, g * width:(g + 1) * width]
                vt_scr[g, j] = jnp.concatenate([v_tile.astype(F32).T.astype(BF16), ones], axis=0)

    q_all = q_ref[...]
    lane = lax.broadcasted_iota(jnp.int32, (tile, width), 1)
    qs = []
    for g in range(group):
        q = q_all[:, g * width:(g + 1) * width].astype(F32) * (half ** -0.5 * LOG2_E)
        stacked = jnp.concatenate([jnp.where(lane < half, q, 0.0), jnp.where(lane >= half, q, 0.0)], axis=0)
        qs.append(stacked.astype(BF16))

    acc_scr[...] = jnp.zeros_like(acc_scr)
    lam_v = lam_ref[...]
    lam = (jnp.exp(jnp.sum(lam_v[0:1, :] * lam_v[1:2, :], axis=1, keepdims=True))
           - jnp.exp(jnp.sum(lam_v[2:3, :] * lam_v[3:4, :], axis=1, keepdims=True)) + lam_init)

    def scores(kj, g):
        start = pl.multiple_of(kj * tile, tile)
        return lax.dot_general(k_ref[pl.ds(start, tile), g * width:(g + 1) * width], qs[g],
                               (((1,), (1,)), ((), ())), preferred_element_type=F32)

    def block(tiles, maxes):
        units = [(kj, band_idx, g) for kj, band_idx in tiles for g in range(group)]
        all_scores, parked = [], 0
        for kj, band_idx, g in units:
            if band_idx is None:
                all_scores.append(scores(kj, g))
            else:
                st_scr[parked] = scores(kj, g)
                all_scores.append(parked)
                parked += 1
        maxes = list(maxes)
        for st, (kj, band_idx, g) in zip(all_scores, units):
            m_prev = maxes[g]
            if band_idx is None:
                shift = far_bias[g]
            else:
                shift = 0.0
                band = band_ref[g, band_idx]
                st = st_scr[st] + jnp.concatenate([band, band], axis=1)
            vt = vt_scr[g, kj]
            m_cols = []
            for c0 in range(0, 2 * tile, ATTN_PV_COLS):
                p_cols, a_cols = [], []
                for c in range(c0, c0 + ATTN_PV_COLS, ATTN_STRIP):
                    cols = slice(c, c + ATTN_STRIP)
                    s_c = st[:, cols]
                    m_c = jnp.maximum(m_prev[:, cols], jnp.max(s_c, axis=0, keepdims=True) + shift)
                    a_cols.append(jnp.exp2(m_prev[:, cols] - m_c))
                    p_cols.append(jnp.exp2(s_c - (m_c - shift)).astype(BF16))
                    m_cols.append(m_c)
                pv = jnp.dot(vt, jnp.concatenate(p_cols, axis=1), preferred_element_type=F32)
                wide = slice(c0, c0 + ATTN_PV_COLS)
                acc_scr[g, :, wide] = jnp.concatenate(a_cols, axis=1) * acc_scr[g, :, wide] + pv
            maxes[g] = jnp.concatenate(m_cols, axis=1)
        return tuple(maxes)

    def far(first, count):
        return tuple((first + i, None) for i in range(count))

    far_bias = tuple(far_ref[head0 + g] * LOG2_E for g in range(group))
    n_far = jnp.maximum(qi - 1, 0)
    n_octs = n_far // 8
    n_quads = n_far // 4
    rest = n_far - 4 * n_quads
    maxes = tuple(jnp.full((1, 2 * tile), NEG_BIG, F32) for _ in range(group))
    maxes = lax.fori_loop(0, n_octs, lambda i, c: block(far(8 * i, 8), c), maxes)
    maxes = lax.cond(n_quads > 2 * n_octs, lambda: block(far(8 * n_octs, 4), maxes), lambda: maxes)
    near = ((qi - 1, 0), (qi, 1))

    def last_tiles():
        return lax.switch(rest, [functools.partial(block, far(n_far - left, left) + near, maxes)
                                 for left in range(4)])

    lax.cond(qi > 0, last_tiles, lambda: block(((qi, 1),), maxes))

    for g in range(group):
        o = acc_scr[g, :width, :] / acc_scr[g, width:width + 1, :]
        out = (o[:, :tile] - lam * o[:, tile:]).T
        gain = gain_ref[:, g * width:(g + 1) * width]
        y_ref[:, g * width:(g + 1) * width] = (_rms(out) * gain * out_scale).astype(BF16)


def _t5_bucket(n):
    max_exact = REL_BUCKETS // 2
    nf = jnp.maximum(n, 1).astype(F32)
    large = max_exact + (jnp.log(nf / max_exact) / math.log(REL_MAX_DIST / max_exact)
                         * (REL_BUCKETS - max_exact)).astype(jnp.int32)
    large = jnp.minimum(large, REL_BUCKETS - 1)
    return jnp.where(n < max_exact, n, large)


def _diff_attn(proj, q_col, k_col, v_col, rel_bias, lam_vecs, head_gain, *, batch, seq):
    n = proj.shape[0]
    heads = rel_bias.shape[1]
    hw = head_gain.shape[0] // heads
    tile = min(ATTN_TILE, seq)
    nq = seq // tile
    assert tile >= REL_MAX_DIST, "distances beyond one tile must all fall in the last bucket"
    group = ATTN_HEAD_GROUP
    gw = group * hw
    assert heads % group == 0 and q_col % group == 0 and k_col % group == 0 and v_col % group == 0

    bucket = _t5_bucket(jnp.arange(2 * tile, dtype=jnp.int32))
    by_dist = jnp.zeros((heads, 2 * tile), F32)
    for b in range(REL_BUCKETS):
        by_dist = jnp.where(bucket[None] == b, rel_bias[b].astype(F32)[:, None], by_dist)
    masked = jnp.full((heads, tile - 1), NEG_BIG, F32)
    line = jnp.concatenate([masked, by_dist * LOG2_E, masked[:, :2]], axis=1)
    length = line.shape[1] - 1
    skew = jnp.tile(line, (1, tile))[:, :tile * length].reshape(heads, tile, length)
    band = jnp.stack([skew[:, :, 2 * tile - 1:3 * tile - 1], skew[:, :, tile - 1:2 * tile - 1]], axis=1)
    far = rel_bias[REL_BUCKETS - 1].astype(F32)

    lam_init = 0.8 - 0.6 * math.exp(-0.3 * LAYER)
    return pl.pallas_call(
        functools.partial(_diff_attn_kernel, out_scale=1.0 - lam_init, lam_init=lam_init),
        grid=(batch, heads // group, nq),
        in_specs=[
            pl.BlockSpec(memory_space=pltpu.SMEM),
            pl.BlockSpec((tile, gw), lambda b, h, i: (b * nq + i, q_col // group + h)),
            pl.BlockSpec((seq, gw), lambda b, h, i: (b, k_col // group + h)),
            pl.BlockSpec((seq, gw), lambda b, h, i: (b, v_col // group + h)),
            pl.BlockSpec((group, 2, tile, tile), lambda b, h, i: (h, 0, 0, 0)),
            _resident(lam_vecs.shape),
            pl.BlockSpec((1, gw), lambda b, h, i: (0, h)),
        ],
        out_specs=pl.BlockSpec((tile, gw), lambda b, h, i: (b * nq + i, h)),
        out_shape=jax.ShapeDtypeStruct((n, heads * hw), BF16),
        scratch_shapes=[
            pltpu.VMEM((group, nq, hw + ATTN_ONES_ROWS, tile), BF16),
            pltpu.VMEM((group, hw + ATTN_ONES_ROWS, 2 * tile), F32),
            pltpu.VMEM((2 * group, tile, 2 * tile), F32),
        ],
        compiler_params=_params("parallel", "parallel", "arbitrary"),
        name="diff_attn",
    )(far, proj, proj, proj, band, lam_vecs, head_gain.reshape(1, -1))


def _mem_kv_kernel(mem_ref, g_ref, w_ref, k_ref, v_ref):
    width = k_ref.shape[-1]
    hb = (_rms(mem_ref[0]) * g_ref[...]).astype(BF16)
    k_ref[0] = jnp.dot(hb, w_ref[:, :width], preferred_element_type=F32).astype(BF16)
    v_ref[0] = jnp.dot(hb, w_ref[:, width:], preferred_element_type=F32).astype(BF16)


def _mem_kv(mem, gain, w_kv):
    batch, mlen, d = mem.shape
    width = w_kv.shape[1] // 2
    out = jax.ShapeDtypeStruct((batch, mlen, width), BF16)
    return pl.pallas_call(
        _mem_kv_kernel,
        grid=(batch,),
        in_specs=[
            pl.BlockSpec((1, mlen, d), lambda b: (b, 0, 0)),
            _resident((1, d)),
            _resident(w_kv.shape),
        ],
        out_specs=[pl.BlockSpec((1, mlen, width), lambda b: (b, 0, 0))] * 2,
        out_shape=[out, out],
        compiler_params=_params("parallel"),
        name="mem_kv",
    )(mem, gain.reshape(1, d), w_kv)


def _merge_kernel(x_ref, qx_ref, g0_ref, g1_ref, g2_ref, ym_ref, ya_ref, kx_ref, vx_ref,
                  bg_ref, wm_ref, wa_ref, wx_ref, wo_ref, o_ref):
    d = x_ref.shape[1]
    xw = qx_ref.shape[1]
    xhd = xw // X_HEADS
    heads = []
    for h in range(X_HEADS):
        cols = slice(h * xhd, (h + 1) * xhd)
        s = lax.dot_general(qx_ref[:, cols], kx_ref[0, :, cols], (((1,), (1,)), ((), ())),
                            preferred_element_type=F32) * (xhd ** -0.5)
        p = jnp.exp(s - jnp.max(s, axis=1, keepdims=True))
        pv = jnp.dot(p.astype(BF16), vx_ref[0, :, cols], preferred_element_type=F32)
        heads.append((pv / jnp.sum(p, axis=1, keepdims=True)).astype(BF16))
    yx = jnp.concatenate(heads, axis=1)

    def gate(g_ref, k):
        return _sigmoid(g_ref[...].astype(F32) + bg_ref[:, k * d:(k + 1) * d])

    merged = (gate(g0_ref, 0) * jnp.dot(ym_ref[...], wm_ref[...], preferred_element_type=F32)
              + gate(g1_ref, 1) * jnp.dot(ya_ref[...], wa_ref[...], preferred_element_type=F32)
              + gate(g2_ref, 2) * jnp.dot(yx, wx_ref[...], preferred_element_type=F32))
    o_ref[...] = x_ref[...] + jnp.dot(merged.astype(BF16), wo_ref[...], preferred_element_type=F32)


def _merge(x, proj, qx_col, g_col, y_m, y_a, kx, vx, b_gate, w_m, w_a, w_x, w_o, *, batch, seq):
    n, d = x.shape
    tm = min(MERGE_ROWS, seq)
    nt = seq // tm

    def row(b, t):
        return b * nt + t

    def proj_spec(col):
        return pl.BlockSpec((tm, d), lambda b, t: (row(b, t), col))

    def mem_spec(a):
        return pl.BlockSpec((1,) + a.shape[1:], lambda b, t: (b, 0, 0))

    return pl.pallas_call(
        _merge_kernel,
        grid=(batch, nt),
        in_specs=[
            pl.BlockSpec((tm, d), lambda b, t: (row(b, t), 0)),
            proj_spec(qx_col), proj_spec(g_col), proj_spec(g_col + 1), proj_spec(g_col + 2),
            pl.BlockSpec((tm, y_m.shape[1]), lambda b, t: (row(b, t), 0)),
            pl.BlockSpec((tm, y_a.shape[1]), lambda b, t: (row(b, t), 0)),
            mem_spec(kx), mem_spec(vx),
            _resident((1, b_gate.shape[0])),
            _resident(w_m.shape), _resident(w_a.shape), _resident(w_x.shape), _resident(w_o.shape),
        ],
        out_specs=pl.BlockSpec((tm, d), lambda b, t: (row(b, t), 0)),
        out_shape=jax.ShapeDtypeStruct((n, d), F32),
        compiler_params=_params("parallel", "parallel"),
        name="merge",
    )(x, proj, proj, proj, proj, y_m, y_a, kx, vx, b_gate.reshape(1, -1), w_m, w_a, w_x, w_o)


def kernel(x, mem, positions, rel_bias, ffn1_norm, ffn1_w_gu, ffn1_w_down, mix_norm, mem_norm, w_in, b_if, b_gate, m_conv_w, m_conv_b, m_wq, m_wk, m_wv, m_head_norm, a_lam_q1, a_lam_k1, a_lam_q2, a_lam_k2, a_head_norm, w_mem_kv, w_branch_m, w_branch_a, w_branch_x, w_out, ffn2_norm, ffn2_w_gu, ffn2_w_down, final_norm):
    del positions
    batch, seq, d = x.shape
    n = batch * seq
    l = LAYER
    m_inner = m_conv_w.shape[-1]
    a_width = a_head_norm.shape[-1]
    x_width = w_mem_kv.shape[-1] // 2
    bf = lambda a: a.astype(BF16)

    gate_lo = 2 * m_inner
    gate_hi = gate_lo + 2 * M_HEADS
    w_all = bf(w_in[l])
    w_tail = w_all[:, gate_hi:]
    w_gate = jnp.pad(w_all[:, gate_lo:gate_hi], ((0, 0), (0, LANES - 2 * M_HEADS)))

    x0 = x.reshape(n, d)
    x1 = _ffn(x0, ffn1_norm[l], bf(ffn1_w_gu[l]), bf(ffn1_w_down[l]), final_norm, final_norm=False)

    proj, gates = _inproj(x1, mix_norm[l], w_all, gate_lo, w_tail, w_gate)

    m_hd = m_inner // M_HEADS
    q_m, k_m, v_m = _mlstm_qkv(proj, m_conv_w[l], m_conv_b[l], bf(m_wq[l]), bf(m_wk[l] * m_hd ** -0.5),
                               bf(m_wv[l]), batch=batch, seq=seq)
    gates_t = gates[:, :2 * M_HEADS].reshape(batch, seq, 2, M_HEADS).transpose(0, 3, 2, 1)
    y_m = _mlstm_scan(q_m, k_m, v_m, gates_t, b_if[l], proj, m_inner // m_hd, m_head_norm[l],
                      batch=batch, seq=seq)

    a_hw = a_width // A_HEADS
    qa_col = 2 * m_inner // a_hw
    lam_vecs = jnp.stack([a_lam_q1[l], a_lam_k1[l], a_lam_q2[l], a_lam_k2[l]])
    y_a = _diff_attn(proj, qa_col, qa_col + A_HEADS, qa_col + 2 * A_HEADS, rel_bias, lam_vecs,
                     a_head_norm[l], batch=batch, seq=seq)

    kx, vx = _mem_kv(mem, mem_norm[l], bf(w_mem_kv[l]))
    qx_col = (2 * m_inner + 3 * a_width) // d
    x2 = _merge(x1, proj, qx_col, qx_col + x_width // d, y_m, y_a, kx, vx, b_gate[l],
                bf(w_branch_m[l]), bf(w_branch_a[l]), bf(w_branch_x[l]), bf(w_out[l]),
                batch=batch, seq=seq)

    out = _ffn(x2, ffn2_norm[l], bf(ffn2_w_gu[l]), bf(ffn2_w_down[l]), final_norm, final_norm=True)
    return out.reshape(batch, seq, d)
```

```python
import functools
import math

import jax
import jax.numpy as jnp
from jax import lax
from jax.experimental import pallas as pl
from jax.experimental.pallas import tpu as pltpu

EPS = 1e-6
BF16 = jnp.bfloat16
F32 = jnp.float32
NEG_BIG = -1e30

M_HEADS = 4
M_CONV = 4
A_HEADS = 8
X_HEADS = 4
N_BRANCH = 3
REL_BUCKETS = 32
REL_MAX_DIST = 128
LAYER = 0

VMEM_LIMIT_BYTES = 56 * 1024 * 1024

FFN_ROWS = 1024
FFN_SUBTILES = 4
INPROJ_ROWS = 2048
INPROJ_COLS = 1024
QKV_ROWS = 4096
QKV_SUBTILES = 16
MLSTM_CHUNK = 256
MLSTM_HEAD_GROUP = 2
ATTN_TILE = 256
ATTN_HEAD_GROUP = 4
LANES = 128
ATTN_STRIP = LANES
ATTN_PV_COLS = 256
ATTN_ONES_ROWS = 16
LOG2_E = math.log2(math.e)
MERGE_ROWS = 512
CONV_HALO = 16


def _params(*sem):
    return pltpu.CompilerParams(dimension_semantics=sem, vmem_limit_bytes=VMEM_LIMIT_BYTES)


def _resident(shape):
    zeros = (0,) * len(shape)
    return pl.BlockSpec(shape, lambda *_: zeros, pipeline_mode=pl.Buffered(1))


def _rms(x):
    return x * lax.rsqrt(jnp.mean(x * x, axis=-1, keepdims=True) + EPS)


def _sigmoid(x):
    return 1.0 / (1.0 + jnp.exp(-x))


def _ffn_kernel(x_ref, g_ref, wgu_ref, wd_ref, fg_ref, o_ref, *, d_ff, final_norm):
    sub = x_ref.shape[0] // FFN_SUBTILES
    for r0 in range(0, x_ref.shape[0], sub):
        x = x_ref[r0:r0 + sub, :]
        hb = (_rms(x) * g_ref[...]).astype(BF16)
        gate = jnp.dot(hb, wgu_ref[:, :d_ff], preferred_element_type=F32)
        up = jnp.dot(hb, wgu_ref[:, d_ff:], preferred_element_type=F32)
        act = (gate * _sigmoid(gate) * up).astype(BF16)
        y = x + 0.5 * jnp.dot(act, wd_ref[...], preferred_element_type=F32)
        if final_norm:
            y = _rms(y) * fg_ref[...]
        o_ref[r0:r0 + sub, :] = y


def _ffn(x, gain, w_gu, w_down, final_gain, *, final_norm):
    n, d = x.shape
    d_ff = w_down.shape[0]
    tm = min(FFN_ROWS, n)
    return pl.pallas_call(
        functools.partial(_ffn_kernel, d_ff=d_ff, final_norm=final_norm),
        grid=(n // tm,),
        in_specs=[
            pl.BlockSpec((tm, d), lambda i: (i, 0)),
            _resident((1, d)),
            _resident((d, 2 * d_ff)),
            _resident((d_ff, d)),
            _resident((1, d)),
        ],
        out_specs=pl.BlockSpec((tm, d), lambda i: (i, 0)),
        out_shape=jax.ShapeDtypeStruct((n, d), F32),
        compiler_params=_params("parallel"),
        name="ffn_final" if final_norm else "ffn",
    )(x, gain.reshape(1, d), w_gu, w_down, final_gain.reshape(1, d))


def _inproj_kernel(x_ref, g_ref, w_head_ref, w_tail_ref, wgate_ref, p_ref, gates_ref, h_scr, *, head_blocks):
    j = pl.program_id(1)

    @pl.when(j == 0)
    def _():
        hb = (_rms(x_ref[...]) * g_ref[...]).astype(BF16)
        h_scr[...] = hb
        gates_ref[...] = jnp.dot(hb, wgate_ref[...], preferred_element_type=F32)

    @pl.when(j < head_blocks)
    def _():
        p_ref[...] = jnp.dot(h_scr[...], w_head_ref[...], preferred_element_type=F32).astype(BF16)

    @pl.when(j >= head_blocks)
    def _():
        p_ref[...] = jnp.dot(h_scr[...], w_tail_ref[...], preferred_element_type=F32).astype(BF16)


def _inproj(x, gain, w_all, head_width, w_tail, w_gate):
    n, d = x.shape
    tm = min(INPROJ_ROWS, n)
    tn = INPROJ_COLS
    head_blocks = head_width // tn
    tail_blocks = w_tail.shape[1] // tn
    width = head_width + w_tail.shape[1]
    return pl.pallas_call(
        functools.partial(_inproj_kernel, head_blocks=head_blocks),
        grid=(n // tm, width // tn),
        in_specs=[
            pl.BlockSpec((tm, d), lambda i, j: (i, 0)),
            _resident((1, d)),
            pl.BlockSpec((d, tn), lambda i, j: (0, jnp.minimum(j, head_blocks - 1))),
            pl.BlockSpec((d, tn), lambda i, j: (0, jnp.where(j < head_blocks, tail_blocks - 1, j - head_blocks))),
            _resident(w_gate.shape),
        ],
        out_specs=[
            pl.BlockSpec((tm, tn), lambda i, j: (i, j)),
            pl.BlockSpec((tm, w_gate.shape[1]), lambda i, j: (i, 0)),
        ],
        out_shape=[
            jax.ShapeDtypeStruct((n, width), BF16),
            jax.ShapeDtypeStruct((n, w_gate.shape[1]), F32),
        ],
        scratch_shapes=[pltpu.VMEM((tm, d), BF16)],
        compiler_params=_params("parallel", "arbitrary"),
        name="inproj",
    )(x, gain.reshape(1, d), w_all, w_tail, w_gate)


def _mlstm_qkv_kernel(x_ref, halo_ref, cw_ref, cb_ref, wq_ref, wk_ref, wv_ref,
                      q_ref, k_ref, v_ref):
    ts = x_ref.shape[0]
    xb = x_ref[...]
    v_ref[...] = jnp.dot(xb, wv_ref[0], preferred_element_type=F32).astype(BF16)
    prev = jnp.where(pl.program_id(2) > 0, halo_ref[...].astype(F32), 0.0)
    full = jnp.concatenate([prev, xb.astype(F32)], axis=0)
    cw = cw_ref[...]
    sub = ts // QKV_SUBTILES
    for r0 in range(0, ts, sub):
        conv = cb_ref[...]
        for back in range(M_CONV):
            tap = M_CONV - 1 - back
            start = CONV_HALO + r0 - back
            conv = conv + cw[tap:tap + 1, :] * full[start:start + sub, :]
        xc = (conv * _sigmoid(conv)).astype(BF16)
        q_ref[r0:r0 + sub, :] = jnp.dot(xc, wq_ref[0], preferred_element_type=F32).astype(BF16)
        k_ref[r0:r0 + sub, :] = jnp.dot(xc, wk_ref[0], preferred_element_type=F32).astype(BF16)


def _mlstm_qkv(proj, conv_w, conv_b, wq, wk, wv, *, batch, seq):
    heads, hd, _ = wq.shape
    n = batch * seq
    ts = min(QKV_ROWS, seq)
    nt = seq // ts
    halo_per_tile = ts // CONV_HALO

    def x_map(h, b, t):
        return (b * nt + t, h)

    def halo_map(h, b, t):
        return (jnp.maximum((b * nt + t) * halo_per_tile - 1, 0), h)

    def w_map(h, b, t):
        return (h, 0, 0)

    out = jax.ShapeDtypeStruct((n, heads * hd), BF16)
    return pl.pallas_call(
        _mlstm_qkv_kernel,
        grid=(heads, batch, nt),
        in_specs=[
            pl.BlockSpec((ts, hd), x_map),
            pl.BlockSpec((CONV_HALO, hd), halo_map),
            pl.BlockSpec((M_CONV, hd), lambda h, b, t: (0, h)),
            pl.BlockSpec((1, hd), lambda h, b, t: (0, h)),
            pl.BlockSpec((1, hd, hd), w_map),
            pl.BlockSpec((1, hd, hd), w_map),
            pl.BlockSpec((1, hd, hd), w_map),
        ],
        out_specs=[pl.BlockSpec((ts, hd), x_map)] * 3,
        out_shape=[out, out, out],
        compiler_params=_params("parallel", "parallel", "parallel"),
        name="mlstm_qkv",
    )(proj, proj, conv_w, conv_b.reshape(1, -1), wq, wk, wv)


def _mlstm_scan_kernel(q_ref, k_ref, v_ref, g_ref, bias_ref, o_ref, gain_ref, y_ref,
                       c_scr, n_scr, m_scr):
    @pl.when(pl.program_id(2) == 0)
    def _():
        c_scr[...] = jnp.zeros_like(c_scr)
        n_scr[...] = jnp.zeros_like(n_scr)
        m_scr[...] = jnp.zeros_like(m_scr)

    group, hd, _ = c_scr.shape
    chunk = q_ref.shape[0]
    heads = range(group)
    cols = [slice(g * hd, (g + 1) * hd) for g in heads]
    q = [q_ref[:, cols[g]] for g in heads]
    k = [k_ref[:, cols[g]] for g in heads]
    v = [v_ref[:, cols[g]] for g in heads]
    c_prev = [c_scr[g] for g in heads]
    n_prev = [n_scr[g] for g in heads]
    m_prev = [m_scr[g] for g in heads]

    qk = [lax.dot_general(q[g], k[g], (((1,), (1,)), ((), ())), preferred_element_type=F32) for g in heads]
    qc = [jnp.dot(q[g], c_prev[g].astype(BF16), preferred_element_type=F32) for g in heads]

    t_idx = lax.broadcasted_iota(jnp.int32, (chunk, chunk), 0)
    s_idx = lax.broadcasted_iota(jnp.int32, (chunk, chunk), 1)
    causal = s_idx <= t_idx
    diag = s_idx == t_idx

    def to_col(row):
        return jnp.sum(jnp.where(diag, row, 0.0), axis=1, keepdims=True)

    def to_row(col):
        return jnp.sum(jnp.where(diag, col, 0.0), axis=0, keepdims=True)

    s, w_inter, m_t, w_s, decay, m_new = [], [], [], [], [], []
    for g in heads:
        gates = g_ref[0, g] + bias_ref[g]
        ig = gates[0:1, :]
        fpre = gates[1:2, :]
        lf = jnp.minimum(fpre, 0.0) - jnp.log1p(jnp.exp(-jnp.abs(fpre)))
        b_col = jnp.sum(jnp.where(causal, lf, 0.0), axis=1, keepdims=True)
        b_row = to_row(b_col)
        dlog = jnp.where(causal, b_col - b_row + ig, NEG_BIG)
        inter = b_col + m_prev[g]
        m_t.append(jnp.maximum(inter, jnp.max(dlog, axis=1, keepdims=True)))
        s.append(qk[g] * jnp.exp(dlog - m_t[g]))
        w_inter.append(jnp.exp(inter - m_t[g]))
        b_last = b_col[chunk - 1:chunk, :]
        tail = b_last - b_row + ig
        m_new.append(jnp.maximum(b_last + m_prev[g], jnp.max(tail, axis=1, keepdims=True)))
        decay.append(jnp.exp(b_last + m_prev[g] - m_new[g]))
        w_s.append(jnp.exp(to_col(tail) - m_new[g]))

    sv = [jnp.dot(s[g].astype(BF16), v[g], preferred_element_type=F32) for g in heads]
    kw = [k[g].astype(F32) * w_s[g] for g in heads]
    kv = [jnp.dot(kw[g].T.astype(BF16), v[g], preferred_element_type=F32) for g in heads]

    for g in heads:
        num = sv[g] + w_inter[g] * qc[g]
        qn = jnp.sum(q[g].astype(F32) * n_prev[g], axis=1, keepdims=True)
        den = jnp.sum(s[g], axis=1, keepdims=True) + w_inter[g] * qn
        h = num / jnp.maximum(jnp.abs(den), jnp.exp(-m_t[g]))
        c_scr[g] = decay[g] * c_prev[g] + kv[g]
        n_scr[g] = decay[g] * n_prev[g] + jnp.sum(kw[g], axis=0, keepdims=True)
        m_scr[g] = m_new[g]
        gate = _sigmoid(o_ref[:, cols[g]].astype(F32))
        y_ref[:, cols[g]] = (gate * (_rms(h) * gain_ref[:, cols[g]])).astype(BF16)


def _mlstm_scan(q, k, v, gates_t, b_if, proj, o_col_block, head_gain, *, batch, seq):
    n, width = q.shape
    heads = gates_t.shape[1]
    hd = width // heads
    group = MLSTM_HEAD_GROUP
    gw = group * hd
    assert heads % group == 0 and o_col_block % group == 0
    chunk = min(MLSTM_CHUNK, seq)
    nc = seq // chunk

    def qkv_map(b, h, c):
        return (b * nc + c, h)

    return pl.pallas_call(
        _mlstm_scan_kernel,
        grid=(batch, heads // group, nc),
        in_specs=[
            pl.BlockSpec((chunk, gw), qkv_map),
            pl.BlockSpec((chunk, gw), qkv_map),
            pl.BlockSpec((chunk, gw), qkv_map),
            pl.BlockSpec((1, group, 2, chunk), lambda b, h, c: (b, h, 0, c)),
            pl.BlockSpec((group, 2, 1), lambda b, h, c: (h, 0, 0)),
            pl.BlockSpec((chunk, gw), lambda b, h, c: (b * nc + c, o_col_block // group + h)),
            pl.BlockSpec((1, gw), lambda b, h, c: (0, h)),
        ],
        out_specs=pl.BlockSpec((chunk, gw), qkv_map),
        out_shape=jax.ShapeDtypeStruct((n, width), BF16),
        scratch_shapes=[pltpu.VMEM((group, hd, hd), F32), pltpu.VMEM((group, 1, hd), F32),
                        pltpu.VMEM((group, 1, 1), F32)],
        compiler_params=_params("parallel", "parallel", "arbitrary"),
        name="mlstm_scan",
    )(q, k, v, gates_t, b_if.reshape(2, heads, 1).transpose(1, 0, 2), proj, head_gain.reshape(1, -1))


def _diff_attn_kernel(far_ref, q_ref, k_ref, v_ref, band_ref, lam_ref, gain_ref, y_ref,
                      vt_scr, acc_scr, st_scr, *, out_scale, lam_init):
    group, n_tiles, rows, tile = vt_scr.shape
    width = q_ref.shape[1] // group
    head0 = pl.program_id(1) * group
    qi = pl.program_id(2)
    half = width // 2

    @pl.when(qi == 0)
    def _():
        ones = jnp.ones((rows - width, tile), BF16)
        for g in range(group):
            for j in range(n_tiles):
                v_tile = v_ref[j * tile:(j + 1) * tile, g * width:(g + 1) * width]
                vt_scr[g, j] = jnp.concatenate([v_tile.astype(F32).T.astype(BF16), ones], axis=0)

    q_all = q_ref[...]
    lane = lax.broadcasted_iota(jnp.int32, (tile, width), 1)
    qs = []
    for g in range(group):
        q = q_all[:, g * width:(g + 1) * width].astype(F32) * (half ** -0.5 * LOG2_E)
        stacked = jnp.concatenate([jnp.where(lane < half, q, 0.0), jnp.where(lane >= half, q, 0.0)], axis=0)
        qs.append(stacked.astype(BF16))

    acc_scr[...] = jnp.zeros_like(acc_scr)
    lam_v = lam_ref[...]
    lam = (jnp.exp(jnp.sum(lam_v[0:1, :] * lam_v[1:2, :], axis=1, keepdims=True))
           - jnp.exp(jnp.sum(lam_v[2:3, :] * lam_v[3:4, :], axis=1, keepdims=True)) + lam_init)

    def scores(kj, g):
        start = pl.multiple_of(kj * tile, tile)
        return lax.dot_general(k_ref[pl.ds(start, tile), g * width:(g + 1) * width], qs[g],
                               (((1,), (1,)), ((), ())), preferred_element_type=F32)

    def block(tiles, maxes):
        units = [(kj, band_idx, g) for kj, band_idx in tiles for g in range(group)]
        all_scores, parked = [], 0
        for kj, band_idx, g in units:
            if band_idx is None:
                all_scores.append(scores(kj, g))
            else:
                st_scr[parked] = scores(kj, g)
                all_scores.append(parked)
                parked += 1
        maxes = list(maxes)
        for st, (kj, band_idx, g) in zip(all_scores, units):
            m_prev = maxes[g]
            if band_idx is None:
                shift = far_bias[g]
            else:
                shift = 0.0
                band = band_ref[g, band_idx]
                st = st_scr[st] + jnp.concatenate([band, band], axis=1)
            vt = vt_scr[g, kj]
            m_cols = []
            for c0 in range(0, 2 * tile, ATTN_PV_COLS):
                p_cols, a_cols = [], []
                for c in range(c0, c0 + ATTN_PV_COLS, ATTN_STRIP):
                    cols = slice(c, c + ATTN_STRIP)
                    s_c = st[:, cols]
                    m_c = jnp.maximum(m_prev[:, cols], jnp.max(s_c, axis=0, keepdims=True) + shift)
                    a_cols.append(jnp.exp2(m_prev[:, cols] - m_c))
                    p_cols.append(jnp.exp2(s_c - (m_c - shift)).astype(BF16))
                    m_cols.append(m_c)
                pv = jnp.dot(vt, jnp.concatenate(p_cols, axis=1), preferred_element_type=F32)
                wide = slice(c0, c0 + ATTN_PV_COLS)
                acc_scr[g, :, wide] = jnp.concatenate(a_cols, axis=1) * acc_scr[g, :, wide] + pv
            maxes[g] = jnp.concatenate(m_cols, axis=1)
        return tuple(maxes)

    def far(first, count):
        return tuple((first + i, None) for i in range(count))

    far_bias = tuple(far_ref[head0 + g] * LOG2_E for g in range(group))
    n_far = jnp.maximum(qi - 1, 0)
    n_octs = n_far // 8
    n_quads = n_far // 4
    rest = n_far - 4 * n_quads
    maxes = tuple(jnp.full((1, 2 * tile), NEG_BIG, F32) for _ in range(group))
    maxes = lax.fori_loop(0, n_octs, lambda i, c: block(far(8 * i, 8), c), maxes)
    maxes = lax.cond(n_quads > 2 * n_octs, lambda: block(far(8 * n_octs, 4), maxes), lambda: maxes)
    near = ((qi - 1, 0), (qi, 1))

    def last_tiles():
        return lax.switch(rest, [functools.partial(block, far(n_far - left, left) + near, maxes)
                                 for left in range(4)])

    lax.cond(qi > 0, last_tiles, lambda: block(((qi, 1),), maxes))

    for g in range(group):
        o = acc_scr[g, :width, :] / acc_scr[g, width:width + 1, :]
        out = (o[:, :tile] - lam * o[:, tile:]).T
        gain = gain_ref[:, g * width:(g + 1) * width]
        y_ref[:, g * width:(g + 1) * width] = (_rms(out) * gain * out_scale).astype(BF16)


def _t5_bucket(n):
    max_exact = REL_BUCKETS // 2
    nf = jnp.maximum(n, 1).astype(F32)
    large = max_exact + (jnp.log(nf / max_exact) / math.log(REL_MAX_DIST / max_exact)
                         * (REL_BUCKETS - max_exact)).astype(jnp.int32)
    large = jnp.minimum(large, REL_BUCKETS - 1)
    return jnp.where(n < max_exact, n, large)


def _diff_attn(proj, q_col, k_col, v_col, rel_bias, lam_vecs, head_gain, *, batch, seq):
    n = proj.shape[0]
    heads = rel_bias.shape[1]
    hw = head_gain.shape[0] // heads
    tile = min(ATTN_TILE, seq)
    nq = seq // tile
    assert tile >= REL_MAX_DIST, "distances beyond one tile must all fall in the last bucket"
    group = ATTN_HEAD_GROUP
    gw = group * hw
    assert heads % group == 0 and q_col % group == 0 and k_col % group == 0 and v_col % group == 0

    bucket = _t5_bucket(jnp.arange(2 * tile, dtype=jnp.int32))
    by_dist = jnp.zeros((heads, 2 * tile), F32)
    for b in range(REL_BUCKETS):
        by_dist = jnp.where(bucket[None] == b, rel_bias[b].astype(F32)[:, None], by_dist)
    masked = jnp.full((heads, tile - 1), NEG_BIG, F32)
    line = jnp.concatenate([masked, by_dist * LOG2_E, masked[:, :2]], axis=1)
    length = line.shape[1] - 1
    skew = jnp.tile(line, (1, tile))[:, :tile * length].reshape(heads, tile, length)
    band = jnp.stack([skew[:, :, 2 * tile - 1:3 * tile - 1], skew[:, :, tile - 1:2 * tile - 1]], axis=1)
    far = rel_bias[REL_BUCKETS - 1].astype(F32)

    lam_init = 0.8 - 0.6 * math.exp(-0.3 * LAYER)
    return pl.pallas_call(
        functools.partial(_diff_attn_kernel, out_scale=1.0 - lam_init, lam_init=lam_init),
        grid=(batch, heads // group, nq),
        in_specs=[
            pl.BlockSpec(memory_space=pltpu.SMEM),
            pl.BlockSpec((tile, gw), lambda b, h, i: (b * nq + i, q_col // group + h)),
            pl.BlockSpec((seq, gw), lambda b, h, i: (b, k_col // group + h)),
            pl.BlockSpec((seq, gw), lambda b, h, i: (b, v_col // group + h)),
            pl.BlockSpec((group, 2, tile, tile), lambda b, h, i: (h, 0, 0, 0)),
            _resident(lam_vecs.shape),
            pl.BlockSpec((1, gw), lambda b, h, i: (0, h)),
        ],
        out_specs=pl.BlockSpec((tile, gw), lambda b, h, i: (b * nq + i, h)),
        out_shape=jax.ShapeDtypeStruct((n, heads * hw), BF16),
        scratch_shapes=[
            pltpu.VMEM((group, nq, hw + ATTN_ONES_ROWS, tile), BF16),
            pltpu.VMEM((group, hw + ATTN_ONES_ROWS, 2 * tile), F32),
            pltpu.VMEM((2 * group, tile, 2 * tile), F32),
        ],
        compiler_params=_params("parallel", "parallel", "arbitrary"),
        name="diff_attn",
    )(far, proj, proj, proj, band, lam_vecs, head_gain.reshape(1, -1))


def _mem_kv_kernel(mem_ref, g_ref, w_ref, k_ref, v_ref):
    width = k_ref.shape[-1]
    hb = (_rms(mem_ref[0]) * g_ref[...]).astype(BF16)
    k_ref[0] = jnp.dot(hb, w_ref[:, :width], preferred_element_type=F32).astype(BF16)
    v_ref[0] = jnp.dot(hb, w_ref[:, width:], preferred_element_type=F32).astype(BF16)


def _mem_kv(mem, gain, w_kv):
    batch, mlen, d = mem.shape
    width = w_kv.shape[1] // 2
    out = jax.ShapeDtypeStruct((batch, mlen, width), BF16)
    return pl.pallas_call(
        _mem_kv_kernel,
        grid=(batch,),
        in_specs=[
            pl.BlockSpec((1, mlen, d), lambda b: (b, 0, 0)),
            _resident((1, d)),
            _resident(w_kv.shape),
        ],
        out_specs=[pl.BlockSpec((1, mlen, width), lambda b: (b, 0, 0))] * 2,
        out_shape=[out, out],
        compiler_params=_params("parallel"),
        name="mem_kv",
    )(mem, gain.reshape(1, d), w_kv)


def _merge_kernel(x_ref, qx_ref, g0_ref, g1_ref, g2_ref, ym_ref, ya_ref, mem_ref, gmem_ref, wkv_ref,
                  bg_ref, wm_ref, wa_ref, wx_ref, wo_ref, o_ref, kx_ref, vx_ref):
    d = x_ref.shape[1]
    xw = qx_ref.shape[1]
    xhd = xw // X_HEADS

    @pl.when(pl.program_id(1) == 0)
    def _():
        mb = (_rms(mem_ref[0]) * gmem_ref[...]).astype(BF16)
        kx_ref[0] = jnp.dot(mb, wkv_ref[:, :xw], preferred_element_type=F32).astype(BF16)
        vx_ref[0] = jnp.dot(mb, wkv_ref[:, xw:], preferred_element_type=F32).astype(BF16)

    heads = []
    for h in range(X_HEADS):
        cols = slice(h * xhd, (h + 1) * xhd)
        s = lax.dot_general(qx_ref[:, cols], kx_ref[0, :, cols], (((1,), (1,)), ((), ())),
                            preferred_element_type=F32) * (xhd ** -0.5)
        p = jnp.exp(s - jnp.max(s, axis=1, keepdims=True))
        pv = jnp.dot(p.astype(BF16), vx_ref[0, :, cols], preferred_element_type=F32)
        heads.append((pv / jnp.sum(p, axis=1, keepdims=True)).astype(BF16))
    yx = jnp.concatenate(heads, axis=1)

    def gate(g_ref, k):
        return _sigmoid(g_ref[...].astype(F32) + bg_ref[:, k * d:(k + 1) * d])

    merged = (gate(g0_ref, 0) * jnp.dot(ym_ref[...], wm_ref[...], preferred_element_type=F32)
              + gate(g1_ref, 1) * jnp.dot(ya_ref[...], wa_ref[...], preferred_element_type=F32)
              + gate(g2_ref, 2) * jnp.dot(yx, wx_ref[...], preferred_element_type=F32))
    o_ref[...] = x_ref[...] + jnp.dot(merged.astype(BF16), wo_ref[...], preferred_element_type=F32)


def _merge(x, proj, qx_col, g_col, y_m, y_a, mem, mem_gain, w_kv, b_gate, w_m, w_a, w_x, w_o, *, batch, seq):
    n, d = x.shape
    tm = min(MERGE_ROWS, seq)
    nt = seq // tm

    def row(b, t):
        return b * nt + t

    def proj_spec(col):
        return pl.BlockSpec((tm, d), lambda b, t: (row(b, t), col))

    def mem_spec(a):
        return pl.BlockSpec((1,) + a.shape[1:], lambda b, t: (b, 0, 0))

    return pl.pallas_call(
        _merge_kernel,
        grid=(batch, nt),
        in_specs=[
            pl.BlockSpec((tm, d), lambda b, t: (row(b, t), 0)),
            proj_spec(qx_col), proj_spec(g_col), proj_spec(g_col + 1), proj_spec(g_col + 2),
            pl.BlockSpec((tm, y_m.shape[1]), lambda b, t: (row(b, t), 0)),
            pl.BlockSpec((tm, y_a.shape[1]), lambda b, t: (row(b, t), 0)),
            mem_spec(mem), _resident((1, d)), _resident(w_kv.shape),
            _resident((1, b_gate.shape[0])),
            _resident(w_m.shape), _resident(w_a.shape), _resident(w_x.shape), _resident(w_o.shape),
        ],
        out_specs=pl.BlockSpec((tm, d), lambda b, t: (row(b, t), 0)),
        out_shape=jax.ShapeDtypeStruct((n, d), F32),
        scratch_shapes=[pltpu.VMEM((1, mem.shape[1], w_kv.shape[1] // 2), BF16)] * 2,
        compiler_params=_params("parallel", "arbitrary"),
        name="merge",
    )(x, proj, proj, proj, proj, y_m, y_a, mem, mem_gain.reshape(1, d), w_kv, b_gate.reshape(1, -1),
      w_m, w_a, w_x, w_o)


def kernel(x, mem, positions, rel_bias, ffn1_norm, ffn1_w_gu, ffn1_w_down, mix_norm, mem_norm, w_in, b_if, b_gate, m_conv_w, m_conv_b, m_wq, m_wk, m_wv, m_head_norm, a_lam_q1, a_lam_k1, a_lam_q2, a_lam_k2, a_head_norm, w_mem_kv, w_branch_m, w_branch_a, w_branch_x, w_out, ffn2_norm, ffn2_w_gu, ffn2_w_down, final_norm):
    del positions
    batch, seq, d = x.shape
    n = batch * seq
    l = LAYER
    m_inner = m_conv_w.shape[-1]
    a_width = a_head_norm.shape[-1]
    x_width = w_mem_kv.shape[-1] // 2
    bf = lambda a: a.astype(BF16)

    gate_lo = 2 * m_inner
    gate_hi = gate_lo + 2 * M_HEADS
    w_all = bf(w_in[l])
    w_tail = w_all[:, gate_hi:]
    w_gate = jnp.pad(w_all[:, gate_lo:gate_hi], ((0, 0), (0, LANES - 2 * M_HEADS)))

    x0 = x.reshape(n, d)
    x1 = _ffn(x0, ffn1_norm[l], bf(ffn1_w_gu[l]), bf(ffn1_w_down[l]), final_norm, final_norm=False)

    proj, gates = _inproj(x1, mix_norm[l], w_all, gate_lo, w_tail, w_gate)

    m_hd = m_inner // M_HEADS
    q_m, k_m, v_m = _mlstm_qkv(proj, m_conv_w[l], m_conv_b[l], bf(m_wq[l]), bf(m_wk[l] * m_hd ** -0.5),
                               bf(m_wv[l]), batch=batch, seq=seq)
    gates_t = gates[:, :2 * M_HEADS].reshape(batch, seq, 2, M_HEADS).transpose(0, 3, 2, 1)
    y_m = _mlstm_scan(q_m, k_m, v_m, gates_t, b_if[l], proj, m_inner // m_hd, m_head_norm[l],
                      batch=batch, seq=seq)

    a_hw = a_width // A_HEADS
    qa_col = 2 * m_inner // a_hw
    lam_vecs = jnp.stack([a_lam_q1[l], a_lam_k1[l], a_lam_q2[l], a_lam_k2[l]])
    y_a = _diff_attn(proj, qa_col, qa_col + A_HEADS, qa_col + 2 * A_HEADS, rel_bias, lam_vecs,
                     a_head_norm[l], batch=batch, seq=seq)

    qx_col = (2 * m_inner + 3 * a_width) // d
    x2 = _merge(x1, proj, qx_col, qx_col + x_width // d, y_m, y_a, mem, mem_norm[l], bf(w_mem_kv[l]), b_gate[l],
                bf(w_branch_m[l]), bf(w_branch_a[l]), bf(w_branch_x[l]), bf(w_out[l]),
                batch=batch, seq=seq)

    out = _ffn(x2, ffn2_norm[l], bf(ffn2_w_gu[l]), bf(ffn2_w_down[l]), final_norm, final_norm=True)
    return out.reshape(batch, seq, d)
```
